```python
import jax
import jax.numpy as jnp
from jax import lax
import numpy as np

D_MODEL = 1024
BATCH = 16
SEQ = 2048
DEPTH = 2

HEAD_DIM = 64
NSA_HEADS = 6
NSA_KV_HEADS = 2
NSA_GROUP = NSA_HEADS // NSA_KV_HEADS
CMP_BLOCK = 32
CMP_STRIDE = 16
CMP_HIDDEN = 256
SLC_BLOCK = 64
SLC_TOPN = 8
WIN = 512
NSA_QB = 64
FORCE_SCORE = 1e9
DIL_PAIRS = ((128, 1), (512, 4), (2048, 16))
DIL_HPG = 2
DIL_HEADS = DIL_HPG * len(DIL_PAIRS)
DIL_QB = 128
HG_HEADS = 4
HG_DIM = 64
HG_CHUNK = 64
LB_TINY = 1e-30
D_FF = 4 * D_MODEL
PLE_DIM = 256
EPS = 1e-6
NEG = -1e30
NSA_Q_W = NSA_HEADS * HEAD_DIM
NSA_KV_W = NSA_KV_HEADS * HEAD_DIM
DIL_W = DIL_HEADS * HEAD_DIM
HG_W = HG_HEADS * HG_DIM
IN_WIDTHS = (NSA_Q_W,) + (NSA_KV_W,) * 6 + (3 * NSA_HEADS,) + (DIL_W,) * 3 + (HG_W,) * 4
IN_TOTAL = sum(IN_WIDTHS)
MIX_W = NSA_Q_W + DIL_W + HG_W

kernel_name = 'hybrid_nsa_dilated_hgrn2_parallel'


def _rmsnorm(x, g):
    xf = x.astype(jnp.float32)
    y = xf * lax.rsqrt(jnp.mean(xf * xf, axis=-1, keepdims=True) + EPS)
    return (y * g.astype(jnp.float32)).astype(x.dtype)


def _softmax_masked(s, mask):
    s = jnp.where(mask, s, NEG)
    m = jnp.max(s, axis=-1, keepdims=True)
    e = jnp.where(mask, jnp.exp(s - m), 0.0)
    den = jnp.maximum(jnp.sum(e, axis=-1, keepdims=True), 1e-30)
    return e / den, (m + jnp.log(den))[..., 0]


def _alibi_slopes():
    n = NSA_HEADS + DIL_HEADS
    s = 2.0 ** (-8.0 * np.arange(1, n + 1) / n)
    quads = s.reshape(-1, 4)
    nsa = quads[:, 2:].reshape(-1)
    dil = quads[:, :2].reshape(-1)
    return jnp.asarray(nsa, jnp.float32), jnp.asarray(dil, jnp.float32)


def _compress(x, pos, w1, w2):
    B, Hk, S, Dh = x.shape
    n_cmp = (S - CMP_BLOCK) // CMP_STRIDE + 1
    idx = np.arange(n_cmp)[:, None] * CMP_STRIDE + np.arange(CMP_BLOCK)[None, :]
    blocks = jnp.take(x, jnp.asarray(idx.reshape(-1)), axis=2).reshape(B, Hk, n_cmp, CMP_BLOCK, Dh) + pos
    flat = blocks.reshape(B, Hk, n_cmp, CMP_BLOCK * Dh)
    return jax.nn.silu(flat @ w1) @ w2


def _nsa(q, kc, vc, ks, vs, kw, vw, gate_logits, cmp_pos, cmp_w1, cmp_w2, slopes):
    f32 = jnp.float32
    B, S, _ = q.shape
    dt = q.dtype
    qh = q.reshape(B, S, NSA_KV_HEADS, NSA_GROUP, HEAD_DIM).transpose(0, 2, 3, 1, 4) * (HEAD_DIM ** -0.5)

    def kvh(t):
        return t.reshape(B, S, NSA_KV_HEADS, HEAD_DIM).transpose(0, 2, 1, 3)

    gates = jax.nn.sigmoid(gate_logits.astype(f32)).reshape(B, S, NSA_KV_HEADS, NSA_GROUP, 3).transpose(0, 2, 3, 1, 4)
    k_cmp = _compress(kvh(kc), cmp_pos[0], cmp_w1[0], cmp_w2[0])
    v_cmp = _compress(kvh(vc), cmp_pos[1], cmp_w1[1], cmp_w2[1])
    n_cmp = k_cmp.shape[2]
    n_slc = S // SLC_BLOCK
    k_top = min(SLC_TOPN, n_slc)
    cmp_start = np.arange(n_cmp) * CMP_STRIDE
    cmp_end = jnp.asarray(cmp_start + CMP_BLOCK - 1)
    cmp_ctr = jnp.asarray(cmp_start + 0.5 * (CMP_BLOCK - 1), f32)
    slc_start = np.arange(n_slc) * SLC_BLOCK
    overlap = jnp.asarray(((cmp_start[:, None] < slc_start[None, :] + SLC_BLOCK)
                           & (cmp_start[:, None] + CMP_BLOCK > slc_start[None, :])).astype(np.float32))
    ks_blk = kvh(ks).reshape(B, NSA_KV_HEADS, n_slc, SLC_BLOCK, HEAD_DIM)
    vs_blk = kvh(vs).reshape(B, NSA_KV_HEADS, n_slc, SLC_BLOCK, HEAD_DIM)
    kw_pad = jnp.pad(kvh(kw), ((0, 0), (0, 0), (WIN, 0), (0, 0)))
    vw_pad = jnp.pad(kvh(vw), ((0, 0), (0, 0), (WIN, 0), (0, 0)))
    sl = slopes.reshape(1, NSA_KV_HEADS, NSA_GROUP, 1, 1)
    b_idx = jnp.arange(B)[:, None, None, None]
    h_idx = jnp.arange(NSA_KV_HEADS)[None, :, None, None]
    blk_ids = jnp.arange(n_slc)

    def block(i):
        q0 = i * NSA_QB
        t = q0 + jnp.arange(NSA_QB)
        tf = t.astype(f32)
        qi = lax.dynamic_slice_in_dim(qh, q0, NSA_QB, axis=3)
        gi = lax.dynamic_slice_in_dim(gates, q0, NSA_QB, axis=3)
        s_c = jnp.einsum('bkgqd,bknd->bkgqn', qi, k_cmp).astype(f32) - sl * jnp.abs(tf[:, None] - cmp_ctr[None, :])
        p_c, _ = _softmax_masked(s_c, cmp_end[None, :] <= t[:, None])
        o_c = jnp.einsum('bkgqn,bknd->bkgqd', p_c.astype(dt), v_cmp)
        imp = jnp.einsum('bkgqn,nm->bkqm', p_c, overlap)
        cur = (t // SLC_BLOCK)[:, None]
        forced = (blk_ids[None, :] == 0) | (blk_ids[None, :] == cur) | (blk_ids[None, :] == cur - 1)
        valid = blk_ids[None, :] <= cur
        score = jnp.where(valid, jnp.where(forced, FORCE_SCORE, imp), -FORCE_SCORE)
        top_s, top_i = lax.top_k(score, k_top)
        k_sel = ks_blk[b_idx, h_idx, top_i]
        v_sel = vs_blk[b_idx, h_idx, top_i]
        pos = top_i[..., None] * SLC_BLOCK + jnp.arange(SLC_BLOCK)
        m_s = (top_s >= 0)[..., None] & (pos <= t[:, None, None])
        dist = tf[:, None, None] - pos.astype(f32)
        s_s = jnp.einsum('bkgqd,bkqjld->bkgqjl', qi, k_sel).astype(f32) - sl[..., None] * dist[:, :, None]
        shp = s_s.shape
        p_s, _ = _softmax_masked(s_s.reshape(shp[0], shp[1], shp[2], shp[3], -1),
                                 m_s.reshape(shp[0], shp[1], 1, shp[3], -1))
        o_s = jnp.einsum('bkgqjl,bkqjld->bkgqd', p_s.reshape(shp).astype(dt), v_sel)
        kwi = lax.dynamic_slice_in_dim(kw_pad, q0, NSA_QB + WIN, axis=2)
        vwi = lax.dynamic_slice_in_dim(vw_pad, q0, NSA_QB + WIN, axis=2)
        kp = q0 - WIN + jnp.arange(NSA_QB + WIN)
        rel = t[:, None] - kp[None, :]
        m_w = (kp[None, :] >= 0) & (rel >= 0) & (rel < WIN)
        s_w = jnp.einsum('bkgqd,bknd->bkgqn', qi, kwi).astype(f32) - sl * rel.astype(f32)
        p_w, _ = _softmax_masked(s_w, m_w)
        o_w = jnp.einsum('bkgqn,bknd->bkgqd', p_w.astype(dt), vwi)
        o = gi[..., 0:1] * o_c + gi[..., 1:2] * o_s + gi[..., 2:3] * o_w
        return o.astype(dt)

    out = lax.map(block, jnp.arange(S // NSA_QB))
    return out.transpose(1, 0, 4, 2, 3, 5).reshape(B, S, NSA_Q_W)


def _dilated(q, k, v, slopes):
    f32 = jnp.float32
    B, S, _ = q.shape
    dt = q.dtype

    def heads(t):
        return t.reshape(B, S, DIL_HEADS, HEAD_DIM).transpose(0, 2, 1, 3)

    qh = heads(q) * (HEAD_DIM ** -0.5)
    kh = heads(k)
    vh = heads(v)
    groups = []
    for g, (w, d) in enumerate(DIL_PAIRS):
        hs = slice(g * DIL_HPG, (g + 1) * DIL_HPG)
        groups.append((qh[:, hs], kh[:, hs], vh[:, hs], slopes[hs], w, d))

    def block(i):
        q0 = i * DIL_QB
        t = q0 + jnp.arange(DIL_QB)
        outs, lses = [], []
        for qg, kg, vg, sg, w, d in groups:
            n_keys = w // d + 1
            dist = jnp.arange(n_keys) * d
            pos = t[:, None] - dist[None, :]
            idx = jnp.maximum(pos, 0).reshape(-1)
            qi = lax.dynamic_slice_in_dim(qg, q0, DIL_QB, axis=2)
            ki = jnp.take(kg, idx, axis=2).reshape(B, DIL_HPG, DIL_QB, n_keys, HEAD_DIM)
            vi = jnp.take(vg, idx, axis=2).reshape(B, DIL_HPG, DIL_QB, n_keys, HEAD_DIM)
            s = jnp.einsum('bhqd,bhqjd->bhqj', qi, ki).astype(f32) - sg[None, :, None, None] * dist.astype(f32)
            p, lse = _softmax_masked(s, pos >= 0)
            outs.append(jnp.einsum('bhqj,bhqjd->bhqd', p.astype(dt), vi))
            lses.append(lse)
        alpha = jax.nn.softmax(jnp.stack(lses), axis=0)
        return jnp.concatenate([outs[g] * alpha[g][..., None].astype(dt) for g in range(len(outs))], axis=1)

    out = lax.map(block, jnp.arange(S // DIL_QB))
    return out.transpose(1, 0, 3, 2, 4).reshape(B, S, DIL_W)


def _hgrn2(q, f_logit, inp, g, lb):
    f32 = jnp.float32
    B, S, _ = q.shape
    dt = q.dtype

    def heads(t):
        return t.astype(f32).reshape(B, S, HG_HEADS, HG_DIM).transpose(0, 2, 1, 3)

    qh, flh, vh = heads(q), heads(f_logit), heads(inp)
    lbh = lb.astype(f32).reshape(1, HG_HEADS, 1, HG_DIM)
    log_f = jnp.logaddexp(jnp.log(lbh + LB_TINY), jnp.log1p(-lbh) + jax.nn.log_sigmoid(flh))
    kh = -jnp.expm1(log_f)
    nc = S // HG_CHUNK

    def chunks(t):
        return t.reshape(B, HG_HEADS, nc, HG_CHUNK, HG_DIM).transpose(2, 0, 1, 3, 4)

    ar = jnp.arange(HG_CHUNK)
    causal = (ar[:, None] >= ar[None, :])[:, :, None]

    def step(state, xs):
        qc, kc, vc, lfc = xs
        b = jnp.cumsum(lfc, axis=2)
        diff = jnp.where(causal, b[:, :, :, None, :] - b[:, :, None, :, :], 0.0)
        decay = jnp.where(causal, jnp.exp(diff), 0.0)
        a = jnp.einsum('bhtd,bhsd,bhtsd->bhts', qc, kc, decay)
        o = jnp.einsum('bhts,bhse->bhte', a, vc) + jnp.einsum('bhtd,bhde->bhte', qc * jnp.exp(b), state)
        b_last = b[:, :, -1:, :]
        new_state = jnp.exp(b_last[:, :, 0, :])[..., None] * state + jnp.einsum('bhsd,bhse->bhde', kc * jnp.exp(b_last - b), vc)
        return new_state, o

    s0 = jnp.zeros((B, HG_HEADS, HG_DIM, HG_DIM), f32)
    _, o = lax.scan(step, s0, (chunks(qh), chunks(kh), chunks(vh), chunks(log_f)))
    o = o.transpose(1, 2, 0, 3, 4).reshape(B, HG_HEADS, S, HG_DIM)
    o = o * lax.rsqrt(jnp.mean(o * o, axis=-1, keepdims=True) + EPS)
    o = o.transpose(0, 2, 1, 3).reshape(B, S, HG_W) * jax.nn.silu(g.astype(f32))
    return o.astype(dt)


def setup_inputs(seed: int = 0) -> dict:
    key = jax.random.key(seed)
    ks = jax.random.split(key, 18)
    f32 = jnp.float32

    def nrm(k, shape, fan_in):
        return jax.random.normal(k, shape, f32) * (fan_in ** -0.5)

    def gain(k, shape):
        return 1.0 + 0.05 * jax.random.normal(k, shape, f32)

    return {
        'x': jax.random.normal(ks[0], (BATCH, SEQ, D_MODEL), f32),
        'p': jax.random.normal(ks[1], (DEPTH, BATCH, SEQ, PLE_DIM), f32),
        'w_in': nrm(ks[2], (DEPTH, D_MODEL, IN_TOTAL), D_MODEL),
        'w_out': nrm(ks[3], (DEPTH, MIX_W, D_MODEL), MIX_W),
        'cmp_pos': 0.02 * jax.random.normal(ks[4], (DEPTH, 2, CMP_BLOCK, HEAD_DIM), f32),
        'cmp_w1': nrm(ks[5], (DEPTH, 2, CMP_BLOCK * HEAD_DIM, CMP_HIDDEN), CMP_BLOCK * HEAD_DIM),
        'cmp_w2': nrm(ks[6], (DEPTH, 2, CMP_HIDDEN, HEAD_DIM), CMP_HIDDEN),
        'hg_lb': 0.1 * jax.random.normal(ks[7], (DEPTH, HG_W), f32),
        'g_pre_mix': gain(ks[8], (DEPTH, D_MODEL)),
        'g_post_mix': gain(ks[9], (DEPTH, D_MODEL)),
        'g_pre_mlp': gain(ks[10], (DEPTH, D_MODEL)),
        'g_post_mlp': gain(ks[11], (DEPTH, D_MODEL)),
        'w_up': nrm(ks[12], (DEPTH, D_MODEL, D_FF), D_MODEL),
        'w_down': nrm(ks[13], (DEPTH, D_FF, D_MODEL), D_FF),
        'g_ple': gain(ks[14], (DEPTH, D_MODEL)),
        'w_ple_gate': nrm(ks[15], (DEPTH, D_MODEL, D_MODEL), D_MODEL),
        'w_ple_proj': nrm(ks[16], (DEPTH, PLE_DIM, D_MODEL), PLE_DIM),
    }


def reference(x, p, w_in, w_out, cmp_pos, cmp_w1, cmp_w2, hg_lb, g_pre_mix, g_post_mix,
              g_pre_mlp, g_post_mlp, w_up, w_down, g_ple, w_ple_gate, w_ple_proj):
    nsa_slopes, dil_slopes = _alibi_slopes()
    sm = jax.nn.softmax(hg_lb.astype(jnp.float32), axis=0)
    lower_bounds = jnp.maximum(jnp.cumsum(sm, axis=0) - sm[0:1], 0.0)
    split_pts = [int(v) for v in np.cumsum(IN_WIDTHS)[:-1]]
    h = x
    for l in range(DEPTH):
        hn = _rmsnorm(h, g_pre_mix[l])
        proj = hn @ w_in[l]
        (nq, nkc, nvc, nks, nvs, nkw, nvw, ngate, dq, dk, dv, hq, hf, hi, hg) = jnp.split(proj, split_pts, axis=-1)
        o_nsa = _nsa(nq, nkc, nvc, nks, nvs, nkw, nvw, ngate, cmp_pos[l], cmp_w1[l], cmp_w2[l], nsa_slopes)
        o_dil = _dilated(dq, dk, dv, dil_slopes)
        o_hg = _hgrn2(hq, hf, hi, hg, lower_bounds[l])
        mix = jnp.concatenate([o_nsa, o_dil, o_hg], axis=-1) @ w_out[l]
        h = h + _rmsnorm(mix, g_post_mix[l])
        hn = _rmsnorm(h, g_pre_mlp[l])
        u = jnp.square(jax.nn.relu(hn @ w_up[l]))
        h = h + _rmsnorm(u @ w_down[l], g_post_mlp[l])
        gate = jax.nn.sigmoid(_rmsnorm(h, g_ple[l]) @ w_ple_gate[l])
        h = h + (p[l] @ w_ple_proj[l]) * gate
    return h
```

```python
import functools

import numpy as np
import jax
import jax.numpy as jnp
from jax import lax
from jax.experimental import pallas as pl
from jax.experimental.pallas import tpu as pltpu

F32 = jnp.float32
BF16 = jnp.bfloat16

D_MODEL = 1024
DEPTH = 2
HEAD_DIM = 64
NSA_HEADS = 6
NSA_KV_HEADS = 2
NSA_GROUP = NSA_HEADS // NSA_KV_HEADS
CMP_BLOCK = 32
CMP_STRIDE = 16
CMP_HIDDEN = 256
SLC_BLOCK = 64
SLC_TOPN = 8
WIN = 512
FORCE_SCORE = 1e9
DIL_PAIRS = ((128, 1), (512, 4), (2048, 16))
DIL_HPG = 2
DIL_HEADS = DIL_HPG * len(DIL_PAIRS)
HG_HEADS = 4
HG_DIM = 64
LB_TINY = 1e-30
D_FF = 4 * D_MODEL
PLE_DIM = 256
EPS = 1e-6
NEG = -1e30

NSA_Q_W = NSA_HEADS * HEAD_DIM
NSA_KV_W = NSA_KV_HEADS * HEAD_DIM
DIL_W = DIL_HEADS * HEAD_DIM
HG_W = HG_HEADS * HG_DIM
IN_WIDTHS = (NSA_Q_W,) + (NSA_KV_W,) * 6 + (3 * NSA_HEADS,) + (DIL_W,) * 3 + (HG_W,) * 4
IN_TOTAL = sum(IN_WIDTHS)
IN_OFF = tuple(int(v) for v in np.cumsum((0,) + IN_WIDTHS))
(OFF_NQ, OFF_NKC, OFF_NVC, OFF_NKS, OFF_NVS, OFF_NKW, OFF_NVW, OFF_GATE,
 OFF_DQ, OFF_DK, OFF_DV, OFF_HQ, OFF_HF, OFF_HI, OFF_HG) = IN_OFF[:-1]

LANE = 128
VMEM_LIMIT = 56 * 1024 * 1024

P16_NQ = 0
P16_KVS = P16_NQ + NSA_HEADS * LANE
P16_KVW = P16_KVS + NSA_KV_HEADS * LANE
P16_DIL = P16_KVW + NSA_KV_HEADS * LANE
DIL_GROUP_W = 2 * DIL_HPG * LANE
W16 = P16_DIL + len(DIL_PAIRS) * DIL_GROUP_W
PF_HQ, PF_HF, PF_HI, PF_HG, PF_GATE = 0, HG_W, 2 * HG_W, 3 * HG_W, 4 * HG_W
WF = PF_GATE + LANE
WC = 2 * NSA_KV_W
MIX_DIL_W = DIL_HPG * LANE


def _dot(a, b):
    return jnp.dot(a, b, preferred_element_type=F32)


def _dot_nt(a, b):
    return lax.dot_general(a, b, (((1,), (1,)), ((), ())), preferred_element_type=F32)


def _rms(x, g):
    return x * lax.rsqrt(jnp.mean(x * x, axis=-1, keepdims=True) + EPS) * g


def _log2(n):
    l = int(n).bit_length() - 1
    assert (1 << l) == n, n
    return l


def _alibi_slopes():
    n = NSA_HEADS + DIL_HEADS
    s = 2.0 ** (-8.0 * np.arange(1, n + 1) / n)
    quads = s.reshape(-1, 4)
    nsa = [float(np.float32(v)) for v in quads[:, 2:].reshape(-1)]
    dil = [float(np.float32(v)) for v in quads[:, :2].reshape(-1)]
    return nsa, dil


def _inproj_body(x_ref, g_ref, w16_ref, wf_ref, wc_ref,
                 o16_ref, of_ref, kc0_ref, kc1_ref, vc0_ref, vc1_ref):
    hn = _rms(x_ref[...], g_ref[...]).astype(BF16)
    for c in range(0, W16, 256):
        o16_ref[:, c:c + 256] = _dot(hn, w16_ref[:, c:c + 256]).astype(BF16)
    for c in range(0, WF, 384):
        of_ref[:, c:c + 384] = _dot(hn, wf_ref[:, c:c + 384])
    cc = _dot(hn, wc_ref[...])
    kc0_ref[...] = cc[:, 0:64]
    kc1_ref[...] = cc[:, 64:128]
    vc0_ref[...] = cc[:, 128:192]
    vc1_ref[...] = cc[:, 192:256]


def _inproj(x2, g, w16, wf, wc, tm):
    T = x2.shape[0]
    row = lambda i: (i, 0)
    full = lambda i: (0, 0)
    return pl.pallas_call(
        _inproj_body,
        grid=(T // tm,),
        in_specs=[pl.BlockSpec((tm, D_MODEL), row),
                  pl.BlockSpec((1, D_MODEL), full),
                  pl.BlockSpec((D_MODEL, W16), full),
                  pl.BlockSpec((D_MODEL, WF), full),
                  pl.BlockSpec((D_MODEL, WC), full)],
        out_specs=[pl.BlockSpec((tm, W16), row), pl.BlockSpec((tm, WF), row)]
        + [pl.BlockSpec((tm, HEAD_DIM), row)] * 4,
        out_shape=[jax.ShapeDtypeStruct((T, W16), BF16), jax.ShapeDtypeStruct((T, WF), F32)]
        + [jax.ShapeDtypeStruct((T, HEAD_DIM), F32)] * 4,
        compiler_params=pltpu.CompilerParams(dimension_semantics=("arbitrary",),
                                             vmem_limit_bytes=VMEM_LIMIT),
        name="inproj",
    )(x2, g, w16, wf, wc)


def _compress_body(kc0_ref, kc1_ref, vc0_ref, vc1_ref, pos_ref, w1_ref, w2_ref, out_ref, *, n_cmp):
    nr = kc0_ref.shape[1]
    half = (CMP_BLOCK // 2) * HEAD_DIM
    rows = lax.broadcasted_iota(jnp.int32, (nr, 1), 0)
    out_ref[...] = jnp.zeros(out_ref.shape, out_ref.dtype)
    srcs = ((kc0_ref, vc0_ref), (kc1_ref, vc1_ref))
    for h in range(NSA_KV_HEADS):
        acc = jnp.zeros((nr, LANE), F32)
        for ten in range(2):
            x = srcs[h][ten][0].astype(BF16)
            first = _dot(x, w1_ref[ten, 0:half, :])
            second = _dot(x, w1_ref[ten, half:2 * half, :])
            posb = _dot(pos_ref[ten], w1_ref[ten])[0:1, :]
            hid = first + pltpu.roll(second, nr - 1, 0) + posb
            act = hid * jax.nn.sigmoid(hid)
            acc = acc + _dot(act.astype(BF16), w2_ref[ten])
        acc = jnp.where(rows < n_cmp, acc, 0.0)
        out_ref[0, 0:nr, h * LANE:(h + 1) * LANE] = acc.astype(BF16)


def _compress(kc0, kc1, vc0, vc1, pos8, w1, w2p, ncp, n_cmp):
    B, nr, kw = kc0.shape
    seq = lambda b: (b, 0, 0)
    full3 = lambda b: (0, 0, 0)
    return pl.pallas_call(
        functools.partial(_compress_body, n_cmp=n_cmp),
        grid=(B,),
        in_specs=[pl.BlockSpec((1, nr, kw), seq)] * 4
        + [pl.BlockSpec(pos8.shape, full3), pl.BlockSpec(w1.shape, full3), pl.BlockSpec(w2p.shape, full3)],
        out_specs=pl.BlockSpec((1, ncp, NSA_KV_HEADS * LANE), seq),
        out_shape=jax.ShapeDtypeStruct((B, ncp, NSA_KV_HEADS * LANE), BF16),
        compiler_params=pltpu.CompilerParams(dimension_semantics=("arbitrary",),
                                             vmem_limit_bytes=VMEM_LIMIT),
        name="nsa_compress",
    )(kc0, kc1, vc0, vc1, pos8, w1, w2p)


def _nsa_body(q_ref, kvs_ref, kvw_ref, kvc_ref, gate_ref, e_ref, out_ref, s_scr, mb_scr,
              *, seq, tq, tk, n_cmp, n_slc, k_top, slopes):
    kh = pl.program_id(1)
    qi = pl.program_id(2)
    t0 = qi * tq
    ncp = kvc_ref.shape[1]
    ltk = _log2(tk)
    ii = lax.broadcasted_iota(jnp.int32, (tq, 1), 0)
    row_t = t0 + ii
    slope = [jnp.where(kh == 0, slopes[g], slopes[NSA_GROUP + g]).astype(F32) for g in range(NSA_GROUP)]
    gates = jax.nn.sigmoid(gate_ref[0])

    def gate_col(g, c):
        j0 = g * 3 + c
        j1 = (NSA_GROUP + g) * 3 + c
        return jnp.where(kh == 0, gates[:, j0:j0 + 1], gates[:, j1:j1 + 1])

    q = [q_ref[0, :, g * LANE:(g + 1) * LANE] * (HEAD_DIM ** -0.5) for g in range(NSA_GROUP)]

    kvc = kvc_ref[0]
    nn = lax.broadcasted_iota(jnp.int32, (1, ncp), 1)
    maskc = ((nn * CMP_STRIDE + (CMP_BLOCK - 1)) <= row_t) & (nn < n_cmp)
    absd = jnp.abs(row_t.astype(F32) - (nn.astype(F32) * CMP_STRIDE + 0.5 * (CMP_BLOCK - 1)))
    psum = jnp.zeros((tq, ncp), F32)
    o_cmp = []
    for g in range(NSA_GROUP):
        s = _dot_nt(q[g], kvc) - slope[g] * absd
        s = jnp.where(maskc, s, NEG)
        m = jnp.max(s, axis=-1, keepdims=True)
        e = jnp.where(maskc, jnp.exp(s - m), 0.0)
        p = e / jnp.maximum(jnp.sum(e, axis=-1, keepdims=True), 1e-30)
        o_cmp.append(_dot(p.astype(BF16), kvc))
        psum = psum + p

    ni = lax.broadcasted_iota(jnp.int32, (ncp, 1), 0)
    mj = lax.broadcasted_iota(jnp.int32, (1, LANE), 1)
    ov = ((ni * CMP_STRIDE < mj * SLC_BLOCK + SLC_BLOCK) & (ni * CMP_STRIDE + CMP_BLOCK > mj * SLC_BLOCK)
          & (ni < n_cmp) & (mj < n_slc))
    ovb = jnp.where(ov, 1.0, 0.0).astype(BF16)
    p_hi = psum.astype(BF16)
    p_lo = (psum - p_hi.astype(F32)).astype(BF16)
    imp = _dot(p_hi, ovb) + _dot(p_lo, ovb)
    cur = row_t >> _log2(SLC_BLOCK)
    in_rng = mj < n_slc
    valid = (mj <= cur) & in_rng
    forced = (mj == 0) | (mj == cur) | (mj == cur - 1)
    score = jnp.where(valid, jnp.where(forced, FORCE_SCORE, imp), -FORCE_SCORE)
    score = jnp.where(in_rng, score, -3.0 * FORCE_SCORE)
    rank = jnp.zeros((tq, LANE), F32)
    for mp in range(n_slc):
        col = score[:, mp:mp + 1]
        beats = (col > score) | ((col == score) & (mp < mj))
        rank = rank + jnp.where(beats, 1.0, 0.0)
    sel = (rank < k_top) & valid
    selb = jnp.where(sel, 1.0, 0.0).astype(BF16)

    n_kt = (t0 + tq + tk - 1) >> ltk
    for kt in range(seq // tk):
        @pl.when(kt < n_kt)
        def _():
            mb_scr[kt] = (_dot(selb, e_ref[:, kt * tk:(kt + 1) * tk]) - 1.0) * 1e30

    jj = lax.broadcasted_iota(jnp.int32, (1, tk), 1)
    dmat = (jj - ii).astype(F32)
    kt_lo_win = jnp.maximum(t0 - (WIN - 1), 0) >> ltk

    def attend(kv_ref, qg, dg, slope_g, lo, sel_branch):
        def scores(kt, m_acc):
            kv = kv_ref[0, pl.ds(pl.multiple_of(kt * tk, tk), tk), :]
            s = _dot_nt(qg, kv) + dg + slope_g * (kt * tk - t0).astype(F32)
            rel = row_t - (jj + kt * tk)
            if sel_branch:
                s = jnp.where(rel >= 0, s + mb_scr[kt], NEG)
            else:
                s = jnp.where((rel >= 0) & (rel < WIN), s, NEG)
            s_scr[kt] = s
            return jnp.maximum(m_acc, s)

        m_acc = lax.fori_loop(lo, n_kt, scores, jnp.full((tq, tk), NEG, F32))
        m = jnp.max(m_acc, axis=-1, keepdims=True)

        def accum(kt, carry):
            l_acc, acc = carry
            p = jnp.exp(s_scr[kt] - m)
            kv = kv_ref[0, pl.ds(pl.multiple_of(kt * tk, tk), tk), :]
            return l_acc + p, acc + _dot(p.astype(BF16), kv)

        l_acc, acc = lax.fori_loop(lo, n_kt, accum,
                                   (jnp.zeros((tq, tk), F32), jnp.zeros((tq, LANE), F32)))
        return acc / jnp.sum(l_acc, axis=-1, keepdims=True)

    for g in range(NSA_GROUP):
        dg = slope[g] * dmat
        o_sel = attend(kvs_ref, q[g], dg, slope[g], 0, True)
        o_win = attend(kvw_ref, q[g], dg, slope[g], kt_lo_win, False)
        o = gate_col(g, 0) * o_cmp[g] + gate_col(g, 1) * o_sel + gate_col(g, 2) * o_win
        out_ref[0, :, g * LANE:(g + 1) * LANE] = o.astype(BF16)


def _nsa(p16, pf, cmp_kv, e_sel, seq, tq, tk, nsa_slopes):
    B = p16.shape[0]
    n_cmp = (seq - CMP_BLOCK) // CMP_STRIDE + 1
    n_slc = seq // SLC_BLOCK
    ncp = cmp_kv.shape[1]
    qw = NSA_GROUP * LANE
    body = functools.partial(_nsa_body, seq=seq, tq=tq, tk=tk, n_cmp=n_cmp, n_slc=n_slc,
                             k_top=min(SLC_TOPN, n_slc), slopes=tuple(nsa_slopes))
    return pl.pallas_call(
        body,
        grid=(B, NSA_KV_HEADS, seq // tq),
        in_specs=[pl.BlockSpec((1, tq, qw), lambda b, k, i: (b, i, k)),
                  pl.BlockSpec((1, seq, LANE), lambda b, k, i: (b, 0, P16_KVS // LANE + k)),
                  pl.BlockSpec((1, seq, LANE), lambda b, k, i: (b, 0, P16_KVW // LANE + k)),
                  pl.BlockSpec((1, ncp, LANE), lambda b, k, i: (b, 0, k)),
                  pl.BlockSpec((1, tq, LANE), lambda b, k, i: (b, i, PF_GATE // LANE)),
                  pl.BlockSpec(e_sel.shape, lambda b, k, i: (0, 0))],
        out_specs=pl.BlockSpec((1, tq, qw), lambda b, k, i: (b, i, k)),
        out_shape=jax.ShapeDtypeStruct((B, seq, NSA_HEADS * LANE), BF16),
        scratch_shapes=[pltpu.VMEM((seq // tk, tq, tk), F32), pltpu.VMEM((seq // tk, tq, tk), F32)],
        compiler_params=pltpu.CompilerParams(dimension_semantics=("arbitrary",) * 3,
                                             vmem_limit_bytes=VMEM_LIMIT),
        name="nsa_attention",
    )(p16, p16, p16, cmp_kv, pf, e_sel)


def _dilated_body(q_ref, kv_ref, o_ref, lse_ref, *, ls, dil, win_keys, slopes, tq):
    ii = lax.broadcasted_iota(jnp.int32, (tq, 1), 0)
    jj = lax.broadcasted_iota(jnp.int32, (1, 2 * tq), 1)
    rel = ii - jj + tq
    in_win = (rel >= 0) & (rel <= win_keys)
    relf = rel.astype(F32) * float(dil)
    for h in range(DIL_HPG):
        lanes = slice(h * LANE, (h + 1) * LANE)
        bias = slopes[h] * relf

        def tile(t, carry, lanes=lanes, bias=bias):
            u0 = pl.multiple_of(t * tq, tq)
            prev = pl.multiple_of(jnp.maximum(u0 - tq, 0), tq)
            qt = q_ref[0, pl.ds(u0, tq), lanes] * (HEAD_DIM ** -0.5)
            kv = jnp.concatenate([kv_ref[0, pl.ds(prev, tq), lanes], kv_ref[0, pl.ds(u0, tq), lanes]], axis=0)
            ok = in_win & ((jj >= tq) | (t > 0))
            s = jnp.where(ok, _dot_nt(qt, kv) - bias, NEG)
            m = jnp.max(s, axis=-1, keepdims=True)
            e = jnp.exp(s - m)
            l = jnp.sum(e, axis=-1, keepdims=True)
            o = _dot(e.astype(BF16), kv) / l
            o_ref[0, pl.ds(u0, tq), lanes] = o.astype(BF16)
            lse_ref[0, pl.ds(u0, tq), lanes] = jnp.broadcast_to(m + jnp.log(l), (tq, LANE))
            return carry

        lax.fori_loop(0, ls // tq, tile, 0)


def _dilated(p16, seq, group, slopes, tq):
    B = p16.shape[0]
    win, dil = DIL_PAIRS[group]
    ls = seq // dil
    tq = min(tq, ls)
    view = p16.reshape(B, ls, dil * W16)
    blk = MIX_DIL_W
    per_tok = W16 // blk
    qcol = (P16_DIL + group * DIL_GROUP_W) // blk
    body = functools.partial(_dilated_body, ls=ls, dil=dil, win_keys=win // dil,
                             slopes=tuple(slopes[group * DIL_HPG:(group + 1) * DIL_HPG]), tq=tq)
    o, lse = pl.pallas_call(
        body,
        grid=(B, dil),
        in_specs=[pl.BlockSpec((1, ls, blk), lambda b, r: (b, 0, r * per_tok + qcol)),
                  pl.BlockSpec((1, ls, blk), lambda b, r: (b, 0, r * per_tok + qcol + 1))],
        out_specs=[pl.BlockSpec((1, ls, blk), lambda b, r: (b, 0, r))] * 2,
        out_shape=[jax.ShapeDtypeStruct((B, ls, dil * blk), BF16),
                   jax.ShapeDtypeStruct((B, ls, dil * blk), F32)],
        compiler_params=pltpu.CompilerParams(dimension_semantics=("arbitrary",) * 2,
                                             vmem_limit_bytes=VMEM_LIMIT),
        name=f"dilated_attention_g{group}",
    )(view, view)
    return o.reshape(B * seq, blk), lse.reshape(B * seq, blk)


def _hgrn2_body(q_ref, f_ref, i_ref, g_ref, lb_ref, o_ref, *, layer, seq, chunk):
    c = chunk
    sub = 8
    lbs = lb_ref[...].astype(F32)
    mx = jnp.max(lbs, axis=0, keepdims=True)
    ex = jnp.exp(lbs - mx)
    sm = ex / jnp.sum(ex, axis=0, keepdims=True)
    lower = jnp.maximum(jnp.sum(sm[0:layer + 1], axis=0, keepdims=True) - sm[0:1], 0.0)
    log_lb = jnp.log(lower + LB_TINY)
    log_1m = jnp.log1p(-lower)

    lane = lax.broadcasted_iota(jnp.int32, (1, LANE), 1)
    head0 = lane < HG_DIM
    ri = lax.broadcasted_iota(jnp.int32, (c, 1), 0)
    ci = lax.broadcasted_iota(jnp.int32, (1, c), 1)
    tri = jnp.where(ci <= ri, 1.0, 0.0).astype(BF16)
    di = lax.broadcasted_iota(jnp.int32, (LANE, 1), 0)
    same_head = (di >= HG_DIM) == (lane >= HG_DIM)
    ones_blk = jnp.where(same_head, 1.0, 0.0).astype(BF16)
    gcol = lax.broadcasted_iota(jnp.int32, (1, sub * c), 1)
    gsum = jnp.where(((gcol >> _log2(c)) == (ri & (sub - 1)))
                     & (((gcol & (c - 1)) >> 3) == (ri >> 3)), 1.0, 0.0).astype(BF16)
    sp = lax.broadcasted_iota(jnp.int32, (1, sub, 1), 1)
    levels = []
    w = sub
    while w < c:
        same = (ri >> _log2(2 * w)) == (ci >> _log2(2 * w))
        levels.append((w, same & ((ri & (2 * w - 1)) >= w) & ((ci & (2 * w - 1)) < w)))
        w *= 2

    def split3(x):
        hi = x.astype(BF16)
        r1 = x - hi.astype(F32)
        mid = r1.astype(BF16)
        lo = (r1 - mid.astype(F32)).astype(BF16)
        return hi, mid, lo

    def step(ic, state_t):
        rows = pl.ds(pl.multiple_of(ic * c, c), c)
        q = q_ref[0, rows, :]
        v = i_ref[0, rows, :]
        x = f_ref[0, rows, :]
        log_sig = jnp.minimum(x, 0.0) - jnp.log1p(jnp.exp(-jnp.abs(x)))
        t2 = log_1m + log_sig
        lf = jnp.maximum(log_lb, t2) + jnp.log1p(jnp.exp(-jnp.abs(log_lb - t2)))
        kk = 1.0 - jnp.exp(lf)
        hi, mid, lo = split3(lf)
        b = _dot(tri, hi) + _dot(tri, mid) + _dot(tri, lo)
        vb = v.astype(BF16)

        q3 = q.reshape(c // sub, sub, LANE)
        k3 = kk.reshape(c // sub, sub, LANE)
        b3 = b.reshape(c // sub, sub, LANE)
        parts = []
        for tp in range(sub):
            dec = jnp.exp(jnp.minimum(b3[:, tp:tp + 1, :] - b3, 0.0))
            parts.append(jnp.where(sp <= tp, q3[:, tp:tp + 1, :] * k3 * dec, 0.0).reshape(c, LANE))
        wall = jnp.concatenate(parts, axis=0)
        a_rep = _dot(wall.astype(BF16), ones_blk)
        z = a_rep * jnp.concatenate([v] * sub, axis=0)
        o = _dot(gsum, z.astype(BF16))

        a0 = jnp.zeros((c, c), F32)
        a1 = jnp.zeros((c, c), F32)
        for w, lmask in levels:
            b_r = b.reshape(c // (2 * w), 2 * w, LANE)
            bnd = jnp.broadcast_to(b_r[:, w - 1:w, :], b_r.shape).reshape(c, LANE)
            qe = q * jnp.exp(jnp.minimum(b - bnd, 0.0))
            ke = (kk * jnp.exp(jnp.minimum(bnd - b, 0.0))).astype(BF16)
            a0 = a0 + jnp.where(lmask, _dot_nt(jnp.where(head0, qe, 0.0).astype(BF16), ke), 0.0)
            a1 = a1 + jnp.where(lmask, _dot_nt(jnp.where(head0, 0.0, qe).astype(BF16), ke), 0.0)
        o = o + jnp.where(head0, _dot(a0.astype(BF16), vb), _dot(a1.astype(BF16), vb))

        o = o + _dot_nt((q * jnp.exp(b)).astype(BF16), state_t.astype(BF16))
        b_last = b[c - 1:c, :]
        khat = (kk * jnp.exp(b_last - b)).astype(BF16)
        upd = lax.dot_general(vb, khat, (((0,), (0,)), ((), ())), preferred_element_type=F32)
        state_t = jnp.exp(b_last) * state_t + jnp.where(same_head, upd, 0.0)

        o2 = o * o
        ms0 = jnp.sum(jnp.where(head0, o2, 0.0), axis=-1, keepdims=True)
        ms1 = jnp.sum(jnp.where(head0, 0.0, o2), axis=-1, keepdims=True)
        o = o * lax.rsqrt(jnp.where(head0, ms0, ms1) * (1.0 / HG_DIM) + EPS)
        gt = g_ref[0, rows, :]
        o_ref[0, rows, :] = (o * (gt * jax.nn.sigmoid(gt))).astype(BF16)
        return state_t

    lax.fori_loop(0, seq // c, step, jnp.zeros((LANE, LANE), F32))


def _hgrn2(pf, hg_lb, layer, seq, chunk):
    B = pf.shape[0]
    npair = HG_W // LANE
    sec = lambda off: (lambda b, p: (b, 0, off // LANE + p))
    return pl.pallas_call(
        functools.partial(_hgrn2_body, layer=layer, seq=seq, chunk=chunk),
        grid=(B, npair),
        in_specs=[pl.BlockSpec((1, seq, LANE), sec(PF_HQ)),
                  pl.BlockSpec((1, seq, LANE), sec(PF_HF)),
                  pl.BlockSpec((1, seq, LANE), sec(PF_HI)),
                  pl.BlockSpec((1, seq, LANE), sec(PF_HG)),
                  pl.BlockSpec((DEPTH, LANE), lambda b, p: (0, p))],
        out_specs=pl.BlockSpec((1, seq, LANE), lambda b, p: (b, 0, p)),
        out_shape=jax.ShapeDtypeStruct((B, seq, HG_W), BF16),
        compiler_params=pltpu.CompilerParams(dimension_semantics=("arbitrary",) * 2,
                                             vmem_limit_bytes=VMEM_LIMIT),
        name="hgrn2",
    )(pf, pf, pf, pf, hg_lb)


def _outproj_body(nsa_ref, d0_ref, d1_ref, d2_ref, l0_ref, l1_ref, l2_ref, hg_ref, h_ref,
                  wn_ref, wd_ref, wh_ref, g_ref, out_ref):
    ls = [l0_ref[...], l1_ref[...], l2_ref[...]]
    lm = jnp.maximum(jnp.maximum(ls[0], ls[1]), ls[2])
    es = [jnp.exp(l - lm) for l in ls]
    inv = 1.0 / (es[0] + es[1] + es[2])
    acc = _dot(nsa_ref[...], wn_ref[...])
    for g, d_ref in enumerate((d0_ref, d1_ref, d2_ref)):
        acc = acc + _dot((d_ref[...].astype(F32) * (es[g] * inv)).astype(BF16), wd_ref[g])
    acc = acc + _dot(hg_ref[...], wh_ref[...])
    out_ref[...] = h_ref[...] + _rms(acc, g_ref[...])


def _outproj(nsa, dil_o, dil_l, hg, h2, wn, wd, wh, g, tm):
    T = h2.shape[0]
    row = lambda i: (i, 0)
    full = lambda i: (0, 0)
    return pl.pallas_call(
        _outproj_body,
        grid=(T // tm,),
        in_specs=[pl.BlockSpec((tm, NSA_HEADS * LANE), row)]
        + [pl.BlockSpec((tm, MIX_DIL_W), row)] * 6
        + [pl.BlockSpec((tm, HG_W), row), pl.BlockSpec((tm, D_MODEL), row),
           pl.BlockSpec(wn.shape, full), pl.BlockSpec(wd.shape, lambda i: (0, 0, 0)),
           pl.BlockSpec(wh.shape, full), pl.BlockSpec((1, D_MODEL), full)],
        out_specs=pl.BlockSpec((tm, D_MODEL), row),
        out_shape=jax.ShapeDtypeStruct((T, D_MODEL), F32),
        compiler_params=pltpu.CompilerParams(dimension_semantics=("arbitrary",),
                                             vmem_limit_bytes=VMEM_LIMIT),
        name="outproj",
    )(nsa, *dil_o, *dil_l, hg, h2, wn, wd, wh, g)


def _mlp_body(h_ref, p_ref, gpre_ref, wup_ref, wdn_ref, gpost_ref, gple_ref, wg_ref, wp_ref, out_ref, *, fc):
    h = h_ref[...]
    hn = _rms(h, gpre_ref[...]).astype(BF16)
    acc = jnp.zeros(h.shape, F32)
    for c in range(0, D_FF, fc):
        u = jnp.maximum(_dot(hn, wup_ref[:, c:c + fc]), 0.0)
        acc = acc + _dot((u * u).astype(BF16), wdn_ref[c:c + fc, :])
    h = h + _rms(acc, gpost_ref[...])
    gate = jax.nn.sigmoid(_dot(_rms(h, gple_ref[...]).astype(BF16), wg_ref[...]))
    out_ref[...] = h + _dot(p_ref[...].astype(BF16), wp_ref[...]) * gate


def _mlp(h2, p2, gpre, wup, wdn, gpost, gple, wg, wp, tm):
    T = h2.shape[0]
    row = lambda i: (i, 0)
    full = lambda i: (0, 0)
    vec = pl.BlockSpec((1, D_MODEL), full)
    return pl.pallas_call(
        functools.partial(_mlp_body, fc=512),
        grid=(T // tm,),
        in_specs=[pl.BlockSpec((tm, D_MODEL), row), pl.BlockSpec((tm, PLE_DIM), row), vec,
                  pl.BlockSpec(wup.shape, full), pl.BlockSpec(wdn.shape, full), vec, vec,
                  pl.BlockSpec(wg.shape, full), pl.BlockSpec(wp.shape, full)],
        out_specs=pl.BlockSpec((tm, D_MODEL), row),
        out_shape=jax.ShapeDtypeStruct((T, D_MODEL), F32),
        compiler_params=pltpu.CompilerParams(dimension_semantics=("arbitrary",),
                                             vmem_limit_bytes=VMEM_LIMIT),
        name="mlp_ple",
    )(h2, p2, gpre, wup, wdn, gpost, gple, wg, wp)


def _inproj_columns():
    zero = IN_TOTAL
    c16 = np.full((W16,), zero, np.int64)
    for hd in range(NSA_HEADS):
        c16[P16_NQ + hd * LANE:P16_NQ + hd * LANE + HEAD_DIM] = OFF_NQ + hd * HEAD_DIM + np.arange(HEAD_DIM)
    for kh in range(NSA_KV_HEADS):
        for base, ko, vo in ((P16_KVS, OFF_NKS, OFF_NVS), (P16_KVW, OFF_NKW, OFF_NVW)):
            o = base + kh * LANE
            c16[o:o + HEAD_DIM] = ko + kh * HEAD_DIM + np.arange(HEAD_DIM)
            c16[o + HEAD_DIM:o + LANE] = vo + kh * HEAD_DIM + np.arange(HEAD_DIM)
    for g in range(len(DIL_PAIRS)):
        base = P16_DIL + g * DIL_GROUP_W
        for i in range(DIL_HPG):
            hd = g * DIL_HPG + i
            c16[base + i * LANE:base + i * LANE + HEAD_DIM] = OFF_DQ + hd * HEAD_DIM + np.arange(HEAD_DIM)
            o = base + DIL_HPG * LANE + i * LANE
            c16[o:o + HEAD_DIM] = OFF_DK + hd * HEAD_DIM + np.arange(HEAD_DIM)
            c16[o + HEAD_DIM:o + LANE] = OFF_DV + hd * HEAD_DIM + np.arange(HEAD_DIM)
    cf = np.full((WF,), zero, np.int64)
    for dst, src in ((PF_HQ, OFF_HQ), (PF_HF, OFF_HF), (PF_HI, OFF_HI), (PF_HG, OFF_HG)):
        cf[dst:dst + HG_W] = src + np.arange(HG_W)
    cf[PF_GATE:PF_GATE + 3 * NSA_HEADS] = OFF_GATE + np.arange(3 * NSA_HEADS)
    cc = np.concatenate([OFF_NKC + np.arange(NSA_KV_W), OFF_NVC + np.arange(NSA_KV_W)])
    return c16, cf, cc


def _outproj_rows():
    zero = NSA_Q_W + DIL_W + HG_W
    rn = np.full((NSA_HEADS * LANE,), zero, np.int64)
    for hd in range(NSA_HEADS):
        rn[hd * LANE + HEAD_DIM:(hd + 1) * LANE] = hd * HEAD_DIM + np.arange(HEAD_DIM)
    rd = np.full((len(DIL_PAIRS), MIX_DIL_W), zero, np.int64)
    for g in range(len(DIL_PAIRS)):
        for i in range(DIL_HPG):
            rd[g, i * LANE + HEAD_DIM:(i + 1) * LANE] = NSA_Q_W + (g * DIL_HPG + i) * HEAD_DIM + np.arange(HEAD_DIM)
    return rn, rd


def kernel(x, p, w_in, w_out, cmp_pos, cmp_w1, cmp_w2, hg_lb, g_pre_mix, g_post_mix,
           g_pre_mlp, g_post_mlp, w_up, w_down, g_ple, w_ple_gate, w_ple_proj):
    B, S, D = x.shape
    assert D == D_MODEL and S % 256 == 0 and S % (CMP_STRIDE * 8) == 0 and S // SLC_BLOCK <= LANE
    T = B * S
    tm = 512 if T % 512 == 0 else 256
    tm_mlp = 256
    tq_nsa, tk_nsa = 128, 256
    hg_chunk = 128
    nsa_slopes, dil_slopes = _alibi_slopes()
    c16, cf, cc = _inproj_columns()
    rn, rd = _outproj_rows()
    n_cmp = (S - CMP_BLOCK) // CMP_STRIDE + 1
    nr = S // CMP_STRIDE
    ncp = -(-nr // LANE) * LANE
    e_sel = jnp.asarray((np.arange(S)[None, :] // SLC_BLOCK) == np.arange(LANE)[:, None], BF16)

    h = x.reshape(T, D)
    for l in range(DEPTH):
        w_ext = jnp.concatenate([w_in[l], jnp.zeros((D, 1), w_in.dtype)], axis=1)
        w16 = w_ext[:, c16].astype(BF16)
        wf = w_ext[:, cf].astype(BF16)
        wc = w_ext[:, cc].astype(BF16)
        p16, pf, kc0, kc1, vc0, vc1 = _inproj(h, g_pre_mix[l][None, :], w16, wf, wc, tm)
        p16 = p16.reshape(B, S, W16)
        pf = pf.reshape(B, S, WF)

        blk16 = lambda a: a.reshape(B, nr, CMP_STRIDE * HEAD_DIM)
        pos8 = jnp.pad(cmp_pos[l].reshape(2, 1, CMP_BLOCK * HEAD_DIM), ((0, 0), (0, 7), (0, 0))).astype(BF16)
        w2p = jnp.stack([jnp.pad(cmp_w2[l, 0], ((0, 0), (0, HEAD_DIM))),
                         jnp.pad(cmp_w2[l, 1], ((0, 0), (HEAD_DIM, 0)))]).astype(BF16)
        cmp_kv = _compress(blk16(kc0), blk16(kc1), blk16(vc0), blk16(vc1), pos8,
                           cmp_w1[l].astype(BF16), w2p, ncp, n_cmp)

        o_nsa = _nsa(p16, pf, cmp_kv, e_sel, S, tq_nsa, tk_nsa, nsa_slopes).reshape(T, NSA_HEADS * LANE)
        dil = [_dilated(p16, S, g, dil_slopes, 128) for g in range(len(DIL_PAIRS))]
        o_hg = _hgrn2(pf, hg_lb, l, S, hg_chunk).reshape(T, HG_W)

        wo_ext = jnp.concatenate([w_out[l], jnp.zeros((1, D), w_out.dtype)], axis=0)
        wn = wo_ext[rn].astype(BF16)
        wd = wo_ext[rd].astype(BF16)
        wh = w_out[l, NSA_Q_W + DIL_W:].astype(BF16)
        h = _outproj(o_nsa, [d[0] for d in dil], [d[1] for d in dil], o_hg, h, wn, wd, wh,
                     g_post_mix[l][None, :], tm)
        h = _mlp(h, p[l].reshape(T, PLE_DIM), g_pre_mlp[l][None, :], w_up[l].astype(BF16),
                 w_down[l].astype(BF16), g_post_mlp[l][None, :], g_ple[l][None, :],
                 w_ple_gate[l].astype(BF16), w_ple_proj[l].astype(BF16), tm_mlp)
    return h.reshape(B, S, D)
```

```python
import functools

import numpy as np
import jax
import jax.numpy as jnp
from jax import lax
from jax.experimental import pallas as pl
from jax.experimental.pallas import tpu as pltpu

F32 = jnp.float32
BF16 = jnp.bfloat16

D_MODEL = 1024
DEPTH = 2
HEAD_DIM = 64
NSA_HEADS = 6
NSA_KV_HEADS = 2
NSA_GROUP = NSA_HEADS // NSA_KV_HEADS
CMP_BLOCK = 32
CMP_STRIDE = 16
CMP_HIDDEN = 256
SLC_BLOCK = 64
SLC_TOPN = 8
WIN = 512
FORCE_SCORE = 1e9
DIL_PAIRS = ((128, 1), (512, 4), (2048, 16))
DIL_HPG = 2
DIL_HEADS = DIL_HPG * len(DIL_PAIRS)
HG_HEADS = 4
HG_DIM = 64
LB_TINY = 1e-30
D_FF = 4 * D_MODEL
PLE_DIM = 256
EPS = 1e-6
NEG = -1e30

NSA_Q_W = NSA_HEADS * HEAD_DIM
NSA_KV_W = NSA_KV_HEADS * HEAD_DIM
DIL_W = DIL_HEADS * HEAD_DIM
HG_W = HG_HEADS * HG_DIM
IN_WIDTHS = (NSA_Q_W,) + (NSA_KV_W,) * 6 + (3 * NSA_HEADS,) + (DIL_W,) * 3 + (HG_W,) * 4
IN_TOTAL = sum(IN_WIDTHS)
IN_OFF = tuple(int(v) for v in np.cumsum((0,) + IN_WIDTHS))
(OFF_NQ, OFF_NKC, OFF_NVC, OFF_NKS, OFF_NVS, OFF_NKW, OFF_NVW, OFF_GATE,
 OFF_DQ, OFF_DK, OFF_DV, OFF_HQ, OFF_HF, OFF_HI, OFF_HG) = IN_OFF[:-1]

LANE = 128
VMEM_LIMIT = 56 * 1024 * 1024

P16_NQ = 0
P16_KVS = P16_NQ + NSA_HEADS * LANE
P16_KVW = P16_KVS + NSA_KV_HEADS * LANE
P16_DIL = P16_KVW + NSA_KV_HEADS * LANE
DIL_GROUP_W = 2 * DIL_HPG * LANE
W16 = P16_DIL + len(DIL_PAIRS) * DIL_GROUP_W
PF_HQ, PF_HF, PF_HI, PF_HG, PF_GATE = 0, HG_W, 2 * HG_W, 3 * HG_W, 4 * HG_W
WF = PF_GATE + LANE
WC = 2 * NSA_KV_W
MIX_DIL_W = DIL_HPG * LANE


def _dot(a, b):
    return jnp.dot(a, b, preferred_element_type=F32)


def _dot_nt(a, b):
    return lax.dot_general(a, b, (((1,), (1,)), ((), ())), preferred_element_type=F32)


def _rms(x, g):
    return x * lax.rsqrt(jnp.mean(x * x, axis=-1, keepdims=True) + EPS) * g


def _log2(n):
    l = int(n).bit_length() - 1
    assert (1 << l) == n, n
    return l


def _alibi_slopes():
    n = NSA_HEADS + DIL_HEADS
    s = 2.0 ** (-8.0 * np.arange(1, n + 1) / n)
    quads = s.reshape(-1, 4)
    nsa = [float(np.float32(v)) for v in quads[:, 2:].reshape(-1)]
    dil = [float(np.float32(v)) for v in quads[:, :2].reshape(-1)]
    return nsa, dil


def _inproj_body(x_ref, g_ref, w16_ref, wf_ref, wc_ref,
                 o16_ref, of_ref, kc0_ref, kc1_ref, vc0_ref, vc1_ref):
    hn = _rms(x_ref[...], g_ref[...]).astype(BF16)
    for c in range(0, W16, 256):
        o16_ref[:, c:c + 256] = _dot(hn, w16_ref[:, c:c + 256]).astype(BF16)
    for c in range(0, WF, 384):
        of_ref[:, c:c + 384] = _dot(hn, wf_ref[:, c:c + 384])
    cc = _dot(hn, wc_ref[...])
    kc0_ref[...] = cc[:, 0:64]
    kc1_ref[...] = cc[:, 64:128]
    vc0_ref[...] = cc[:, 128:192]
    vc1_ref[...] = cc[:, 192:256]


def _inproj(x2, g, w16, wf, wc, tm):
    T = x2.shape[0]
    row = lambda i: (i, 0)
    full = lambda i: (0, 0)
    return pl.pallas_call(
        _inproj_body,
        grid=(T // tm,),
        in_specs=[pl.BlockSpec((tm, D_MODEL), row),
                  pl.BlockSpec((1, D_MODEL), full),
                  pl.BlockSpec((D_MODEL, W16), full),
                  pl.BlockSpec((D_MODEL, WF), full),
                  pl.BlockSpec((D_MODEL, WC), full)],
        out_specs=[pl.BlockSpec((tm, W16), row), pl.BlockSpec((tm, WF), row)]
        + [pl.BlockSpec((tm, HEAD_DIM), row)] * 4,
        out_shape=[jax.ShapeDtypeStruct((T, W16), BF16), jax.ShapeDtypeStruct((T, WF), F32)]
        + [jax.ShapeDtypeStruct((T, HEAD_DIM), F32)] * 4,
        compiler_params=pltpu.CompilerParams(dimension_semantics=("arbitrary",),
                                             vmem_limit_bytes=VMEM_LIMIT),
        name="inproj",
    )(x2, g, w16, wf, wc)


def _compress_body(kc0_ref, kc1_ref, vc0_ref, vc1_ref, pos_ref, w1_ref, w2_ref, out_ref, *, n_cmp):
    nr = kc0_ref.shape[1]
    half = (CMP_BLOCK // 2) * HEAD_DIM
    rows = lax.broadcasted_iota(jnp.int32, (nr, 1), 0)
    out_ref[...] = jnp.zeros(out_ref.shape, out_ref.dtype)
    srcs = ((kc0_ref, vc0_ref), (kc1_ref, vc1_ref))
    for h in range(NSA_KV_HEADS):
        acc = jnp.zeros((nr, LANE), F32)
        for ten in range(2):
            x = srcs[h][ten][0].astype(BF16)
            first = _dot(x, w1_ref[ten, 0:half, :])
            second = _dot(x, w1_ref[ten, half:2 * half, :])
            posb = _dot(pos_ref[ten], w1_ref[ten])[0:1, :]
            hid = first + pltpu.roll(second, nr - 1, 0) + posb
            act = hid * jax.nn.sigmoid(hid)
            acc = acc + _dot(act.astype(BF16), w2_ref[ten])
        acc = jnp.where(rows < n_cmp, acc, 0.0)
        out_ref[0, 0:nr, h * LANE:(h + 1) * LANE] = acc.astype(BF16)


def _compress(kc0, kc1, vc0, vc1, pos8, w1, w2p, ncp, n_cmp):
    B, nr, kw = kc0.shape
    seq = lambda b: (b, 0, 0)
    full3 = lambda b: (0, 0, 0)
    return pl.pallas_call(
        functools.partial(_compress_body, n_cmp=n_cmp),
        grid=(B,),
        in_specs=[pl.BlockSpec((1, nr, kw), seq)] * 4
        + [pl.BlockSpec(pos8.shape, full3), pl.BlockSpec(w1.shape, full3), pl.BlockSpec(w2p.shape, full3)],
        out_specs=pl.BlockSpec((1, ncp, NSA_KV_HEADS * LANE), seq),
        out_shape=jax.ShapeDtypeStruct((B, ncp, NSA_KV_HEADS * LANE), BF16),
        compiler_params=pltpu.CompilerParams(dimension_semantics=("arbitrary",),
                                             vmem_limit_bytes=VMEM_LIMIT),
        name="nsa_compress",
    )(kc0, kc1, vc0, vc1, pos8, w1, w2p)


def _nsa_body(q_ref, kvs_ref, kvw_ref, kvc_ref, gate_ref, e_ref, out_ref, s_scr, mb_scr,
              *, seq, tq, tk, n_cmp, n_slc, k_top, slopes):
    kh = pl.program_id(1)
    qi = pl.program_id(2)
    t0 = qi * tq
    ncp = kvc_ref.shape[1]
    ltk = _log2(tk)
    ii = lax.broadcasted_iota(jnp.int32, (tq, 1), 0)
    row_t = t0 + ii
    slope = [jnp.where(kh == 0, slopes[g], slopes[NSA_GROUP + g]).astype(F32) for g in range(NSA_GROUP)]
    gates = jax.nn.sigmoid(gate_ref[0])

    def gate_col(g, c):
        j0 = g * 3 + c
        j1 = (NSA_GROUP + g) * 3 + c
        return jnp.where(kh == 0, gates[:, j0:j0 + 1], gates[:, j1:j1 + 1])

    q = [q_ref[0, :, g * LANE:(g + 1) * LANE] * (HEAD_DIM ** -0.5) for g in range(NSA_GROUP)]

    kvc = kvc_ref[0]
    nn = lax.broadcasted_iota(jnp.int32, (1, ncp), 1)
    maskc = ((nn * CMP_STRIDE + (CMP_BLOCK - 1)) <= row_t) & (nn < n_cmp)
    absd = jnp.abs(row_t.astype(F32) - (nn.astype(F32) * CMP_STRIDE + 0.5 * (CMP_BLOCK - 1)))
    psum = jnp.zeros((tq, ncp), F32)
    o_cmp = []
    for g in range(NSA_GROUP):
        s = _dot_nt(q[g], kvc) - slope[g] * absd
        s = jnp.where(maskc, s, NEG)
        m = jnp.max(s, axis=-1, keepdims=True)
        e = jnp.where(maskc, jnp.exp(s - m), 0.0)
        p = e / jnp.maximum(jnp.sum(e, axis=-1, keepdims=True), 1e-30)
        o_cmp.append(_dot(p.astype(BF16), kvc))
        psum = psum + p

    ni = lax.broadcasted_iota(jnp.int32, (ncp, 1), 0)
    mj = lax.broadcasted_iota(jnp.int32, (1, LANE), 1)
    ov = ((ni * CMP_STRIDE < mj * SLC_BLOCK + SLC_BLOCK) & (ni * CMP_STRIDE + CMP_BLOCK > mj * SLC_BLOCK)
          & (ni < n_cmp) & (mj < n_slc))
    ovb = jnp.where(ov, 1.0, 0.0).astype(BF16)
    p_hi = psum.astype(BF16)
    p_lo = (psum - p_hi.astype(F32)).astype(BF16)
    imp = _dot(p_hi, ovb) + _dot(p_lo, ovb)
    cur = row_t >> _log2(SLC_BLOCK)
    in_rng = mj < n_slc
    valid = (mj <= cur) & in_rng
    forced = (mj == 0) | (mj == cur) | (mj == cur - 1)
    score = jnp.where(valid, jnp.where(forced, FORCE_SCORE, imp), -FORCE_SCORE)
    score = jnp.where(in_rng, score, -3.0 * FORCE_SCORE)
    rank = jnp.zeros((tq, LANE), F32)
    for mp in range(n_slc):
        col = score[:, mp:mp + 1]
        beats = (col > score) | ((col == score) & (mp < mj))
        rank = rank + jnp.where(beats, 1.0, 0.0)
    sel = (rank < k_top) & valid
    selb = jnp.where(sel, 1.0, 0.0).astype(BF16)

    n_kt = (t0 + tq + tk - 1) >> ltk
    for kt in range(seq // tk):
        @pl.when(kt < n_kt)
        def _():
            mb_scr[kt] = (_dot(selb, e_ref[:, kt * tk:(kt + 1) * tk]) - 1.0) * 1e30

    jj = lax.broadcasted_iota(jnp.int32, (1, tk), 1)
    dmat = (jj - ii).astype(F32)
    kt_lo_win = jnp.maximum(t0 - (WIN - 1), 0) >> ltk

    def attend(kv_ref, qg, dg, slope_g, lo, sel_branch):
        def scores(kt, m_acc):
            kv = kv_ref[0, pl.ds(pl.multiple_of(kt * tk, tk), tk), :]
            s = _dot_nt(qg, kv) + dg + slope_g * (kt * tk - t0).astype(F32)
            rel = row_t - (jj + kt * tk)
            if sel_branch:
                s = jnp.where(rel >= 0, s + mb_scr[kt], NEG)
            else:
                s = jnp.where((rel >= 0) & (rel < WIN), s, NEG)
            s_scr[kt] = s
            return jnp.maximum(m_acc, s)

        m_acc = lax.fori_loop(lo, n_kt, scores, jnp.full((tq, tk), NEG, F32))
        m = jnp.max(m_acc, axis=-1, keepdims=True)

        def accum(kt, carry):
            l_acc, acc = carry
            p = jnp.exp(s_scr[kt] - m)
            kv = kv_ref[0, pl.ds(pl.multiple_of(kt * tk, tk), tk), :]
            return l_acc + p, acc + _dot(p.astype(BF16), kv)

        l_acc, acc = lax.fori_loop(lo, n_kt, accum,
                                   (jnp.zeros((tq, tk), F32), jnp.zeros((tq, LANE), F32)))
        return acc / jnp.sum(l_acc, axis=-1, keepdims=True)

    for g in range(NSA_GROUP):
        dg = slope[g] * dmat
        o_sel = attend(kvs_ref, q[g], dg, slope[g], 0, True)
        o_win = attend(kvw_ref, q[g], dg, slope[g], kt_lo_win, False)
        o = gate_col(g, 0) * o_cmp[g] + gate_col(g, 1) * o_sel + gate_col(g, 2) * o_win
        out_ref[0, :, g * LANE:(g + 1) * LANE] = o.astype(BF16)


def _nsa(p16, pf, cmp_kv, e_sel, seq, tq, tk, nsa_slopes):
    B = p16.shape[0]
    n_cmp = (seq - CMP_BLOCK) // CMP_STRIDE + 1
    n_slc = seq // SLC_BLOCK
    ncp = cmp_kv.shape[1]
    qw = NSA_GROUP * LANE
    body = functools.partial(_nsa_body, seq=seq, tq=tq, tk=tk, n_cmp=n_cmp, n_slc=n_slc,
                             k_top=min(SLC_TOPN, n_slc), slopes=tuple(nsa_slopes))
    return pl.pallas_call(
        body,
        grid=(B, NSA_KV_HEADS, seq // tq),
        in_specs=[pl.BlockSpec((1, tq, qw), lambda b, k, i: (b, i, k)),
                  pl.BlockSpec((1, seq, LANE), lambda b, k, i: (b, 0, P16_KVS // LANE + k)),
                  pl.BlockSpec((1, seq, LANE), lambda b, k, i: (b, 0, P16_KVW // LANE + k)),
                  pl.BlockSpec((1, ncp, LANE), lambda b, k, i: (b, 0, k)),
                  pl.BlockSpec((1, tq, LANE), lambda b, k, i: (b, i, PF_GATE // LANE)),
                  pl.BlockSpec(e_sel.shape, lambda b, k, i: (0, 0))],
        out_specs=pl.BlockSpec((1, tq, qw), lambda b, k, i: (b, i, k)),
        out_shape=jax.ShapeDtypeStruct((B, seq, NSA_HEADS * LANE), BF16),
        scratch_shapes=[pltpu.VMEM((seq // tk, tq, tk), F32), pltpu.VMEM((seq // tk, tq, tk), F32)],
        compiler_params=pltpu.CompilerParams(dimension_semantics=("arbitrary",) * 3,
                                             vmem_limit_bytes=VMEM_LIMIT),
        name="nsa_attention",
    )(p16, p16, p16, cmp_kv, pf, e_sel)


NSA_PAT_BLK = HEAD_DIM
NSA_PAT_POS = HEAD_DIM + 32


def _nsa_pattern(seq):
    pos = np.arange(seq)
    pat = np.zeros((seq, LANE), np.float32)
    pat[pos, NSA_PAT_BLK + pos // SLC_BLOCK] = 1.0
    pat[:, NSA_PAT_POS:NSA_PAT_POS + 3] = (SLC_BLOCK * (pos // SLC_BLOCK))[:, None]
    pat[:, NSA_PAT_POS + 3:NSA_PAT_POS + 6] = (pos % SLC_BLOCK)[:, None]
    return jnp.asarray(pat, BF16)


def _nsa_slope_rows(nsa_slopes):
    rows = np.zeros((NSA_KV_HEADS, 8, LANE), np.float32)
    for kh in range(NSA_KV_HEADS):
        for g in range(NSA_GROUP):
            rest = np.float32(nsa_slopes[kh * NSA_GROUP + g])
            for part in range(3):
                piece = np.float32(np.asarray(rest, dtype=BF16))
                rows[kh, g, NSA_PAT_POS + part] = piece
                rows[kh, g, NSA_PAT_POS + 3 + part] = piece
                rest = np.float32(rest - piece)
    return jnp.asarray(rows)


def _nsa2_body(q_ref, kvs_ref, kvw_ref, kvc_ref, gate_ref, pat_ref, sl_ref, out_ref,
               ks_scr, vs_scr, kw_scr, vw_scr, s_scr, *, seq, tq, tk, n_cmp, n_slc, k_top, slopes):
    kh = pl.program_id(1)
    qi = pl.program_id(2)
    G = NSA_GROUP
    t0 = qi * tq
    ncp = kvc_ref.shape[1]
    ltk = _log2(tk)
    lane = lax.broadcasted_iota(jnp.int32, (1, LANE), 1)
    lo_half = lane < HEAD_DIM

    @pl.when(qi == 0)
    def _():
        pat = pat_ref[...]
        kvs = kvs_ref[0]
        kvw = kvw_ref[0]
        one = jnp.ones(kvs.shape, BF16)
        ks_scr[...] = jnp.where(lo_half, kvs, pat)
        kw_scr[...] = jnp.where(lo_half, kvw, pat)
        vs_scr[...] = jnp.where(lo_half, one, kvs)
        vw_scr[...] = jnp.where(lo_half, one, kvw)

    ii = lax.broadcasted_iota(jnp.int32, (tq, 1), 0)
    row_t = t0 + ii
    row_t3 = jnp.concatenate([row_t] * G, axis=0)
    slope = [jnp.where(kh == 0, slopes[g], slopes[G + g]).astype(F32) for g in range(G)]
    qs = [q_ref[0, :, g * LANE:(g + 1) * LANE] * (HEAD_DIM ** -0.5) for g in range(G)]
    q3 = jnp.concatenate(qs, axis=0)

    kvc = kvc_ref[0]
    nn = lax.broadcasted_iota(jnp.int32, (1, ncp), 1)
    maskc = ((nn * CMP_STRIDE + (CMP_BLOCK - 1)) <= row_t) & (nn < n_cmp)
    absd = jnp.abs(row_t.astype(F32) - (nn.astype(F32) * CMP_STRIDE + 0.5 * (CMP_BLOCK - 1)))
    s_c = _dot_nt(q3, kvc)
    ps = []
    for g in range(G):
        s = jnp.where(maskc, s_c[g * tq:(g + 1) * tq] - slope[g] * absd, NEG)
        m = jnp.max(s, axis=-1, keepdims=True)
        e = jnp.where(maskc, jnp.exp(s - m), 0.0)
        ps.append(e / jnp.maximum(jnp.sum(e, axis=-1, keepdims=True), 1e-30))
    o_cmp = _dot(jnp.concatenate(ps, axis=0).astype(BF16), kvc)
    psum = ps[0] + ps[1] + ps[2]

    mi = lax.broadcasted_iota(jnp.int32, (LANE, 1), 0)
    ov_t = ((nn * CMP_STRIDE < mi * SLC_BLOCK + SLC_BLOCK) & (nn * CMP_STRIDE + CMP_BLOCK > mi * SLC_BLOCK)
            & (nn < n_cmp) & (mi < n_slc))
    ov_t = jnp.where(ov_t, 1.0, 0.0).astype(BF16)
    p_hi = psum.astype(BF16)
    p_lo = (psum - p_hi.astype(F32)).astype(BF16)
    imp_t = _dot_nt(ov_t, p_hi) + _dot_nt(ov_t, p_lo)
    nsp = -(-n_slc // 8) * 8
    mi_s = mi[0:nsp]
    cur_l = (t0 + lax.broadcasted_iota(jnp.int32, (1, tq), 1)) >> _log2(SLC_BLOCK)
    in_rng = mi_s < n_slc
    valid = (mi_s <= cur_l) & in_rng
    forced = (mi_s == 0) | (mi_s == cur_l) | (mi_s == cur_l - 1)
    score = jnp.where(valid, jnp.where(forced, FORCE_SCORE, imp_t[0:nsp]), -FORCE_SCORE)
    score = jnp.where(in_rng, score, -3.0 * FORCE_SCORE)
    rank = jnp.zeros((nsp, tq), F32)
    for mp in range(n_slc):
        row = score[mp:mp + 1, :]
        beats = (row > score) | ((row == score) & (mp < mi_s))
        rank = rank + jnp.where(beats, 1.0, 0.0)
    neg_t = jnp.where((rank < k_top) & valid, 0.0, NEG)
    neg_t = jnp.concatenate([jnp.zeros((NSA_PAT_BLK, tq), F32), neg_t,
                             jnp.zeros((LANE - NSA_PAT_BLK - nsp, tq), F32)], axis=0)
    neg = neg_t.T

    def extended(with_mask):
        out = []
        for g in range(G):
            extra = sl_ref[0, g:g + 1, :] + (neg if with_mask else 0.0)
            out.append(jnp.where(lo_half, qs[g], extra.astype(BF16)))
        return jnp.concatenate(out, axis=0)

    hi_half = lane >= HEAD_DIM

    def normalize(acc):
        den = pltpu.roll(acc, HEAD_DIM, 1)
        return jnp.where(hi_half, acc / jnp.where(hi_half, den, 1.0), 0.0)

    n_kt = (t0 + tq + tk - 1) >> ltk

    nwt = min(-(-(WIN + tq) // tk), seq // tk)
    wk = nwt * tk
    k0 = pl.multiple_of(jnp.clip(n_kt - nwt, 0, seq // tk - nwt) * tk, tk)
    kt0 = jnp.clip(n_kt - nwt, 0, seq // tk - nwt)
    span = pl.ds(k0, wk)
    rel = row_t - (k0 + lax.broadcasted_iota(jnp.int32, (1, wk), 1))
    ok_causal = jnp.where(rel >= 0, 0.0, NEG)
    ok_win = jnp.where(rel < WIN, ok_causal, NEG)

    def masked(s, bias):
        return (s.reshape(G, tq, wk) + bias[None]).reshape(G * tq, wk)

    s = masked(_dot_nt(extended(False), kw_scr[span, :]), ok_win)
    p = jnp.exp(s - jnp.max(s, axis=-1, keepdims=True))
    o_win = normalize(_dot(p.astype(BF16), vw_scr[span, :]))

    q_sel = extended(True)

    def far_scores(kt, m_acc):
        s = _dot_nt(q_sel, ks_scr[pl.ds(pl.multiple_of(kt * tk, tk), tk), :])
        s_scr[kt] = s
        for c in range(0, tk, LANE):
            m_acc = jnp.maximum(m_acc, s[:, c:c + LANE])
        return m_acc

    m_far = lax.fori_loop(0, kt0, far_scores, jnp.full((G * tq, LANE), NEG, F32))
    s_near = masked(_dot_nt(q_sel, ks_scr[span, :]), ok_causal)
    m_row = jnp.maximum(jnp.max(m_far, axis=-1, keepdims=True), jnp.max(s_near, axis=-1, keepdims=True))
    acc = _dot(jnp.exp(s_near - m_row).astype(BF16), vs_scr[span, :])

    def far_accum(kt, acc):
        p = jnp.exp(s_scr[kt] - m_row)
        return acc + _dot(p.astype(BF16), vs_scr[pl.ds(pl.multiple_of(kt * tk, tk), tk), :])

    o_sel = normalize(lax.fori_loop(0, kt0, far_accum, acc))

    gates = jax.nn.sigmoid(gate_ref[0])
    for g in range(G):
        rows = slice(g * tq, (g + 1) * tq)
        gc = [jnp.where(kh == 0, gates[:, g * 3 + c:g * 3 + c + 1],
                        gates[:, (G + g) * 3 + c:(G + g) * 3 + c + 1]) for c in range(3)]
        o = gc[0] * o_cmp[rows] + gc[1] * o_sel[rows] + gc[2] * o_win[rows]
        out_ref[0, :, g * LANE:(g + 1) * LANE] = o.astype(BF16)


def _nsa2(p16, pf, cmp_kv, seq, tq, tk, nsa_slopes):
    B = p16.shape[0]
    n_cmp = (seq - CMP_BLOCK) // CMP_STRIDE + 1
    n_slc = seq // SLC_BLOCK
    ncp = cmp_kv.shape[1]
    qw = NSA_GROUP * LANE
    pat = _nsa_pattern(seq)
    sl_rows = _nsa_slope_rows(nsa_slopes)
    body = functools.partial(_nsa2_body, seq=seq, tq=tq, tk=tk, n_cmp=n_cmp, n_slc=n_slc,
                             k_top=min(SLC_TOPN, n_slc), slopes=tuple(nsa_slopes))
    return pl.pallas_call(
        body,
        grid=(B, NSA_KV_HEADS, seq // tq),
        in_specs=[pl.BlockSpec((1, tq, qw), lambda b, k, i: (b, i, k)),
                  pl.BlockSpec((1, seq, LANE), lambda b, k, i: (b, 0, P16_KVS // LANE + k)),
                  pl.BlockSpec((1, seq, LANE), lambda b, k, i: (b, 0, P16_KVW // LANE + k)),
                  pl.BlockSpec((1, ncp, LANE), lambda b, k, i: (b, 0, k)),
                  pl.BlockSpec((1, tq, LANE), lambda b, k, i: (b, i, PF_GATE // LANE)),
                  pl.BlockSpec((seq, LANE), lambda b, k, i: (0, 0)),
                  pl.BlockSpec((1, 8, LANE), lambda b, k, i: (k, 0, 0))],
        out_specs=pl.BlockSpec((1, tq, qw), lambda b, k, i: (b, i, k)),
        out_shape=jax.ShapeDtypeStruct((B, seq, NSA_HEADS * LANE), BF16),
        scratch_shapes=[pltpu.VMEM((seq, LANE), BF16)] * 4
        + [pltpu.VMEM((seq // tk, NSA_GROUP * tq, tk), F32)],
        compiler_params=pltpu.CompilerParams(dimension_semantics=("arbitrary",) * 3,
                                             vmem_limit_bytes=VMEM_LIMIT),
        name="nsa_attention",
    )(p16, p16, p16, cmp_kv, pf, pat, sl_rows)


def _dilated_body(q_ref, kv_ref, o_ref, lse_ref, *, ls, dil, win_keys, slopes, tq):
    ii = lax.broadcasted_iota(jnp.int32, (tq, 1), 0)
    jj = lax.broadcasted_iota(jnp.int32, (1, 2 * tq), 1)
    rel = ii - jj + tq
    in_win = (rel >= 0) & (rel <= win_keys)
    relf = rel.astype(F32) * float(dil)
    for h in range(DIL_HPG):
        lanes = slice(h * LANE, (h + 1) * LANE)
        bias = slopes[h] * relf

        def tile(t, carry, lanes=lanes, bias=bias):
            u0 = pl.multiple_of(t * tq, tq)
            prev = pl.multiple_of(jnp.maximum(u0 - tq, 0), tq)
            qt = q_ref[0, pl.ds(u0, tq), lanes] * (HEAD_DIM ** -0.5)
            kv = jnp.concatenate([kv_ref[0, pl.ds(prev, tq), lanes], kv_ref[0, pl.ds(u0, tq), lanes]], axis=0)
            ok = in_win & ((jj >= tq) | (t > 0))
            s = jnp.where(ok, _dot_nt(qt, kv) - bias, NEG)
            m = jnp.max(s, axis=-1, keepdims=True)
            e = jnp.exp(s - m)
            l = jnp.sum(e, axis=-1, keepdims=True)
            o = _dot(e.astype(BF16), kv) / l
            o_ref[0, pl.ds(u0, tq), lanes] = o.astype(BF16)
            lse_ref[0, pl.ds(u0, tq), lanes] = jnp.broadcast_to(m + jnp.log(l), (tq, LANE))
            return carry

        lax.fori_loop(0, ls // tq, tile, 0)


def _dilated(p16, seq, group, slopes, tq):
    B = p16.shape[0]
    win, dil = DIL_PAIRS[group]
    ls = seq // dil
    tq = min(tq, ls)
    view = p16.reshape(B, ls, dil * W16)
    blk = MIX_DIL_W
    per_tok = W16 // blk
    qcol = (P16_DIL + group * DIL_GROUP_W) // blk
    body = functools.partial(_dilated_body, ls=ls, dil=dil, win_keys=win // dil,
                             slopes=tuple(slopes[group * DIL_HPG:(group + 1) * DIL_HPG]), tq=tq)
    o, lse = pl.pallas_call(
        body,
        grid=(B, dil),
        in_specs=[pl.BlockSpec((1, ls, blk), lambda b, r: (b, 0, r * per_tok + qcol)),
                  pl.BlockSpec((1, ls, blk), lambda b, r: (b, 0, r * per_tok + qcol + 1))],
        out_specs=[pl.BlockSpec((1, ls, blk), lambda b, r: (b, 0, r))] * 2,
        out_shape=[jax.ShapeDtypeStruct((B, ls, dil * blk), BF16),
                   jax.ShapeDtypeStruct((B, ls, dil * blk), F32)],
        compiler_params=pltpu.CompilerParams(dimension_semantics=("arbitrary",) * 2,
                                             vmem_limit_bytes=VMEM_LIMIT),
        name=f"dilated_attention_g{group}",
    )(view, view)
    return o.reshape(B * seq, blk), lse.reshape(B * seq, blk)


def _hgrn2_body(q_ref, f_ref, i_ref, g_ref, lb_ref, o_ref, *, layer, seq, chunk):
    c = chunk
    sub = 8
    lbs = lb_ref[...].astype(F32)
    mx = jnp.max(lbs, axis=0, keepdims=True)
    ex = jnp.exp(lbs - mx)
    sm = ex / jnp.sum(ex, axis=0, keepdims=True)
    lower = jnp.maximum(jnp.sum(sm[0:layer + 1], axis=0, keepdims=True) - sm[0:1], 0.0)
    log_lb = jnp.log(lower + LB_TINY)
    log_1m = jnp.log1p(-lower)

    lane = lax.broadcasted_iota(jnp.int32, (1, LANE), 1)
    head0 = lane < HG_DIM
    ri = lax.broadcasted_iota(jnp.int32, (c, 1), 0)
    ci = lax.broadcasted_iota(jnp.int32, (1, c), 1)
    tri = jnp.where(ci <= ri, 1.0, 0.0).astype(BF16)
    di = lax.broadcasted_iota(jnp.int32, (LANE, 1), 0)
    same_head = (di >= HG_DIM) == (lane >= HG_DIM)
    ones_blk = jnp.where(same_head, 1.0, 0.0).astype(BF16)
    gcol = lax.broadcasted_iota(jnp.int32, (1, sub * c), 1)
    gsum = jnp.where(((gcol >> _log2(c)) == (ri & (sub - 1)))
                     & (((gcol & (c - 1)) >> 3) == (ri >> 3)), 1.0, 0.0).astype(BF16)
    sp = lax.broadcasted_iota(jnp.int32, (1, sub, 1), 1)
    levels = []
    w = sub
    while w < c:
        same = (ri >> _log2(2 * w)) == (ci >> _log2(2 * w))
        levels.append((w, same & ((ri & (2 * w - 1)) >= w) & ((ci & (2 * w - 1)) < w)))
        w *= 2

    def split3(x):
        hi = x.astype(BF16)
        r1 = x - hi.astype(F32)
        mid = r1.astype(BF16)
        lo = (r1 - mid.astype(F32)).astype(BF16)
        return hi, mid, lo

    def step(ic, state_t):
        rows = pl.ds(pl.multiple_of(ic * c, c), c)
        q = q_ref[0, rows, :]
        v = i_ref[0, rows, :]
        x = f_ref[0, rows, :]
        log_sig = jnp.minimum(x, 0.0) - jnp.log1p(jnp.exp(-jnp.abs(x)))
        t2 = log_1m + log_sig
        lf = jnp.maximum(log_lb, t2) + jnp.log1p(jnp.exp(-jnp.abs(log_lb - t2)))
        kk = 1.0 - jnp.exp(lf)
        hi, mid, lo = split3(lf)
        b = _dot(tri, hi) + _dot(tri, mid) + _dot(tri, lo)
        vb = v.astype(BF16)

        q3 = q.reshape(c // sub, sub, LANE)
        k3 = kk.reshape(c // sub, sub, LANE)
        b3 = b.reshape(c // sub, sub, LANE)
        parts = []
        for tp in range(sub):
            dec = jnp.exp(jnp.minimum(b3[:, tp:tp + 1, :] - b3, 0.0))
            parts.append(jnp.where(sp <= tp, q3[:, tp:tp + 1, :] * k3 * dec, 0.0).reshape(c, LANE))
        wall = jnp.concatenate(parts, axis=0)
        a_rep = _dot(wall.astype(BF16), ones_blk)
        z = a_rep * jnp.concatenate([v] * sub, axis=0)
        o = _dot(gsum, z.astype(BF16))

        a0 = jnp.zeros((c, c), F32)
        a1 = jnp.zeros((c, c), F32)
        for w, lmask in levels:
            b_r = b.reshape(c // (2 * w), 2 * w, LANE)
            bnd = jnp.broadcast_to(b_r[:, w - 1:w, :], b_r.shape).reshape(c, LANE)
            qe = q * jnp.exp(jnp.minimum(b - bnd, 0.0))
            ke = (kk * jnp.exp(jnp.minimum(bnd - b, 0.0))).astype(BF16)
            a0 = a0 + jnp.where(lmask, _dot_nt(jnp.where(head0, qe, 0.0).astype(BF16), ke), 0.0)
            a1 = a1 + jnp.where(lmask, _dot_nt(jnp.where(head0, 0.0, qe).astype(BF16), ke), 0.0)
        o = o + jnp.where(head0, _dot(a0.astype(BF16), vb), _dot(a1.astype(BF16), vb))

        o = o + _dot_nt((q * jnp.exp(b)).astype(BF16), state_t.astype(BF16))
        b_last = b[c - 1:c, :]
        khat = (kk * jnp.exp(b_last - b)).astype(BF16)
        upd = lax.dot_general(vb, khat, (((0,), (0,)), ((), ())), preferred_element_type=F32)
        state_t = jnp.exp(b_last) * state_t + jnp.where(same_head, upd, 0.0)

        o2 = o * o
        ms0 = jnp.sum(jnp.where(head0, o2, 0.0), axis=-1, keepdims=True)
        ms1 = jnp.sum(jnp.where(head0, 0.0, o2), axis=-1, keepdims=True)
        o = o * lax.rsqrt(jnp.where(head0, ms0, ms1) * (1.0 / HG_DIM) + EPS)
        gt = g_ref[0, rows, :]
        o_ref[0, rows, :] = (o * (gt * jax.nn.sigmoid(gt))).astype(BF16)
        return state_t

    lax.fori_loop(0, seq // c, step, jnp.zeros((LANE, LANE), F32))


def _hgrn2(pf, hg_lb, layer, seq, chunk):
    B = pf.shape[0]
    npair = HG_W // LANE
    sec = lambda off: (lambda b, p: (b, 0, off // LANE + p))
    return pl.pallas_call(
        functools.partial(_hgrn2_body, layer=layer, seq=seq, chunk=chunk),
        grid=(B, npair),
        in_specs=[pl.BlockSpec((1, seq, LANE), sec(PF_HQ)),
                  pl.BlockSpec((1, seq, LANE), sec(PF_HF)),
                  pl.BlockSpec((1, seq, LANE), sec(PF_HI)),
                  pl.BlockSpec((1, seq, LANE), sec(PF_HG)),
                  pl.BlockSpec((DEPTH, LANE), lambda b, p: (0, p))],
        out_specs=pl.BlockSpec((1, seq, LANE), lambda b, p: (b, 0, p)),
        out_shape=jax.ShapeDtypeStruct((B, seq, HG_W), BF16),
        compiler_params=pltpu.CompilerParams(dimension_semantics=("arbitrary",) * 2,
                                             vmem_limit_bytes=VMEM_LIMIT),
        name="hgrn2",
    )(pf, pf, pf, pf, hg_lb)


def _outproj_body(nsa_ref, d0_ref, d1_ref, d2_ref, l0_ref, l1_ref, l2_ref, hg_ref, h_ref,
                  wn_ref, wd_ref, wh_ref, g_ref, out_ref):
    ls = [l0_ref[...], l1_ref[...], l2_ref[...]]
    lm = jnp.maximum(jnp.maximum(ls[0], ls[1]), ls[2])
    es = [jnp.exp(l - lm) for l in ls]
    inv = 1.0 / (es[0] + es[1] + es[2])
    acc = _dot(nsa_ref[...], wn_ref[...])
    for g, d_ref in enumerate((d0_ref, d1_ref, d2_ref)):
        acc = acc + _dot((d_ref[...].astype(F32) * (es[g] * inv)).astype(BF16), wd_ref[g])
    acc = acc + _dot(hg_ref[...], wh_ref[...])
    out_ref[...] = h_ref[...] + _rms(acc, g_ref[...])


def _outproj(nsa, dil_o, dil_l, hg, h2, wn, wd, wh, g, tm):
    T = h2.shape[0]
    row = lambda i: (i, 0)
    full = lambda i: (0, 0)
    return pl.pallas_call(
        _outproj_body,
        grid=(T // tm,),
        in_specs=[pl.BlockSpec((tm, NSA_HEADS * LANE), row)]
        + [pl.BlockSpec((tm, MIX_DIL_W), row)] * 6
        + [pl.BlockSpec((tm, HG_W), row), pl.BlockSpec((tm, D_MODEL), row),
           pl.BlockSpec(wn.shape, full), pl.BlockSpec(wd.shape, lambda i: (0, 0, 0)),
           pl.BlockSpec(wh.shape, full), pl.BlockSpec((1, D_MODEL), full)],
        out_specs=pl.BlockSpec((tm, D_MODEL), row),
        out_shape=jax.ShapeDtypeStruct((T, D_MODEL), F32),
        compiler_params=pltpu.CompilerParams(dimension_semantics=("arbitrary",),
                                             vmem_limit_bytes=VMEM_LIMIT),
        name="outproj",
    )(nsa, *dil_o, *dil_l, hg, h2, wn, wd, wh, g)


def _mlp_body(h_ref, p_ref, gpre_ref, wup_ref, wdn_ref, gpost_ref, gple_ref, wg_ref, wp_ref, out_ref, *, fc):
    h = h_ref[...]
    hn = _rms(h, gpre_ref[...]).astype(BF16)
    acc = jnp.zeros(h.shape, F32)
    for c in range(0, D_FF, fc):
        u = jnp.maximum(_dot(hn, wup_ref[:, c:c + fc]), 0.0)
        acc = acc + _dot((u * u).astype(BF16), wdn_ref[c:c + fc, :])
    h = h + _rms(acc, gpost_ref[...])
    gate = jax.nn.sigmoid(_dot(_rms(h, gple_ref[...]).astype(BF16), wg_ref[...]))
    out_ref[...] = h + _dot(p_ref[...].astype(BF16), wp_ref[...]) * gate


def _mlp(h2, p2, gpre, wup, wdn, gpost, gple, wg, wp, tm):
    T = h2.shape[0]
    row = lambda i: (i, 0)
    full = lambda i: (0, 0)
    vec = pl.BlockSpec((1, D_MODEL), full)
    return pl.pallas_call(
        functools.partial(_mlp_body, fc=512),
        grid=(T // tm,),
        in_specs=[pl.BlockSpec((tm, D_MODEL), row), pl.BlockSpec((tm, PLE_DIM), row), vec,
                  pl.BlockSpec(wup.shape, full), pl.BlockSpec(wdn.shape, full), vec, vec,
                  pl.BlockSpec(wg.shape, full), pl.BlockSpec(wp.shape, full)],
        out_specs=pl.BlockSpec((tm, D_MODEL), row),
        out_shape=jax.ShapeDtypeStruct((T, D_MODEL), F32),
        compiler_params=pltpu.CompilerParams(dimension_semantics=("arbitrary",),
                                             vmem_limit_bytes=VMEM_LIMIT),
        name="mlp_ple",
    )(h2, p2, gpre, wup, wdn, gpost, gple, wg, wp)


def _inproj_columns():
    zero = IN_TOTAL
    c16 = np.full((W16,), zero, np.int64)
    for hd in range(NSA_HEADS):
        c16[P16_NQ + hd * LANE:P16_NQ + hd * LANE + HEAD_DIM] = OFF_NQ + hd * HEAD_DIM + np.arange(HEAD_DIM)
    for kh in range(NSA_KV_HEADS):
        for base, ko, vo in ((P16_KVS, OFF_NKS, OFF_NVS), (P16_KVW, OFF_NKW, OFF_NVW)):
            o = base + kh * LANE
            c16[o:o + HEAD_DIM] = ko + kh * HEAD_DIM + np.arange(HEAD_DIM)
            c16[o + HEAD_DIM:o + LANE] = vo + kh * HEAD_DIM + np.arange(HEAD_DIM)
    for g in range(len(DIL_PAIRS)):
        base = P16_DIL + g * DIL_GROUP_W
        for i in range(DIL_HPG):
            hd = g * DIL_HPG + i
            c16[base + i * LANE:base + i * LANE + HEAD_DIM] = OFF_DQ + hd * HEAD_DIM + np.arange(HEAD_DIM)
            o = base + DIL_HPG * LANE + i * LANE
            c16[o:o + HEAD_DIM] = OFF_DK + hd * HEAD_DIM + np.arange(HEAD_DIM)
            c16[o + HEAD_DIM:o + LANE] = OFF_DV + hd * HEAD_DIM + np.arange(HEAD_DIM)
    cf = np.full((WF,), zero, np.int64)
    for dst, src in ((PF_HQ, OFF_HQ), (PF_HF, OFF_HF), (PF_HI, OFF_HI), (PF_HG, OFF_HG)):
        cf[dst:dst + HG_W] = src + np.arange(HG_W)
    cf[PF_GATE:PF_GATE + 3 * NSA_HEADS] = OFF_GATE + np.arange(3 * NSA_HEADS)
    cc = np.concatenate([OFF_NKC + np.arange(NSA_KV_W), OFF_NVC + np.arange(NSA_KV_W)])
    return c16, cf, cc


def _outproj_rows():
    zero = NSA_Q_W + DIL_W + HG_W
    rn = np.full((NSA_HEADS * LANE,), zero, np.int64)
    for hd in range(NSA_HEADS):
        rn[hd * LANE + HEAD_DIM:(hd + 1) * LANE] = hd * HEAD_DIM + np.arange(HEAD_DIM)
    rd = np.full((len(DIL_PAIRS), MIX_DIL_W), zero, np.int64)
    for g in range(len(DIL_PAIRS)):
        for i in range(DIL_HPG):
            rd[g, i * LANE + HEAD_DIM:(i + 1) * LANE] = NSA_Q_W + (g * DIL_HPG + i) * HEAD_DIM + np.arange(HEAD_DIM)
    return rn, rd


def kernel(x, p, w_in, w_out, cmp_pos, cmp_w1, cmp_w2, hg_lb, g_pre_mix, g_post_mix,
           g_pre_mlp, g_post_mlp, w_up, w_down, g_ple, w_ple_gate, w_ple_proj):
    B, S, D = x.shape
    assert D == D_MODEL and S % 256 == 0 and S % (CMP_STRIDE * 8) == 0 and S // SLC_BLOCK <= LANE
    T = B * S
    tm = 512 if T % 512 == 0 else 256
    tm_mlp = 256
    tq_nsa, tk_nsa = 256, 256
    hg_chunk = 128
    nsa_slopes, dil_slopes = _alibi_slopes()
    c16, cf, cc = _inproj_columns()
    rn, rd = _outproj_rows()
    n_cmp = (S - CMP_BLOCK) // CMP_STRIDE + 1
    nr = S // CMP_STRIDE
    ncp = -(-nr // LANE) * LANE
    e_sel = jnp.asarray((np.arange(S)[None, :] // SLC_BLOCK) == np.arange(LANE)[:, None], BF16)

    h = x.reshape(T, D)
    for l in range(DEPTH):
        w_ext = jnp.concatenate([w_in[l], jnp.zeros((D, 1), w_in.dtype)], axis=1)
        w16 = w_ext[:, c16].astype(BF16)
        wf = w_ext[:, cf].astype(BF16)
        wc = w_ext[:, cc].astype(BF16)
        p16, pf, kc0, kc1, vc0, vc1 = _inproj(h, g_pre_mix[l][None, :], w16, wf, wc, tm)
        p16 = p16.reshape(B, S, W16)
        pf = pf.reshape(B, S, WF)

        blk16 = lambda a: a.reshape(B, nr, CMP_STRIDE * HEAD_DIM)
        pos8 = jnp.pad(cmp_pos[l].reshape(2, 1, CMP_BLOCK * HEAD_DIM), ((0, 0), (0, 7), (0, 0))).astype(BF16)
        w2p = jnp.stack([jnp.pad(cmp_w2[l, 0], ((0, 0), (0, HEAD_DIM))),
                         jnp.pad(cmp_w2[l, 1], ((0, 0), (HEAD_DIM, 0)))]).astype(BF16)
        cmp_kv = _compress(blk16(kc0), blk16(kc1), blk16(vc0), blk16(vc1), pos8,
                           cmp_w1[l].astype(BF16), w2p, ncp, n_cmp)

        o_nsa = _nsa2(p16, pf, cmp_kv, S, tq_nsa, tk_nsa, nsa_slopes).reshape(T, NSA_HEADS * LANE)
        dil = [_dilated(p16, S, g, dil_slopes, 128) for g in range(len(DIL_PAIRS))]
        o_hg = _hgrn2(pf, hg_lb, l, S, hg_chunk).reshape(T, HG_W)

        wo_ext = jnp.concatenate([w_out[l], jnp.zeros((1, D), w_out.dtype)], axis=0)
        wn = wo_ext[rn].astype(BF16)
        wd = wo_ext[rd].astype(BF16)
        wh = w_out[l, NSA_Q_W + DIL_W:].astype(BF16)
        h = _outproj(o_nsa, [d[0] for d in dil], [d[1] for d in dil], o_hg, h, wn, wd, wh,
                     g_post_mix[l][None, :], tm)
        h = _mlp(h, p[l].reshape(T, PLE_DIM), g_pre_mlp[l][None, :], w_up[l].astype(BF16),
                 w_down[l].astype(BF16), g_post_mlp[l][None, :], g_ple[l][None, :],
                 w_ple_gate[l].astype(BF16), w_ple_proj[l].astype(BF16), tm_mlp)
    return h.reshape(B, S, D)
```

```python
import functools

import numpy as np
import jax
import jax.numpy as jnp
from jax import lax
from jax.experimental import pallas as pl
from jax.experimental.pallas import tpu as pltpu

F32 = jnp.float32
BF16 = jnp.bfloat16

D_MODEL = 1024
DEPTH = 2
HEAD_DIM = 64
NSA_HEADS = 6
NSA_KV_HEADS = 2
NSA_GROUP = NSA_HEADS // NSA_KV_HEADS
CMP_BLOCK = 32
CMP_STRIDE = 16
CMP_HIDDEN = 256
SLC_BLOCK = 64
SLC_TOPN = 8
WIN = 512
FORCE_SCORE = 1e9
DIL_PAIRS = ((128, 1), (512, 4), (2048, 16))
DIL_HPG = 2
DIL_HEADS = DIL_HPG * len(DIL_PAIRS)
HG_HEADS = 4
HG_DIM = 64
LB_TINY = 1e-30
D_FF = 4 * D_MODEL
PLE_DIM = 256
EPS = 1e-6
NEG = -1e30

NSA_Q_W = NSA_HEADS * HEAD_DIM
NSA_KV_W = NSA_KV_HEADS * HEAD_DIM
DIL_W = DIL_HEADS * HEAD_DIM
HG_W = HG_HEADS * HG_DIM
IN_WIDTHS = (NSA_Q_W,) + (NSA_KV_W,) * 6 + (3 * NSA_HEADS,) + (DIL_W,) * 3 + (HG_W,) * 4
IN_TOTAL = sum(IN_WIDTHS)
IN_OFF = tuple(int(v) for v in np.cumsum((0,) + IN_WIDTHS))
(OFF_NQ, OFF_NKC, OFF_NVC, OFF_NKS, OFF_NVS, OFF_NKW, OFF_NVW, OFF_GATE,
 OFF_DQ, OFF_DK, OFF_DV, OFF_HQ, OFF_HF, OFF_HI, OFF_HG) = IN_OFF[:-1]

LANE = 128
VMEM_LIMIT = 56 * 1024 * 1024

P16_NQ = 0
P16_KVS = P16_NQ + NSA_HEADS * LANE
P16_KVW = P16_KVS + NSA_KV_HEADS * LANE
P16_DIL = P16_KVW + NSA_KV_HEADS * LANE
DIL_GROUP_W = 2 * DIL_HPG * LANE
WA = P16_DIL + DIL_GROUP_W
W16 = P16_DIL + len(DIL_PAIRS) * DIL_GROUP_W
PF_HQ, PF_HF, PF_HI, PF_HG, PF_GATE = 0, HG_W, 2 * HG_W, 3 * HG_W, 4 * HG_W
WF = PF_GATE + LANE
WC = 2 * NSA_KV_W
MIX_DIL_W = DIL_HPG * LANE
CMP_ROW_W = CMP_STRIDE * NSA_KV_W


def _dot(a, b):
    return jnp.dot(a, b, preferred_element_type=F32)


def _dot_nt(a, b):
    return lax.dot_general(a, b, (((1,), (1,)), ((), ())), preferred_element_type=F32)


def _rms(x, g):
    return x * lax.rsqrt(jnp.mean(x * x, axis=-1, keepdims=True) + EPS) * g


def _log2(n):
    l = int(n).bit_length() - 1
    assert (1 << l) == n, n
    return l


def _alibi_slopes():
    n = NSA_HEADS + DIL_HEADS
    s = 2.0 ** (-8.0 * np.arange(1, n + 1) / n)
    quads = s.reshape(-1, 4)
    nsa = [float(np.float32(v)) for v in quads[:, 2:].reshape(-1)]
    dil = [float(np.float32(v)) for v in quads[:, :2].reshape(-1)]
    return nsa, dil


def _inproj_body(x_ref, g_ref, wa_ref, wf_ref, wd_ref, wc_ref,
                 oa_ref, of_ref, od1_ref, od2_ref, xk_ref, xv_ref, hn_scr, *, tm):
    hn32 = _rms(x_ref[...], g_ref[...])
    hn = hn32.astype(BF16)
    for c in range(0, WA, 256):
        oa_ref[:, c:c + 256] = _dot(hn, wa_ref[:, c:c + 256]).astype(BF16)
    for c in range(0, WF, 384):
        of_ref[:, c:c + 384] = _dot(hn, wf_ref[:, c:c + 384])
    nlb = D_MODEL // LANE
    for c in range(nlb):
        hn_scr[c] = hn32[:, c * LANE:(c + 1) * LANE]

    def by_residue(d):
        parts = [jnp.concatenate([hn_scr[c, pl.ds(r, tm // d, stride=d), :] for c in range(nlb)], axis=1)
                 for r in range(d)]
        return jnp.concatenate(parts, axis=0).astype(BF16)

    for gi, o_ref in ((1, od1_ref), (2, od2_ref)):
        hp = by_residue(DIL_PAIRS[gi][1])
        for c in range(0, DIL_GROUP_W, 256):
            o_ref[:, c:c + 256] = _dot(hp, wd_ref[gi - 1, :, c:c + 256]).astype(BF16)
    assert DIL_PAIRS[2][1] == CMP_STRIDE
    cc = _dot(hp, wc_ref[...])
    nr = tm // CMP_STRIDE
    for j in range(CMP_STRIDE):
        xk_ref[:, j * NSA_KV_W:(j + 1) * NSA_KV_W] = cc[j * nr:(j + 1) * nr, 0:NSA_KV_W]
        xv_ref[:, j * NSA_KV_W:(j + 1) * NSA_KV_W] = cc[j * nr:(j + 1) * nr, NSA_KV_W:2 * NSA_KV_W]


def _inproj(x2, g, wa, wf, wd, wc, tm):
    T = x2.shape[0]
    row = lambda i: (i, 0)
    full = lambda i: (0, 0)
    nr = tm // CMP_STRIDE
    return pl.pallas_call(
        functools.partial(_inproj_body, tm=tm),
        grid=(T // tm,),
        in_specs=[pl.BlockSpec((tm, D_MODEL), row),
                  pl.BlockSpec((1, D_MODEL), full),
                  pl.BlockSpec((D_MODEL, WA), full),
                  pl.BlockSpec((D_MODEL, WF), full),
                  pl.BlockSpec((2, D_MODEL, DIL_GROUP_W), lambda i: (0, 0, 0)),
                  pl.BlockSpec((D_MODEL, WC), full)],
        out_specs=[pl.BlockSpec((tm, WA), row), pl.BlockSpec((tm, WF), row),
                   pl.BlockSpec((tm, DIL_GROUP_W), row), pl.BlockSpec((tm, DIL_GROUP_W), row),
                   pl.BlockSpec((nr, CMP_ROW_W), row), pl.BlockSpec((nr, CMP_ROW_W), row)],
        out_shape=[jax.ShapeDtypeStruct((T, WA), BF16), jax.ShapeDtypeStruct((T, WF), F32),
                   jax.ShapeDtypeStruct((T, DIL_GROUP_W), BF16), jax.ShapeDtypeStruct((T, DIL_GROUP_W), BF16),
                   jax.ShapeDtypeStruct((T // CMP_STRIDE, CMP_ROW_W), F32),
                   jax.ShapeDtypeStruct((T // CMP_STRIDE, CMP_ROW_W), F32)],
        scratch_shapes=[pltpu.VMEM((D_MODEL // LANE, tm, LANE), F32)],
        compiler_params=pltpu.CompilerParams(dimension_semantics=("arbitrary",),
                                             vmem_limit_bytes=VMEM_LIMIT),
        name="inproj",
    )(x2, g, wa, wf, wd, wc)


def _compress_body(xk_ref, xv_ref, pos_ref, w1_ref, w1x_ref, w2_ref, out_ref, *, n_cmp):
    nr = xk_ref.shape[1]
    rows = lax.broadcasted_iota(jnp.int32, (nr, 1), 0)
    out_ref[...] = jnp.zeros(out_ref.shape, out_ref.dtype)
    xs = (xk_ref[0].astype(BF16), xv_ref[0].astype(BF16))
    for h in range(NSA_KV_HEADS):
        acc = jnp.zeros((nr, LANE), F32)
        for ten in range(2):
            first = _dot(xs[ten], w1x_ref[ten, h, 0])
            second = _dot(xs[ten], w1x_ref[ten, h, 1])
            posb = _dot(pos_ref[ten], w1_ref[ten])[0:1, :]
            hid = first + pltpu.roll(second, nr - 1, 0) + posb
            act = hid * jax.nn.sigmoid(hid)
            acc = acc + _dot(act.astype(BF16), w2_ref[ten])
        acc = jnp.where(rows < n_cmp, acc, 0.0)
        out_ref[0, 0:nr, h * LANE:(h + 1) * LANE] = acc.astype(BF16)


def _compress(xk, xv, pos8, w1, w1x, w2p, ncp, n_cmp):
    B, nr, kw = xk.shape
    seq = lambda b: (b, 0, 0)
    full3 = lambda b: (0, 0, 0)
    return pl.pallas_call(
        functools.partial(_compress_body, n_cmp=n_cmp),
        grid=(B,),
        in_specs=[pl.BlockSpec((1, nr, kw), seq)] * 2
        + [pl.BlockSpec(pos8.shape, full3), pl.BlockSpec(w1.shape, full3),
           pl.BlockSpec(w1x.shape, lambda b: (0, 0, 0, 0, 0)), pl.BlockSpec(w2p.shape, full3)],
        out_specs=pl.BlockSpec((1, ncp, NSA_KV_HEADS * LANE), seq),
        out_shape=jax.ShapeDtypeStruct((B, ncp, NSA_KV_HEADS * LANE), BF16),
        compiler_params=pltpu.CompilerParams(dimension_semantics=("arbitrary",),
                                             vmem_limit_bytes=VMEM_LIMIT),
        name="nsa_compress",
    )(xk, xv, pos8, w1, w1x, w2p)


NSA_PAT_BLK = HEAD_DIM
NSA_PAT_POS = HEAD_DIM + 32


def _nsa_pattern(seq):
    pos = np.arange(seq)
    pat = np.zeros((seq, LANE), np.float32)
    pat[pos, NSA_PAT_BLK + pos // SLC_BLOCK] = 1.0
    pat[:, NSA_PAT_POS:NSA_PAT_POS + 3] = (SLC_BLOCK * (pos // SLC_BLOCK))[:, None]
    pat[:, NSA_PAT_POS + 3:NSA_PAT_POS + 6] = (pos % SLC_BLOCK)[:, None]
    return jnp.asarray(pat, BF16)


def _nsa_slope_rows(nsa_slopes):
    rows = np.zeros((NSA_KV_HEADS, 8, LANE), np.float32)
    for kh in range(NSA_KV_HEADS):
        for g in range(NSA_GROUP):
            rest = np.float32(nsa_slopes[kh * NSA_GROUP + g])
            for part in range(3):
                piece = np.float32(np.asarray(rest, dtype=BF16))
                rows[kh, g, NSA_PAT_POS + part] = piece
                rows[kh, g, NSA_PAT_POS + 3 + part] = piece
                rest = np.float32(rest - piece)
    return jnp.asarray(rows)


def _nsa_body(q_ref, kvs_ref, kvw_ref, kvc_ref, gate_ref, pat_ref, sl_ref, out_ref,
              ks_scr, vs_scr, kw_scr, vw_scr, s_scr, *, seq, tq, tk, n_cmp, n_slc, k_top, slopes):
    kh = pl.program_id(1)
    qi = pl.program_id(2)
    G = NSA_GROUP
    t0 = qi * tq
    ncp = kvc_ref.shape[1]
    ltk = _log2(tk)
    lane = lax.broadcasted_iota(jnp.int32, (1, LANE), 1)
    lo_half = lane < HEAD_DIM

    @pl.when(qi == 0)
    def _():
        pat = pat_ref[...]
        kvs = kvs_ref[0]
        kvw = kvw_ref[0]
        one = jnp.ones(kvs.shape, BF16)
        ks_scr[...] = jnp.where(lo_half, kvs, pat)
        kw_scr[...] = jnp.where(lo_half, kvw, pat)
        vs_scr[...] = jnp.where(lo_half, one, kvs)
        vw_scr[...] = jnp.where(lo_half, one, kvw)

    ii = lax.broadcasted_iota(jnp.int32, (tq, 1), 0)
    row_t = t0 + ii
    slope = [jnp.where(kh == 0, slopes[g], slopes[G + g]).astype(F32) for g in range(G)]
    qs = [q_ref[0, :, g * LANE:(g + 1) * LANE] * (HEAD_DIM ** -0.5) for g in range(G)]
    q3 = jnp.concatenate(qs, axis=0)

    kvc = kvc_ref[0]
    nn = lax.broadcasted_iota(jnp.int32, (1, ncp), 1)
    maskc = ((nn * CMP_STRIDE + (CMP_BLOCK - 1)) <= row_t) & (nn < n_cmp)
    absd = jnp.abs(row_t.astype(F32) - (nn.astype(F32) * CMP_STRIDE + 0.5 * (CMP_BLOCK - 1)))
    s_c = _dot_nt(q3, kvc)
    ps = []
    for g in range(G):
        s = jnp.where(maskc, s_c[g * tq:(g + 1) * tq] - slope[g] * absd, NEG)
        m = jnp.max(s, axis=-1, keepdims=True)
        e = jnp.where(maskc, jnp.exp(s - m), 0.0)
        ps.append(e / jnp.maximum(jnp.sum(e, axis=-1, keepdims=True), 1e-30))
    o_cmp = _dot(jnp.concatenate(ps, axis=0).astype(BF16), kvc)
    psum = ps[0] + ps[1] + ps[2]

    mi = lax.broadcasted_iota(jnp.int32, (LANE, 1), 0)
    ov_t = ((nn * CMP_STRIDE < mi * SLC_BLOCK + SLC_BLOCK) & (nn * CMP_STRIDE + CMP_BLOCK > mi * SLC_BLOCK)
            & (nn < n_cmp) & (mi < n_slc))
    ov_t = jnp.where(ov_t, 1.0, 0.0).astype(BF16)
    p_hi = psum.astype(BF16)
    p_lo = (psum - p_hi.astype(F32)).astype(BF16)
    imp_t = _dot_nt(ov_t, p_hi) + _dot_nt(ov_t, p_lo)
    nsp = -(-n_slc // 8) * 8
    mi_s = mi[0:nsp]
    cur_l = (t0 + lax.broadcasted_iota(jnp.int32, (1, tq), 1)) >> _log2(SLC_BLOCK)
    in_rng = mi_s < n_slc
    valid = (mi_s <= cur_l) & in_rng
    forced = (mi_s == 0) | (mi_s == cur_l) | (mi_s == cur_l - 1)
    score = jnp.where(valid, jnp.where(forced, FORCE_SCORE, imp_t[0:nsp]), -FORCE_SCORE)
    score = jnp.where(in_rng, score, -3.0 * FORCE_SCORE)
    rank = jnp.zeros((nsp, tq), F32)
    for mp in range(n_slc):
        row = score[mp:mp + 1, :]
        beats = (row > score) | ((row == score) & (mp < mi_s))
        rank = rank + jnp.where(beats, 1.0, 0.0)
    neg_t = jnp.where((rank < k_top) & valid, 0.0, NEG)
    neg_t = jnp.concatenate([jnp.zeros((NSA_PAT_BLK, tq), F32), neg_t,
                             jnp.zeros((LANE - NSA_PAT_BLK - nsp, tq), F32)], axis=0)
    neg = neg_t.T

    def extended(with_mask):
        out = []
        for g in range(G):
            extra = sl_ref[0, g:g + 1, :] + (neg if with_mask else 0.0)
            out.append(jnp.where(lo_half, qs[g], extra.astype(BF16)))
        return jnp.concatenate(out, axis=0)

    hi_half = lane >= HEAD_DIM

    def normalize(acc):
        den = pltpu.roll(acc, HEAD_DIM, 1)
        return jnp.where(hi_half, acc / jnp.where(hi_half, den, 1.0), 0.0)

    n_kt = (t0 + tq + tk - 1) >> ltk

    nwt = min(-(-(WIN + tq) // tk), seq // tk)
    wk = nwt * tk
    kt0 = jnp.clip(n_kt - nwt, 0, seq // tk - nwt)
    k0 = pl.multiple_of(kt0 * tk, tk)
    span = pl.ds(k0, wk)
    rel = row_t - (k0 + lax.broadcasted_iota(jnp.int32, (1, wk), 1))
    ok_causal = jnp.where(rel >= 0, 0.0, NEG)
    ok_win = jnp.where(rel < WIN, ok_causal, NEG)

    def masked(s, bias):
        return (s.reshape(G, tq, wk) + bias[None]).reshape(G * tq, wk)

    s = masked(_dot_nt(extended(False), kw_scr[span, :]), ok_win)
    p = jnp.exp(s - jnp.max(s, axis=-1, keepdims=True))
    o_win = normalize(_dot(p.astype(BF16), vw_scr[span, :]))

    q_sel = extended(True)

    def far_scores(kt, m_acc):
        s = _dot_nt(q_sel, ks_scr[pl.ds(pl.multiple_of(kt * tk, tk), tk), :])
        s_scr[kt] = s
        for c in range(0, tk, LANE):
            m_acc = jnp.maximum(m_acc, s[:, c:c + LANE])
        return m_acc

    m_far = lax.fori_loop(0, kt0, far_scores, jnp.full((G * tq, LANE), NEG, F32))
    s_near = masked(_dot_nt(q_sel, ks_scr[span, :]), ok_causal)
    m_row = jnp.maximum(jnp.max(m_far, axis=-1, keepdims=True), jnp.max(s_near, axis=-1, keepdims=True))
    acc = _dot(jnp.exp(s_near - m_row).astype(BF16), vs_scr[span, :])

    def far_accum(kt, acc):
        p = jnp.exp(s_scr[kt] - m_row)
        return acc + _dot(p.astype(BF16), vs_scr[pl.ds(pl.multiple_of(kt * tk, tk), tk), :])

    o_sel = normalize(lax.fori_loop(0, kt0, far_accum, acc))

    gates = jax.nn.sigmoid(gate_ref[0])
    for g in range(G):
        rows = slice(g * tq, (g + 1) * tq)
        gc = [jnp.where(kh == 0, gates[:, g * 3 + c:g * 3 + c + 1],
                        gates[:, (G + g) * 3 + c:(G + g) * 3 + c + 1]) for c in range(3)]
        o = gc[0] * o_cmp[rows] + gc[1] * o_sel[rows] + gc[2] * o_win[rows]
        out_ref[0, :, g * LANE:(g + 1) * LANE] = o.astype(BF16)


def _nsa(pa, pf, cmp_kv, seq, tq, tk, nsa_slopes):
    B = pa.shape[0]
    n_cmp = (seq - CMP_BLOCK) // CMP_STRIDE + 1
    n_slc = seq // SLC_BLOCK
    ncp = cmp_kv.shape[1]
    qw = NSA_GROUP * LANE
    pat = _nsa_pattern(seq)
    sl_rows = _nsa_slope_rows(nsa_slopes)
    body = functools.partial(_nsa_body, seq=seq, tq=tq, tk=tk, n_cmp=n_cmp, n_slc=n_slc,
                             k_top=min(SLC_TOPN, n_slc), slopes=tuple(nsa_slopes))
    return pl.pallas_call(
        body,
        grid=(B, NSA_KV_HEADS, seq // tq),
        in_specs=[pl.BlockSpec((1, tq, qw), lambda b, k, i: (b, i, k)),
                  pl.BlockSpec((1, seq, LANE), lambda b, k, i: (b, 0, P16_KVS // LANE + k)),
                  pl.BlockSpec((1, seq, LANE), lambda b, k, i: (b, 0, P16_KVW // LANE + k)),
                  pl.BlockSpec((1, ncp, LANE), lambda b, k, i: (b, 0, k)),
                  pl.BlockSpec((1, tq, LANE), lambda b, k, i: (b, i, PF_GATE // LANE)),
                  pl.BlockSpec((seq, LANE), lambda b, k, i: (0, 0)),
                  pl.BlockSpec((1, 8, LANE), lambda b, k, i: (k, 0, 0))],
        out_specs=pl.BlockSpec((1, tq, qw), lambda b, k, i: (b, i, k)),
        out_shape=jax.ShapeDtypeStruct((B, seq, NSA_HEADS * LANE), BF16),
        scratch_shapes=[pltpu.VMEM((seq, LANE), BF16)] * 4
        + [pltpu.VMEM((seq // tk, NSA_GROUP * tq, tk), F32)],
        compiler_params=pltpu.CompilerParams(dimension_semantics=("arbitrary",) * 3,
                                             vmem_limit_bytes=VMEM_LIMIT),
        name="nsa_attention",
    )(pa, pa, pa, cmp_kv, pf, pat, sl_rows)


def _dilated_body(q_ref, kv_ref, o_ref, lse_ref, *, nres, blk, ls, dil, win_keys, slopes, tq):
    ii = lax.broadcasted_iota(jnp.int32, (tq, 1), 0)
    jj = lax.broadcasted_iota(jnp.int32, (1, 2 * tq), 1)
    rel2 = ii - jj + tq
    ok2 = (rel2 >= 0) & (rel2 <= win_keys)
    rel1 = rel2[:, tq:]
    ok1 = ok2[:, tq:]

    def rows(ref, r, t, lanes):
        if blk >= tq:
            a, b = divmod(t * tq, blk)
            return ref[0, a, r, b:b + tq, lanes]
        n = tq // blk
        return ref[0, t * n:(t + 1) * n, r, :, lanes].reshape(tq, LANE)

    def put(ref, r, t, lanes, val):
        if blk >= tq:
            a, b = divmod(t * tq, blk)
            ref[0, a, r, b:b + tq, lanes] = val
        else:
            n = tq // blk
            ref[0, t * n:(t + 1) * n, r, :, lanes] = val.reshape(n, blk, LANE)

    for r in range(nres):
        for h in range(DIL_HPG):
            lanes = slice(h * LANE, (h + 1) * LANE)
            bias2 = (slopes[h] * float(dil)) * rel2.astype(F32)
            for t in range(ls // tq):
                qt = rows(q_ref, r, t, lanes) * (HEAD_DIM ** -0.5)
                if t == 0:
                    kv = rows(kv_ref, r, 0, lanes)
                    s = jnp.where(ok1, _dot_nt(qt, kv) - bias2[:, tq:], NEG)
                else:
                    kv = jnp.concatenate([rows(kv_ref, r, t - 1, lanes), rows(kv_ref, r, t, lanes)], axis=0)
                    s = jnp.where(ok2, _dot_nt(qt, kv) - bias2, NEG)
                m = jnp.max(s, axis=-1, keepdims=True)
                e = jnp.exp(s - m)
                l = jnp.sum(e, axis=-1, keepdims=True)
                o = _dot(e.astype(BF16), kv) / l
                put(o_ref, r, t, lanes, o.astype(BF16))
                put(lse_ref, r, t, lanes, jnp.broadcast_to(m + jnp.log(l), (tq, LANE)))


def _dilated(src, col0, batch, seq, tm, group, slopes, tq):
    win, dil = DIL_PAIRS[group]
    ls = seq // dil
    tq = min(tq, ls)
    if dil == 1:
        ntile, blk = 1, seq
    else:
        ntile, blk = seq // tm, tm // dil
    nres = max(1, min(dil, 16 * tq // ls))
    width = src.shape[1]
    view = src.reshape(batch, ntile, dil, blk, width)
    qc = col0 // MIX_DIL_W
    body = functools.partial(_dilated_body, nres=nres, blk=blk, ls=ls, dil=dil, win_keys=win // dil,
                             slopes=tuple(slopes[group * DIL_HPG:(group + 1) * DIL_HPG]), tq=tq)
    shp = (1, ntile, nres, blk, MIX_DIL_W)
    o, lse = pl.pallas_call(
        body,
        grid=(batch, dil // nres),
        in_specs=[pl.BlockSpec(shp, lambda b, r: (b, 0, r, 0, qc)),
                  pl.BlockSpec(shp, lambda b, r: (b, 0, r, 0, qc + 1))],
        out_specs=[pl.BlockSpec(shp, lambda b, r: (b, 0, r, 0, 0))] * 2,
        out_shape=[jax.ShapeDtypeStruct((batch, ntile, dil, blk, MIX_DIL_W), BF16),
                   jax.ShapeDtypeStruct((batch, ntile, dil, blk, MIX_DIL_W), F32)],
        compiler_params=pltpu.CompilerParams(dimension_semantics=("arbitrary",) * 2,
                                             vmem_limit_bytes=VMEM_LIMIT),
        name=f"dilated_attention_g{group}",
    )(view, view)
    return o.reshape(batch * seq, MIX_DIL_W), lse.reshape(batch * seq, MIX_DIL_W)


def _hgrn2_body(q_ref, f_ref, i_ref, g_ref, lb_ref, o_ref, *, layer, seq, chunk):
    c = chunk
    sub = 8
    lbs = lb_ref[...].astype(F32)
    mx = jnp.max(lbs, axis=0, keepdims=True)
    ex = jnp.exp(lbs - mx)
    sm = ex / jnp.sum(ex, axis=0, keepdims=True)
    lower = jnp.maximum(jnp.sum(sm[0:layer + 1], axis=0, keepdims=True) - sm[0:1], 0.0)
    log_lb = jnp.log(lower + LB_TINY)
    log_1m = jnp.log1p(-lower)

    lane = lax.broadcasted_iota(jnp.int32, (1, LANE), 1)
    head0 = lane < HG_DIM
    ri = lax.broadcasted_iota(jnp.int32, (c, 1), 0)
    ci = lax.broadcasted_iota(jnp.int32, (1, c), 1)
    tri = jnp.where(ci <= ri, 1.0, 0.0).astype(BF16)
    di = lax.broadcasted_iota(jnp.int32, (LANE, 1), 0)
    same_head = (di >= HG_DIM) == (lane >= HG_DIM)
    ones_blk = jnp.where(same_head, 1.0, 0.0).astype(BF16)
    gcol = lax.broadcasted_iota(jnp.int32, (1, sub * c), 1)
    gsum = jnp.where(((gcol >> _log2(c)) == (ri & (sub - 1)))
                     & (((gcol & (c - 1)) >> 3) == (ri >> 3)), 1.0, 0.0).astype(BF16)
    sp = lax.broadcasted_iota(jnp.int32, (1, sub, 1), 1)
    levels = []
    w = sub
    while w < c:
        same = (ri >> _log2(2 * w)) == (ci >> _log2(2 * w))
        levels.append((w, same & ((ri & (2 * w - 1)) >= w) & ((ci & (2 * w - 1)) < w)))
        w *= 2

    def split3(x):
        hi = x.astype(BF16)
        r1 = x - hi.astype(F32)
        mid = r1.astype(BF16)
        lo = (r1 - mid.astype(F32)).astype(BF16)
        return hi, mid, lo

    def step(ic, state_t):
        rows = pl.ds(pl.multiple_of(ic * c, c), c)
        q = q_ref[0, rows, :]
        v = i_ref[0, rows, :]
        x = f_ref[0, rows, :]
        log_sig = jnp.minimum(x, 0.0) - jnp.log1p(jnp.exp(-jnp.abs(x)))
        t2 = log_1m + log_sig
        lf = jnp.maximum(log_lb, t2) + jnp.log1p(jnp.exp(-jnp.abs(log_lb - t2)))
        kk = 1.0 - jnp.exp(lf)
        hi, mid, lo = split3(lf)
        b = _dot(tri, hi) + _dot(tri, mid) + _dot(tri, lo)
        vb = v.astype(BF16)

        q3 = q.reshape(c // sub, sub, LANE)
        k3 = kk.reshape(c // sub, sub, LANE)
        b3 = b.reshape(c // sub, sub, LANE)
        parts = []
        for tp in range(sub):
            dec = jnp.exp(jnp.minimum(b3[:, tp:tp + 1, :] - b3, 0.0))
            parts.append(jnp.where(sp <= tp, q3[:, tp:tp + 1, :] * k3 * dec, 0.0).reshape(c, LANE))
        wall = jnp.concatenate(parts, axis=0)
        a_rep = _dot(wall.astype(BF16), ones_blk)
        z = a_rep * jnp.concatenate([v] * sub, axis=0)
        o = _dot(gsum, z.astype(BF16))

        a0 = jnp.zeros((c, c), F32)
        a1 = jnp.zeros((c, c), F32)
        for w, lmask in levels:
            b_r = b.reshape(c // (2 * w), 2 * w, LANE)
            bnd = jnp.broadcast_to(b_r[:, w - 1:w, :], b_r.shape).reshape(c, LANE)
            qe = q * jnp.exp(jnp.minimum(b - bnd, 0.0))
            ke = (kk * jnp.exp(jnp.minimum(bnd - b, 0.0))).astype(BF16)
            a0 = a0 + jnp.where(lmask, _dot_nt(jnp.where(head0, qe, 0.0).astype(BF16), ke), 0.0)
            a1 = a1 + jnp.where(lmask, _dot_nt(jnp.where(head0, 0.0, qe).astype(BF16), ke), 0.0)
        o = o + jnp.where(head0, _dot(a0.astype(BF16), vb), _dot(a1.astype(BF16), vb))

        o = o + _dot_nt((q * jnp.exp(b)).astype(BF16), state_t.astype(BF16))
        b_last = b[c - 1:c, :]
        khat = (kk * jnp.exp(b_last - b)).astype(BF16)
        upd = lax.dot_general(vb, khat, (((0,), (0,)), ((), ())), preferred_element_type=F32)
        state_t = jnp.exp(b_last) * state_t + jnp.where(same_head, upd, 0.0)

        o2 = o * o
        ms0 = jnp.sum(jnp.where(head0, o2, 0.0), axis=-1, keepdims=True)
        ms1 = jnp.sum(jnp.where(head0, 0.0, o2), axis=-1, keepdims=True)
        o = o * lax.rsqrt(jnp.where(head0, ms0, ms1) * (1.0 / HG_DIM) + EPS)
        gt = g_ref[0, rows, :]
        o_ref[0, rows, :] = (o * (gt * jax.nn.sigmoid(gt))).astype(BF16)
        return state_t

    lax.fori_loop(0, seq // c, step, jnp.zeros((LANE, LANE), F32))


def _hgrn2(pf, hg_lb, layer, seq, chunk):
    B = pf.shape[0]
    npair = HG_W // LANE
    sec = lambda off: (lambda b, p: (b, 0, off // LANE + p))
    return pl.pallas_call(
        functools.partial(_hgrn2_body, layer=layer, seq=seq, chunk=chunk),
        grid=(B, npair),
        in_specs=[pl.BlockSpec((1, seq, LANE), sec(PF_HQ)),
                  pl.BlockSpec((1, seq, LANE), sec(PF_HF)),
                  pl.BlockSpec((1, seq, LANE), sec(PF_HI)),
                  pl.BlockSpec((1, seq, LANE), sec(PF_HG)),
                  pl.BlockSpec((DEPTH, LANE), lambda b, p: (0, p))],
        out_specs=pl.BlockSpec((1, seq, LANE), lambda b, p: (b, 0, p)),
        out_shape=jax.ShapeDtypeStruct((B, seq, HG_W), BF16),
        compiler_params=pltpu.CompilerParams(dimension_semantics=("arbitrary",) * 2,
                                             vmem_limit_bytes=VMEM_LIMIT),
        name="hgrn2",
    )(pf, pf, pf, pf, hg_lb)


def _outproj_body(nsa_ref, d0_ref, d1_ref, d2_ref, l0_ref, l1_ref, l2_ref, hg_ref, h_ref,
                  wn_ref, wd_ref, wh_ref, g_ref, out_ref, tok_scr, *, tm):
    def token_order(ref, slot, d):
        if d == 1:
            return ref[...].astype(F32)
        n = tm // d
        for r in range(d):
            piece = ref[r * n:(r + 1) * n, :].astype(F32)
            for c in range(DIL_HPG):
                tok_scr[DIL_HPG * slot + c, pl.ds(r, n, stride=d), :] = piece[:, c * LANE:(c + 1) * LANE]
        return jnp.concatenate([tok_scr[DIL_HPG * slot + c] for c in range(DIL_HPG)], axis=1)

    dils = [d for _, d in DIL_PAIRS]
    os_ = [token_order(ref, 2 * g, dils[g]) for g, ref in enumerate((d0_ref, d1_ref, d2_ref))]
    ls = [token_order(ref, 2 * g + 1, dils[g]) for g, ref in enumerate((l0_ref, l1_ref, l2_ref))]
    lm = jnp.maximum(jnp.maximum(ls[0], ls[1]), ls[2])
    es = [jnp.exp(l - lm) for l in ls]
    inv = 1.0 / (es[0] + es[1] + es[2])
    acc = _dot(nsa_ref[...], wn_ref[...])
    for g in range(len(DIL_PAIRS)):
        acc = acc + _dot((os_[g] * (es[g] * inv)).astype(BF16), wd_ref[g])
    acc = acc + _dot(hg_ref[...], wh_ref[...])
    out_ref[...] = h_ref[...] + _rms(acc, g_ref[...])


def _outproj(nsa, dil_o, dil_l, hg, h2, wn, wd, wh, g, tm):
    T = h2.shape[0]
    row = lambda i: (i, 0)
    full = lambda i: (0, 0)
    return pl.pallas_call(
        functools.partial(_outproj_body, tm=tm),
        grid=(T // tm,),
        in_specs=[pl.BlockSpec((tm, NSA_HEADS * LANE), row)]
        + [pl.BlockSpec((tm, MIX_DIL_W), row)] * 6
        + [pl.BlockSpec((tm, HG_W), row), pl.BlockSpec((tm, D_MODEL), row),
           pl.BlockSpec(wn.shape, full), pl.BlockSpec(wd.shape, lambda i: (0, 0, 0)),
           pl.BlockSpec(wh.shape, full), pl.BlockSpec((1, D_MODEL), full)],
        out_specs=pl.BlockSpec((tm, D_MODEL), row),
        out_shape=jax.ShapeDtypeStruct((T, D_MODEL), F32),
        scratch_shapes=[pltpu.VMEM((2 * len(DIL_PAIRS) * DIL_HPG, tm, LANE), F32)],
        compiler_params=pltpu.CompilerParams(dimension_semantics=("arbitrary",),
                                             vmem_limit_bytes=VMEM_LIMIT),
        name="outproj",
    )(nsa, *dil_o, *dil_l, hg, h2, wn, wd, wh, g)


def _mlp_body(h_ref, p_ref, gpre_ref, wup_ref, wdn_ref, gpost_ref, gple_ref, wg_ref, wp_ref, out_ref, *, fc):
    h = h_ref[...]
    hn = _rms(h, gpre_ref[...]).astype(BF16)
    acc = jnp.zeros(h.shape, F32)
    for c in range(0, D_FF, fc):
        u = jnp.maximum(_dot(hn, wup_ref[:, c:c + fc]), 0.0)
        acc = acc + _dot((u * u).astype(BF16), wdn_ref[c:c + fc, :])
    h = h + _rms(acc, gpost_ref[...])
    gate = jax.nn.sigmoid(_dot(_rms(h, gple_ref[...]).astype(BF16), wg_ref[...]))
    out_ref[...] = h + _dot(p_ref[...].astype(BF16), wp_ref[...]) * gate


def _mlp(h2, p2, gpre, wup, wdn, gpost, gple, wg, wp, tm):
    T = h2.shape[0]
    row = lambda i: (i, 0)
    full = lambda i: (0, 0)
    vec = pl.BlockSpec((1, D_MODEL), full)
    resident = lambda shape: pl.BlockSpec(shape, full, pipeline_mode=pl.Buffered(1))
    return pl.pallas_call(
        functools.partial(_mlp_body, fc=512),
        grid=(T // tm,),
        in_specs=[pl.BlockSpec((tm, D_MODEL), row), pl.BlockSpec((tm, PLE_DIM), row), vec,
                  resident(wup.shape), resident(wdn.shape), vec, vec,
                  resident(wg.shape), resident(wp.shape)],
        out_specs=pl.BlockSpec((tm, D_MODEL), row),
        out_shape=jax.ShapeDtypeStruct((T, D_MODEL), F32),
        compiler_params=pltpu.CompilerParams(dimension_semantics=("arbitrary",),
                                             vmem_limit_bytes=VMEM_LIMIT),
        name="mlp_ple",
    )(h2, p2, gpre, wup, wdn, gpost, gple, wg, wp)


def _inproj_columns():
    zero = IN_TOTAL
    c16 = np.full((W16,), zero, np.int64)
    for hd in range(NSA_HEADS):
        c16[P16_NQ + hd * LANE:P16_NQ + hd * LANE + HEAD_DIM] = OFF_NQ + hd * HEAD_DIM + np.arange(HEAD_DIM)
    for kh in range(NSA_KV_HEADS):
        for base, ko, vo in ((P16_KVS, OFF_NKS, OFF_NVS), (P16_KVW, OFF_NKW, OFF_NVW)):
            o = base + kh * LANE
            c16[o:o + HEAD_DIM] = ko + kh * HEAD_DIM + np.arange(HEAD_DIM)
            c16[o + HEAD_DIM:o + LANE] = vo + kh * HEAD_DIM + np.arange(HEAD_DIM)
    for g in range(len(DIL_PAIRS)):
        base = P16_DIL + g * DIL_GROUP_W
        for i in range(DIL_HPG):
            hd = g * DIL_HPG + i
            c16[base + i * LANE:base + i * LANE + HEAD_DIM] = OFF_DQ + hd * HEAD_DIM + np.arange(HEAD_DIM)
            o = base + DIL_HPG * LANE + i * LANE
            c16[o:o + HEAD_DIM] = OFF_DK + hd * HEAD_DIM + np.arange(HEAD_DIM)
            c16[o + HEAD_DIM:o + LANE] = OFF_DV + hd * HEAD_DIM + np.arange(HEAD_DIM)
    cf = np.full((WF,), zero, np.int64)
    for dst, src in ((PF_HQ, OFF_HQ), (PF_HF, OFF_HF), (PF_HI, OFF_HI), (PF_HG, OFF_HG)):
        cf[dst:dst + HG_W] = src + np.arange(HG_W)
    cf[PF_GATE:PF_GATE + 3 * NSA_HEADS] = OFF_GATE + np.arange(3 * NSA_HEADS)
    cc = np.concatenate([OFF_NKC + np.arange(NSA_KV_W), OFF_NVC + np.arange(NSA_KV_W)])
    return c16, cf, cc


def _outproj_rows():
    zero = NSA_Q_W + DIL_W + HG_W
    rn = np.full((NSA_HEADS * LANE,), zero, np.int64)
    for hd in range(NSA_HEADS):
        rn[hd * LANE + HEAD_DIM:(hd + 1) * LANE] = hd * HEAD_DIM + np.arange(HEAD_DIM)
    rd = np.full((len(DIL_PAIRS), MIX_DIL_W), zero, np.int64)
    for g in range(len(DIL_PAIRS)):
        for i in range(DIL_HPG):
            rd[g, i * LANE + HEAD_DIM:(i + 1) * LANE] = NSA_Q_W + (g * DIL_HPG + i) * HEAD_DIM + np.arange(HEAD_DIM)
    return rn, rd


def _compress_weights(w1):
    half = (CMP_BLOCK // 2) * HEAD_DIM
    parts = w1.reshape(2, 2, CMP_STRIDE, HEAD_DIM, CMP_HIDDEN)
    out = []
    for h in range(NSA_KV_HEADS):
        pad = ((0, 0), (0, 0), (0, 0), (h * HEAD_DIM, (NSA_KV_HEADS - 1 - h) * HEAD_DIM), (0, 0))
        out.append(jnp.pad(parts, pad).reshape(2, 2, 2 * half, CMP_HIDDEN))
    return jnp.stack(out, axis=1).astype(BF16)


def kernel(x, p, w_in, w_out, cmp_pos, cmp_w1, cmp_w2, hg_lb, g_pre_mix, g_post_mix,
           g_pre_mlp, g_post_mlp, w_up, w_down, g_ple, w_ple_gate, w_ple_proj):
    B, S, D = x.shape
    T = B * S
    tm = 512 if S % 512 == 0 else 256
    tm_mlp = 512
    tq_nsa, tk_nsa = 256, 256
    hg_chunk = 128
    assert D == D_MODEL and S % tm == 0 and S // SLC_BLOCK <= LANE
    nsa_slopes, dil_slopes = _alibi_slopes()
    c16, cf, cc = _inproj_columns()
    rn, rd = _outproj_rows()
    n_cmp = (S - CMP_BLOCK) // CMP_STRIDE + 1
    nr = S // CMP_STRIDE
    ncp = -(-nr // LANE) * LANE

    h = x.reshape(T, D)
    for l in range(DEPTH):
        w_ext = jnp.concatenate([w_in[l], jnp.zeros((D, 1), w_in.dtype)], axis=1)
        w16 = w_ext[:, c16].astype(BF16)
        wd_in = jnp.stack([w16[:, WA:WA + DIL_GROUP_W], w16[:, WA + DIL_GROUP_W:]])
        pa, pf, pd1, pd2, xk, xv = _inproj(h, g_pre_mix[l][None, :], w16[:, :WA], w_ext[:, cf].astype(BF16),
                                           wd_in, w_ext[:, cc].astype(BF16), tm)
        pa3 = pa.reshape(B, S, WA)
        pf3 = pf.reshape(B, S, WF)

        pos8 = jnp.pad(cmp_pos[l].reshape(2, 1, CMP_BLOCK * HEAD_DIM), ((0, 0), (0, 7), (0, 0))).astype(BF16)
        w2p = jnp.stack([jnp.pad(cmp_w2[l, 0], ((0, 0), (0, HEAD_DIM))),
                         jnp.pad(cmp_w2[l, 1], ((0, 0), (HEAD_DIM, 0)))]).astype(BF16)
        cmp_kv = _compress(xk.reshape(B, nr, CMP_ROW_W), xv.reshape(B, nr, CMP_ROW_W), pos8,
                           cmp_w1[l].astype(BF16), _compress_weights(cmp_w1[l]), w2p, ncp, n_cmp)

        o_nsa = _nsa(pa3, pf3, cmp_kv, S, tq_nsa, tk_nsa, nsa_slopes).reshape(T, NSA_HEADS * LANE)
        dil = [_dilated(pa, P16_DIL, B, S, tm, 0, dil_slopes, 128),
               _dilated(pd1, 0, B, S, tm, 1, dil_slopes, 128),
               _dilated(pd2, 0, B, S, tm, 2, dil_slopes, 128)]
        o_hg = _hgrn2(pf3, hg_lb, l, S, hg_chunk).reshape(T, HG_W)

        wo_ext = jnp.concatenate([w_out[l], jnp.zeros((1, D), w_out.dtype)], axis=0)
        wn = wo_ext[rn].astype(BF16)
        wd = wo_ext[rd].astype(BF16)
        wh = w_out[l, NSA_Q_W + DIL_W:].astype(BF16)
        h = _outproj(o_nsa, [d[0] for d in dil], [d[1] for d in dil], o_hg, h, wn, wd, wh,
                     g_post_mix[l][None, :], tm)
        h = _mlp(h, p[l].reshape(T, PLE_DIM), g_pre_mlp[l][None, :], w_up[l].astype(BF16),
                 w_down[l].astype(BF16), g_post_mlp[l][None, :], g_ple[l][None, :],
                 w_ple_gate[l].astype(BF16), w_ple_proj[l].astype(BF16), tm_mlp)
    return h.reshape(B, S, D)
```

```python
import functools

import numpy as np
import jax
import jax.numpy as jnp
from jax import lax
from jax.experimental import pallas as pl
from jax.experimental.pallas import tpu as pltpu

F32 = jnp.float32
BF16 = jnp.bfloat16

D_MODEL = 1024
DEPTH = 2
HEAD_DIM = 64
NSA_HEADS = 6
NSA_KV_HEADS = 2
NSA_GROUP = NSA_HEADS // NSA_KV_HEADS
CMP_BLOCK = 32
CMP_STRIDE = 16
CMP_HIDDEN = 256
SLC_BLOCK = 64
SLC_TOPN = 8
WIN = 512
FORCE_SCORE = 1e9
DIL_PAIRS = ((128, 1), (512, 4), (2048, 16))
DIL_HPG = 2
DIL_HEADS = DIL_HPG * len(DIL_PAIRS)
HG_HEADS = 4
HG_DIM = 64
LB_TINY = 1e-30
D_FF = 4 * D_MODEL
PLE_DIM = 256
EPS = 1e-6
NEG = -1e30
LOG2E = 1.4426950408889634

NSA_Q_W = NSA_HEADS * HEAD_DIM
NSA_KV_W = NSA_KV_HEADS * HEAD_DIM
DIL_W = DIL_HEADS * HEAD_DIM
HG_W = HG_HEADS * HG_DIM
IN_WIDTHS = (NSA_Q_W,) + (NSA_KV_W,) * 6 + (3 * NSA_HEADS,) + (DIL_W,) * 3 + (HG_W,) * 4
IN_TOTAL = sum(IN_WIDTHS)
IN_OFF = tuple(int(v) for v in np.cumsum((0,) + IN_WIDTHS))
(OFF_NQ, OFF_NKC, OFF_NVC, OFF_NKS, OFF_NVS, OFF_NKW, OFF_NVW, OFF_GATE,
 OFF_DQ, OFF_DK, OFF_DV, OFF_HQ, OFF_HF, OFF_HI, OFF_HG) = IN_OFF[:-1]

LANE = 128
VMEM_LIMIT = 56 * 1024 * 1024

P16_NQ = 0
P16_KVS = P16_NQ + NSA_HEADS * LANE
P16_KVW = P16_KVS + NSA_KV_HEADS * LANE
P16_DIL = P16_KVW + NSA_KV_HEADS * LANE
DIL_GROUP_W = 2 * DIL_HPG * LANE
WA = P16_DIL + DIL_GROUP_W
W16 = P16_DIL + len(DIL_PAIRS) * DIL_GROUP_W
PF_HQ, PF_HF, PF_HI, PF_HG, PF_GATE = 0, HG_W, 2 * HG_W, 3 * HG_W, 4 * HG_W
WF = PF_GATE + LANE
WC = 2 * NSA_KV_W
MIX_DIL_W = DIL_HPG * LANE
CMP_ROW_W = CMP_STRIDE * NSA_KV_W


def _dot(a, b):
    return jnp.dot(a, b, preferred_element_type=F32)


def _dot_nt(a, b):
    return lax.dot_general(a, b, (((1,), (1,)), ((), ())), preferred_element_type=F32)


def _rms(x, g):
    return x * lax.rsqrt(jnp.mean(x * x, axis=-1, keepdims=True) + EPS) * g


def _log2(n):
    l = int(n).bit_length() - 1
    assert (1 << l) == n, n
    return l


def _alibi_slopes():
    n = NSA_HEADS + DIL_HEADS
    s = 2.0 ** (-8.0 * np.arange(1, n + 1) / n)
    quads = s.reshape(-1, 4)
    nsa = [float(np.float32(v)) for v in quads[:, 2:].reshape(-1)]
    dil = [float(np.float32(v)) for v in quads[:, :2].reshape(-1)]
    return nsa, dil


def _inproj_body(x_ref, g_ref, wa_ref, wf_ref, wd_ref, wc_ref,
                 oa_ref, of_ref, od1_ref, od2_ref, xk_ref, xv_ref, hn_scr, *, tm):
    hn32 = _rms(x_ref[...], g_ref[...])
    hn = hn32.astype(BF16)
    for c in range(0, WA, 256):
        oa_ref[:, c:c + 256] = _dot(hn, wa_ref[:, c:c + 256]).astype(BF16)
    for c in range(0, WF, 384):
        of_ref[:, c:c + 384] = _dot(hn, wf_ref[:, c:c + 384])
    nlb = D_MODEL // LANE
    for c in range(nlb):
        hn_scr[c] = hn32[:, c * LANE:(c + 1) * LANE]

    def by_residue(d):
        parts = [jnp.concatenate([hn_scr[c, pl.ds(r, tm // d, stride=d), :] for c in range(nlb)], axis=1)
                 for r in range(d)]
        return jnp.concatenate(parts, axis=0).astype(BF16)

    for gi, o_ref in ((1, od1_ref), (2, od2_ref)):
        hp = by_residue(DIL_PAIRS[gi][1])
        for c in range(0, DIL_GROUP_W, 256):
            o_ref[:, c:c + 256] = _dot(hp, wd_ref[gi - 1, :, c:c + 256]).astype(BF16)
    assert DIL_PAIRS[2][1] == CMP_STRIDE
    cc = _dot(hp, wc_ref[...])
    nr = tm // CMP_STRIDE
    for j in range(CMP_STRIDE):
        xk_ref[:, j * NSA_KV_W:(j + 1) * NSA_KV_W] = cc[j * nr:(j + 1) * nr, 0:NSA_KV_W]
        xv_ref[:, j * NSA_KV_W:(j + 1) * NSA_KV_W] = cc[j * nr:(j + 1) * nr, NSA_KV_W:2 * NSA_KV_W]


def _inproj(x2, g, wa, wf, wd, wc, tm):
    T = x2.shape[0]
    row = lambda i: (i, 0)
    full = lambda i: (0, 0)
    nr = tm // CMP_STRIDE
    return pl.pallas_call(
        functools.partial(_inproj_body, tm=tm),
        grid=(T // tm,),
        in_specs=[pl.BlockSpec((tm, D_MODEL), row),
                  pl.BlockSpec((1, D_MODEL), full),
                  pl.BlockSpec((D_MODEL, WA), full),
                  pl.BlockSpec((D_MODEL, WF), full),
                  pl.BlockSpec((2, D_MODEL, DIL_GROUP_W), lambda i: (0, 0, 0)),
                  pl.BlockSpec((D_MODEL, WC), full)],
        out_specs=[pl.BlockSpec((tm, WA), row), pl.BlockSpec((tm, WF), row),
                   pl.BlockSpec((tm, DIL_GROUP_W), row), pl.BlockSpec((tm, DIL_GROUP_W), row),
                   pl.BlockSpec((nr, CMP_ROW_W), row), pl.BlockSpec((nr, CMP_ROW_W), row)],
        out_shape=[jax.ShapeDtypeStruct((T, WA), BF16), jax.ShapeDtypeStruct((T, WF), F32),
                   jax.ShapeDtypeStruct((T, DIL_GROUP_W), BF16), jax.ShapeDtypeStruct((T, DIL_GROUP_W), BF16),
                   jax.ShapeDtypeStruct((T // CMP_STRIDE, CMP_ROW_W), F32),
                   jax.ShapeDtypeStruct((T // CMP_STRIDE, CMP_ROW_W), F32)],
        scratch_shapes=[pltpu.VMEM((D_MODEL // LANE, tm, LANE), F32)],
        compiler_params=pltpu.CompilerParams(dimension_semantics=("arbitrary",),
                                             vmem_limit_bytes=VMEM_LIMIT),
        name="inproj",
    )(x2, g, wa, wf, wd, wc)


def _compress_body(xk_ref, xv_ref, pos_ref, w1_ref, w1x_ref, w2_ref, out_ref, *, n_cmp):
    nr = xk_ref.shape[1]
    rows = lax.broadcasted_iota(jnp.int32, (nr, 1), 0)
    out_ref[...] = jnp.zeros(out_ref.shape, out_ref.dtype)
    xs = (xk_ref[0].astype(BF16), xv_ref[0].astype(BF16))
    for h in range(NSA_KV_HEADS):
        acc = jnp.zeros((nr, LANE), F32)
        for ten in range(2):
            first = _dot(xs[ten], w1x_ref[ten, h, 0])
            second = _dot(xs[ten], w1x_ref[ten, h, 1])
            posb = _dot(pos_ref[ten], w1_ref[ten])[0:1, :]
            hid = first + pltpu.roll(second, nr - 1, 0) + posb
            act = hid * jax.nn.sigmoid(hid)
            acc = acc + _dot(act.astype(BF16), w2_ref[ten])
        acc = jnp.where(rows < n_cmp, acc, 0.0)
        out_ref[0, 0:nr, h * LANE:(h + 1) * LANE] = acc.astype(BF16)


def _compress(xk, xv, pos8, w1, w1x, w2p, ncp, n_cmp):
    B, nr, kw = xk.shape
    seq = lambda b: (b, 0, 0)
    full3 = lambda b: (0, 0, 0)
    return pl.pallas_call(
        functools.partial(_compress_body, n_cmp=n_cmp),
        grid=(B,),
        in_specs=[pl.BlockSpec((1, nr, kw), seq)] * 2
        + [pl.BlockSpec(pos8.shape, full3), pl.BlockSpec(w1.shape, full3),
           pl.BlockSpec(w1x.shape, lambda b: (0, 0, 0, 0, 0)), pl.BlockSpec(w2p.shape, full3)],
        out_specs=pl.BlockSpec((1, ncp, NSA_KV_HEADS * LANE), seq),
        out_shape=jax.ShapeDtypeStruct((B, ncp, NSA_KV_HEADS * LANE), BF16),
        compiler_params=pltpu.CompilerParams(dimension_semantics=("arbitrary",),
                                             vmem_limit_bytes=VMEM_LIMIT),
        name="nsa_compress",
    )(xk, xv, pos8, w1, w1x, w2p)


NSA_PAT_BLK = HEAD_DIM
NSA_PAT_POS = HEAD_DIM + 32


def _nsa_pattern(seq):
    pos = np.arange(seq)
    pat = np.zeros((seq, LANE), np.float32)
    pat[pos, NSA_PAT_BLK + pos // SLC_BLOCK] = 1.0
    pat[:, NSA_PAT_POS:NSA_PAT_POS + 3] = (SLC_BLOCK * (pos // SLC_BLOCK))[:, None]
    pat[:, NSA_PAT_POS + 3:NSA_PAT_POS + 6] = (pos % SLC_BLOCK)[:, None]
    return jnp.asarray(pat, BF16)


def _nsa_slope_rows(nsa_slopes):
    rows = np.zeros((NSA_KV_HEADS, 8, LANE), np.float32)
    for kh in range(NSA_KV_HEADS):
        for g in range(NSA_GROUP):
            rest = np.float32(nsa_slopes[kh * NSA_GROUP + g])
            for part in range(3):
                piece = np.float32(np.asarray(rest, dtype=BF16))
                rows[kh, g, NSA_PAT_POS + part] = piece
                rows[kh, g, NSA_PAT_POS + 3 + part] = piece
                rest = np.float32(rest - piece)
    return jnp.asarray(rows)


def _nsa_body(q_ref, kvs_ref, kvw_ref, kvc_ref, gate_ref, pat_ref, sl_ref, out_ref,
              ks_scr, vs_scr, kw_scr, vw_scr, s_scr, *, seq, tq, tk, n_cmp, n_slc, k_top, slopes):
    qi = pl.program_id(1)
    G = NSA_GROUP
    KH = NSA_KV_HEADS
    t0 = qi * tq
    ncp = kvc_ref.shape[1]
    ltk = _log2(tk)
    lane = lax.broadcasted_iota(jnp.int32, (1, LANE), 1)
    lo_half = lane < HEAD_DIM
    hi_half = lane >= HEAD_DIM

    @pl.when(qi == 0)
    def _():
        pat = pat_ref[...]
        one = jnp.ones((seq, LANE), BF16)
        for kh in range(KH):
            cols = slice(kh * LANE, (kh + 1) * LANE)
            kvs = kvs_ref[0, :, cols]
            kvw = kvw_ref[0, :, cols]
            ks_scr[kh] = jnp.where(lo_half, kvs, pat)
            kw_scr[kh] = jnp.where(lo_half, kvw, pat)
            vs_scr[kh] = jnp.where(lo_half, one, kvs)
            vw_scr[kh] = jnp.where(lo_half, one, kvw)

    ii = lax.broadcasted_iota(jnp.int32, (tq, 1), 0)
    row_t = t0 + ii
    nn = lax.broadcasted_iota(jnp.int32, (1, ncp), 1)
    maskc = ((nn * CMP_STRIDE + (CMP_BLOCK - 1)) <= row_t) & (nn < n_cmp)
    absd = jnp.abs(row_t.astype(F32) - (nn.astype(F32) * CMP_STRIDE + 0.5 * (CMP_BLOCK - 1)))
    mi = lax.broadcasted_iota(jnp.int32, (LANE, 1), 0)
    ov_t = ((nn * CMP_STRIDE < mi * SLC_BLOCK + SLC_BLOCK) & (nn * CMP_STRIDE + CMP_BLOCK > mi * SLC_BLOCK)
            & (nn < n_cmp) & (mi < n_slc))
    ov_t = jnp.where(ov_t, 1.0, 0.0).astype(BF16)
    nsp = -(-n_slc // 8) * 8
    mi_s = mi[0:nsp]
    cur_l = (t0 + lax.broadcasted_iota(jnp.int32, (1, tq), 1)) >> _log2(SLC_BLOCK)
    in_rng = mi_s < n_slc
    valid = (mi_s <= cur_l) & in_rng
    forced = (mi_s == 0) | (mi_s == cur_l) | (mi_s == cur_l - 1)

    def select(kh):
        qs = [q_ref[0, :, (kh * G + g) * LANE:(kh * G + g + 1) * LANE] * (HEAD_DIM ** -0.5) for g in range(G)]
        kvc = kvc_ref[0, :, kh * LANE:(kh + 1) * LANE]
        s_c = _dot_nt(jnp.concatenate(qs, axis=0), kvc)
        ps = []
        for g in range(G):
            s = jnp.where(maskc, s_c[g * tq:(g + 1) * tq] - slopes[kh * G + g] * absd, NEG)
            m = jnp.max(s, axis=-1, keepdims=True)
            e = jnp.where(maskc, jnp.exp(s - m), 0.0)
            ps.append(e / jnp.maximum(jnp.sum(e, axis=-1, keepdims=True), 1e-30))
        o_cmp = _dot(jnp.concatenate(ps, axis=0).astype(BF16), kvc)
        psum = ps[0] + ps[1] + ps[2]
        p_hi = psum.astype(BF16)
        p_lo = (psum - p_hi.astype(F32)).astype(BF16)
        imp_t = _dot_nt(ov_t, p_hi) + _dot_nt(ov_t, p_lo)
        score = jnp.where(valid, jnp.where(forced, FORCE_SCORE, imp_t[0:nsp]), -FORCE_SCORE)
        score = jnp.where(in_rng, score, -3.0 * FORCE_SCORE)
        rank = jnp.zeros((nsp, tq), F32)
        for mp in range(n_slc):
            row = score[mp:mp + 1, :]
            beats = (row > score) | ((row == score) & (mp < mi_s))
            rank = rank + jnp.where(beats, 1.0, 0.0)
        neg_t = jnp.where((rank < k_top) & valid, 0.0, NEG)
        neg_t = jnp.concatenate([jnp.zeros((NSA_PAT_BLK, tq), F32), neg_t,
                                 jnp.zeros((LANE - NSA_PAT_BLK - nsp, tq), F32)], axis=0)
        return qs, o_cmp, neg_t.T

    def extended(kh, qs, neg):
        out = []
        for g in range(G):
            extra = sl_ref[kh, g:g + 1, :] + (0.0 if neg is None else neg)
            out.append(jnp.where(lo_half, qs[g], extra.astype(BF16)))
        return jnp.concatenate(out, axis=0)

    def normalize(acc):
        den = pltpu.roll(acc, HEAD_DIM, 1)
        return jnp.where(hi_half, acc / jnp.where(hi_half, den, 1.0), 0.0)

    n_kt = (t0 + tq + tk - 1) >> ltk
    nwt = min(-(-(WIN + tq) // tk), seq // tk)
    wk = nwt * tk
    kt0 = jnp.clip(n_kt - nwt, 0, seq // tk - nwt)
    k0 = pl.multiple_of(kt0 * tk, tk)
    span = pl.ds(k0, wk)
    rel = row_t - (k0 + lax.broadcasted_iota(jnp.int32, (1, wk), 1))
    ok_causal = jnp.where(rel >= 0, 0.0, NEG)
    ok_win = jnp.where(rel < WIN, ok_causal, NEG)

    def masked(s, bias):
        return (s.reshape(G, tq, wk) + bias[None]).reshape(G * tq, wk)

    o_cmp, o_win, q_sel, s_near, m_near = [], [], [], [], []
    picked = [select(kh) for kh in range(KH)]
    for kh in range(KH):
        qs, oc, neg = picked[kh]
        o_cmp.append(oc)
        q_sel.append(extended(kh, qs, neg))
        s_near.append(masked(_dot_nt(q_sel[kh], ks_scr[kh, span, :]), ok_causal))
        m_near.append(jnp.max(s_near[kh], axis=-1, keepdims=True))
    for kh in range(KH):
        s = masked(_dot_nt(extended(kh, picked[kh][0], None), kw_scr[kh, span, :]), ok_win)
        p = jnp.exp2(s - jnp.max(s, axis=-1, keepdims=True))
        o_win.append(normalize(_dot(p.astype(BF16), vw_scr[kh, span, :])))

    def far_scores(kt, m_acc):
        out = []
        for kh in range(KH):
            s = _dot_nt(q_sel[kh], ks_scr[kh, pl.ds(pl.multiple_of(kt * tk, tk), tk), :])
            s_scr[kh, kt] = s
            m = m_acc[kh]
            for c in range(0, tk, LANE):
                m = jnp.maximum(m, s[:, c:c + LANE])
            out.append(m)
        return tuple(out)

    m_far = lax.fori_loop(0, kt0, far_scores, tuple(jnp.full((G * tq, LANE), NEG, F32) for _ in range(KH)))
    m_row = [jnp.maximum(jnp.max(m_far[kh], axis=-1, keepdims=True), m_near[kh]) for kh in range(KH)]
    acc0 = tuple(_dot(jnp.exp2(s_near[kh] - m_row[kh]).astype(BF16), vs_scr[kh, span, :]) for kh in range(KH))

    def far_accum(kt, acc):
        out = []
        for kh in range(KH):
            p = jnp.exp2(s_scr[kh, kt] - m_row[kh])
            out.append(acc[kh] + _dot(p.astype(BF16), vs_scr[kh, pl.ds(pl.multiple_of(kt * tk, tk), tk), :]))
        return tuple(out)

    acc = lax.fori_loop(0, kt0, far_accum, acc0)

    gates = jax.nn.sigmoid(gate_ref[0])
    for kh in range(KH):
        o_sel = normalize(acc[kh])
        for g in range(G):
            hd = kh * G + g
            rows = slice(g * tq, (g + 1) * tq)
            o = (gates[:, hd * 3:hd * 3 + 1] * o_cmp[kh][rows] + gates[:, hd * 3 + 1:hd * 3 + 2] * o_sel[rows]
                 + gates[:, hd * 3 + 2:hd * 3 + 3] * o_win[kh][rows])
            out_ref[0, :, hd * LANE:(hd + 1) * LANE] = o.astype(BF16)


def _nsa(pa, pf, cmp_kv, seq, tq, tk, nsa_slopes):
    B = pa.shape[0]
    n_cmp = (seq - CMP_BLOCK) // CMP_STRIDE + 1
    n_slc = seq // SLC_BLOCK
    ncp = cmp_kv.shape[1]
    qw = NSA_HEADS * LANE
    kvw = NSA_KV_HEADS * LANE
    pat = _nsa_pattern(seq)
    sl_rows = _nsa_slope_rows([s * LOG2E for s in nsa_slopes])
    body = functools.partial(_nsa_body, seq=seq, tq=tq, tk=tk, n_cmp=n_cmp, n_slc=n_slc,
                             k_top=min(SLC_TOPN, n_slc), slopes=tuple(nsa_slopes))
    return pl.pallas_call(
        body,
        grid=(B, seq // tq),
        in_specs=[pl.BlockSpec((1, tq, qw), lambda b, i: (b, i, 0)),
                  pl.BlockSpec((1, seq, kvw), lambda b, i: (b, 0, P16_KVS // kvw)),
                  pl.BlockSpec((1, seq, kvw), lambda b, i: (b, 0, P16_KVW // kvw)),
                  pl.BlockSpec((1, ncp, kvw), lambda b, i: (b, 0, 0)),
                  pl.BlockSpec((1, tq, LANE), lambda b, i: (b, i, PF_GATE // LANE)),
                  pl.BlockSpec((seq, LANE), lambda b, i: (0, 0)),
                  pl.BlockSpec((NSA_KV_HEADS, 8, LANE), lambda b, i: (0, 0, 0))],
        out_specs=pl.BlockSpec((1, tq, qw), lambda b, i: (b, i, 0)),
        out_shape=jax.ShapeDtypeStruct((B, seq, qw), BF16),
        scratch_shapes=[pltpu.VMEM((NSA_KV_HEADS, seq, LANE), BF16)] * 4
        + [pltpu.VMEM((NSA_KV_HEADS, seq // tk, NSA_GROUP * tq, tk), F32)],
        compiler_params=pltpu.CompilerParams(dimension_semantics=("arbitrary",) * 2,
                                             vmem_limit_bytes=VMEM_LIMIT),
        name="nsa_attention",
    )(pa, pa, pa, cmp_kv, pf, pat, sl_rows)


def _dilated_body(q_ref, kv_ref, o_ref, lse_ref, *, nres, blk, ls, dil, win_keys, slopes, tq):
    ii = lax.broadcasted_iota(jnp.int32, (tq, 1), 0)
    jj = lax.broadcasted_iota(jnp.int32, (1, 2 * tq), 1)
    rel2 = ii - jj + tq
    ok2 = (rel2 >= 0) & (rel2 <= win_keys)

    def rows(ref, r, t, lanes):
        if blk >= tq:
            a, b = divmod(t * tq, blk)
            return ref[0, a, r, b:b + tq, lanes]
        n = tq // blk
        return ref[0, t * n:(t + 1) * n, r, :, lanes].reshape(tq, LANE)

    def put(ref, r, t, lanes, val):
        if blk >= tq:
            a, b = divmod(t * tq, blk)
            ref[0, a, r, b:b + tq, lanes] = val
        else:
            n = tq // blk
            ref[0, t * n:(t + 1) * n, r, :, lanes] = val.reshape(n, blk, LANE)

    for r in range(nres):
        for h in range(DIL_HPG):
            lanes = slice(h * LANE, (h + 1) * LANE)
            bias2 = (slopes[h] * float(dil)) * rel2.astype(F32)
            for t in range(ls // tq):
                qt = rows(q_ref, r, t, lanes) * (HEAD_DIM ** -0.5)
                if t == 0:
                    kv = jnp.concatenate([rows(kv_ref, r, 0, lanes)] * 2, axis=0)
                    s = jnp.where(ok2 & (jj >= tq), _dot_nt(qt, kv) - bias2, NEG)
                else:
                    kv = jnp.concatenate([rows(kv_ref, r, t - 1, lanes), rows(kv_ref, r, t, lanes)], axis=0)
                    s = jnp.where(ok2, _dot_nt(qt, kv) - bias2, NEG)
                m = jnp.max(s, axis=-1, keepdims=True)
                e = jnp.exp(s - m)
                l = jnp.sum(e, axis=-1, keepdims=True)
                o = _dot(e.astype(BF16), kv) / l
                put(o_ref, r, t, lanes, o.astype(BF16))
                put(lse_ref, r, t, lanes, jnp.broadcast_to(m + jnp.log(l), (tq, LANE)))


def _dilated(src, col0, batch, seq, tm, group, slopes, tq):
    win, dil = DIL_PAIRS[group]
    ls = seq // dil
    tq = min(tq, ls)
    if dil == 1:
        ntile, blk = 1, seq
    else:
        ntile, blk = seq // tm, tm // dil
    nres = max(1, min(dil, 16 * tq // ls))
    width = src.shape[1]
    view = src.reshape(batch, ntile, dil, blk, width)
    qc = col0 // MIX_DIL_W
    body = functools.partial(_dilated_body, nres=nres, blk=blk, ls=ls, dil=dil, win_keys=win // dil,
                             slopes=tuple(slopes[group * DIL_HPG:(group + 1) * DIL_HPG]), tq=tq)
    shp = (1, ntile, nres, blk, MIX_DIL_W)
    o, lse = pl.pallas_call(
        body,
        grid=(batch, dil // nres),
        in_specs=[pl.BlockSpec(shp, lambda b, r: (b, 0, r, 0, qc)),
                  pl.BlockSpec(shp, lambda b, r: (b, 0, r, 0, qc + 1))],
        out_specs=[pl.BlockSpec(shp, lambda b, r: (b, 0, r, 0, 0))] * 2,
        out_shape=[jax.ShapeDtypeStruct((batch, ntile, dil, blk, MIX_DIL_W), BF16),
                   jax.ShapeDtypeStruct((batch, ntile, dil, blk, MIX_DIL_W), F32)],
        compiler_params=pltpu.CompilerParams(dimension_semantics=("arbitrary",) * 2,
                                             vmem_limit_bytes=VMEM_LIMIT),
        name=f"dilated_attention_g{group}",
    )(view, view)
    return o.reshape(batch * seq, MIX_DIL_W), lse.reshape(batch * seq, MIX_DIL_W)


def _hgrn2_body(q_ref, f_ref, i_ref, g_ref, lb_ref, o_ref, *, layer, seq, chunk):
    c = chunk
    sub = 8
    npair = HG_W // LANE
    lbs = lb_ref[...].astype(F32)
    mx = jnp.max(lbs, axis=0, keepdims=True)
    ex = jnp.exp(lbs - mx)
    sm = ex / jnp.sum(ex, axis=0, keepdims=True)
    lower = jnp.maximum(jnp.sum(sm[0:layer + 1], axis=0, keepdims=True) - sm[0:1], 0.0)
    log_lb = jnp.log(lower + LB_TINY)
    log_1m = jnp.log1p(-lower)

    lane = lax.broadcasted_iota(jnp.int32, (1, LANE), 1)
    head0 = lane < HG_DIM
    ri = lax.broadcasted_iota(jnp.int32, (c, 1), 0)
    ci = lax.broadcasted_iota(jnp.int32, (1, c), 1)
    tri = jnp.where(ci <= ri, 1.0, 0.0).astype(BF16)
    di = lax.broadcasted_iota(jnp.int32, (LANE, 1), 0)
    same_head = (di >= HG_DIM) == (lane >= HG_DIM)
    ones_blk = jnp.where(same_head, 1.0, 0.0).astype(BF16)
    gcol = lax.broadcasted_iota(jnp.int32, (1, sub * c), 1)
    gsum = jnp.where(((gcol >> _log2(c)) == (ri & (sub - 1)))
                     & (((gcol & (c - 1)) >> 3) == (ri >> 3)), 1.0, 0.0).astype(BF16)
    sp = lax.broadcasted_iota(jnp.int32, (1, sub, 1), 1)
    levels = []
    w = sub
    while w < c:
        same = (ri >> _log2(2 * w)) == (ci >> _log2(2 * w))
        levels.append((w, same & ((ri & (2 * w - 1)) >= w) & ((ci & (2 * w - 1)) < w)))
        w *= 2

    def split3(x):
        hi = x.astype(BF16)
        r1 = x - hi.astype(F32)
        mid = r1.astype(BF16)
        lo = (r1 - mid.astype(F32)).astype(BF16)
        return hi, mid, lo

    def pair_chunk(q, v, x, gt, llb, l1m, state_t):
        log_sig = jnp.minimum(x, 0.0) - jnp.log1p(jnp.exp(-jnp.abs(x)))
        t2 = l1m + log_sig
        lf = jnp.maximum(llb, t2) + jnp.log1p(jnp.exp(-jnp.abs(llb - t2)))
        kk = 1.0 - jnp.exp(lf)
        hi, mid, lo = split3(lf)
        b = _dot(tri, hi) + _dot(tri, mid) + _dot(tri, lo)
        vb = v.astype(BF16)

        q3 = q.reshape(c // sub, sub, LANE)
        k3 = kk.reshape(c // sub, sub, LANE)
        b3 = b.reshape(c // sub, sub, LANE)
        parts = []
        for tp in range(sub):
            dec = jnp.exp(jnp.minimum(b3[:, tp:tp + 1, :] - b3, 0.0))
            parts.append(jnp.where(sp <= tp, q3[:, tp:tp + 1, :] * k3 * dec, 0.0).reshape(c, LANE))
        wall = jnp.concatenate(parts, axis=0)
        a_rep = _dot(wall.astype(BF16), ones_blk)
        z = a_rep * jnp.concatenate([v] * sub, axis=0)
        o = _dot(gsum, z.astype(BF16))

        a0 = jnp.zeros((c, c), F32)
        a1 = jnp.zeros((c, c), F32)
        for w, lmask in levels:
            b_r = b.reshape(c // (2 * w), 2 * w, LANE)
            bnd = jnp.broadcast_to(b_r[:, w - 1:w, :], b_r.shape).reshape(c, LANE)
            qe = q * jnp.exp(jnp.minimum(b - bnd, 0.0))
            ke = (kk * jnp.exp(jnp.minimum(bnd - b, 0.0))).astype(BF16)
            a0 = a0 + jnp.where(lmask, _dot_nt(jnp.where(head0, qe, 0.0).astype(BF16), ke), 0.0)
            a1 = a1 + jnp.where(lmask, _dot_nt(jnp.where(head0, 0.0, qe).astype(BF16), ke), 0.0)
        o = o + jnp.where(head0, _dot(a0.astype(BF16), vb), _dot(a1.astype(BF16), vb))

        o = o + _dot_nt((q * jnp.exp(b)).astype(BF16), state_t.astype(BF16))
        b_last = b[c - 1:c, :]
        khat = (kk * jnp.exp(b_last - b)).astype(BF16)
        upd = lax.dot_general(vb, khat, (((0,), (0,)), ((), ())), preferred_element_type=F32)
        state_t = jnp.exp(b_last) * state_t + jnp.where(same_head, upd, 0.0)

        o2 = o * o
        ms0 = jnp.sum(jnp.where(head0, o2, 0.0), axis=-1, keepdims=True)
        ms1 = jnp.sum(jnp.where(head0, 0.0, o2), axis=-1, keepdims=True)
        o = o * lax.rsqrt(jnp.where(head0, ms0, ms1) * (1.0 / HG_DIM) + EPS)
        return (o * (gt * jax.nn.sigmoid(gt))).astype(BF16), state_t

    def step(ic, states):
        rows = pl.ds(pl.multiple_of(ic * c, c), c)
        out = []
        for p in range(npair):
            cols = slice(p * LANE, (p + 1) * LANE)
            o, st = pair_chunk(q_ref[0, rows, cols], i_ref[0, rows, cols], f_ref[0, rows, cols],
                               g_ref[0, rows, cols], log_lb[:, cols], log_1m[:, cols], states[p])
            o_ref[0, rows, cols] = o
            out.append(st)
        return tuple(out)

    lax.fori_loop(0, seq // c, step, tuple(jnp.zeros((LANE, LANE), F32) for _ in range(npair)))


def _hgrn2(pf, hg_lb, layer, seq, chunk):
    B = pf.shape[0]
    sec = lambda off: (lambda b: (b, 0, off // HG_W))
    return pl.pallas_call(
        functools.partial(_hgrn2_body, layer=layer, seq=seq, chunk=chunk),
        grid=(B,),
        in_specs=[pl.BlockSpec((1, seq, HG_W), sec(PF_HQ)),
                  pl.BlockSpec((1, seq, HG_W), sec(PF_HF)),
                  pl.BlockSpec((1, seq, HG_W), sec(PF_HI)),
                  pl.BlockSpec((1, seq, HG_W), sec(PF_HG)),
                  pl.BlockSpec((DEPTH, HG_W), lambda b: (0, 0))],
        out_specs=pl.BlockSpec((1, seq, HG_W), lambda b: (b, 0, 0)),
        out_shape=jax.ShapeDtypeStruct((B, seq, HG_W), BF16),
        compiler_params=pltpu.CompilerParams(dimension_semantics=("arbitrary",),
                                             vmem_limit_bytes=VMEM_LIMIT),
        name="hgrn2",
    )(pf, pf, pf, pf, hg_lb)


def _outproj_body(nsa_ref, d0_ref, d1_ref, d2_ref, l0_ref, l1_ref, l2_ref, hg_ref, h_ref,
                  wn_ref, wd_ref, wh_ref, g_ref, out_ref, tok_scr, *, tm):
    def token_order(ref, slot, d):
        if d == 1:
            return ref[...].astype(F32)
        n = tm // d
        for r in range(d):
            piece = ref[r * n:(r + 1) * n, :].astype(F32)
            for c in range(DIL_HPG):
                tok_scr[DIL_HPG * slot + c, pl.ds(r, n, stride=d), :] = piece[:, c * LANE:(c + 1) * LANE]
        return jnp.concatenate([tok_scr[DIL_HPG * slot + c] for c in range(DIL_HPG)], axis=1)

    dils = [d for _, d in DIL_PAIRS]
    os_ = [token_order(ref, 2 * g, dils[g]) for g, ref in enumerate((d0_ref, d1_ref, d2_ref))]
    ls = [token_order(ref, 2 * g + 1, dils[g]) for g, ref in enumerate((l0_ref, l1_ref, l2_ref))]
    lm = jnp.maximum(jnp.maximum(ls[0], ls[1]), ls[2])
    es = [jnp.exp(l - lm) for l in ls]
    inv = 1.0 / (es[0] + es[1] + es[2])
    acc = _dot(nsa_ref[...], wn_ref[...])
    for g in range(len(DIL_PAIRS)):
        acc = acc + _dot((os_[g] * (es[g] * inv)).astype(BF16), wd_ref[g])
    acc = acc + _dot(hg_ref[...], wh_ref[...])
    out_ref[...] = h_ref[...] + _rms(acc, g_ref[...])


def _outproj(nsa, dil_o, dil_l, hg, h2, wn, wd, wh, g, tm):
    T = h2.shape[0]
    row = lambda i: (i, 0)
    full = lambda i: (0, 0)
    return pl.pallas_call(
        functools.partial(_outproj_body, tm=tm),
        grid=(T // tm,),
        in_specs=[pl.BlockSpec((tm, NSA_HEADS * LANE), row)]
        + [pl.BlockSpec((tm, MIX_DIL_W), row)] * 6
        + [pl.BlockSpec((tm, HG_W), row), pl.BlockSpec((tm, D_MODEL), row),
           pl.BlockSpec(wn.shape, full), pl.BlockSpec(wd.shape, lambda i: (0, 0, 0)),
           pl.BlockSpec(wh.shape, full), pl.BlockSpec((1, D_MODEL), full)],
        out_specs=pl.BlockSpec((tm, D_MODEL), row),
        out_shape=jax.ShapeDtypeStruct((T, D_MODEL), F32),
        scratch_shapes=[pltpu.VMEM((2 * len(DIL_PAIRS) * DIL_HPG, tm, LANE), F32)],
        compiler_params=pltpu.CompilerParams(dimension_semantics=("arbitrary",),
                                             vmem_limit_bytes=VMEM_LIMIT),
        name="outproj",
    )(nsa, *dil_o, *dil_l, hg, h2, wn, wd, wh, g)


def _mlp_body(h_ref, p_ref, gpre_ref, wup_ref, wdn_ref, gpost_ref, gple_ref, wg_ref, wp_ref, out_ref, *, fc):
    h = h_ref[...]
    hn = _rms(h, gpre_ref[...]).astype(BF16)
    acc = jnp.zeros(h.shape, F32)
    for c in range(0, D_FF, fc):
        u = jnp.maximum(_dot(hn, wup_ref[:, c:c + fc]), 0.0)
        acc = acc + _dot((u * u).astype(BF16), wdn_ref[c:c + fc, :])
    h = h + _rms(acc, gpost_ref[...])
    gate = jax.nn.sigmoid(_dot(_rms(h, gple_ref[...]).astype(BF16), wg_ref[...]))
    out_ref[...] = h + _dot(p_ref[...].astype(BF16), wp_ref[...]) * gate


def _mlp(h2, p2, gpre, wup, wdn, gpost, gple, wg, wp, tm):
    T = h2.shape[0]
    row = lambda i: (i, 0)
    full = lambda i: (0, 0)
    vec = pl.BlockSpec((1, D_MODEL), full)
    resident = lambda shape: pl.BlockSpec(shape, full, pipeline_mode=pl.Buffered(1))
    return pl.pallas_call(
        functools.partial(_mlp_body, fc=512),
        grid=(T // tm,),
        in_specs=[pl.BlockSpec((tm, D_MODEL), row), pl.BlockSpec((tm, PLE_DIM), row), vec,
                  resident(wup.shape), resident(wdn.shape), vec, vec,
                  resident(wg.shape), resident(wp.shape)],
        out_specs=pl.BlockSpec((tm, D_MODEL), row),
        out_shape=jax.ShapeDtypeStruct((T, D_MODEL), F32),
        compiler_params=pltpu.CompilerParams(dimension_semantics=("arbitrary",),
                                             vmem_limit_bytes=VMEM_LIMIT),
        name="mlp_ple",
    )(h2, p2, gpre, wup, wdn, gpost, gple, wg, wp)


def _inproj_columns():
    zero = IN_TOTAL
    c16 = np.full((W16,), zero, np.int64)
    for hd in range(NSA_HEADS):
        c16[P16_NQ + hd * LANE:P16_NQ + hd * LANE + HEAD_DIM] = OFF_NQ + hd * HEAD_DIM + np.arange(HEAD_DIM)
    for kh in range(NSA_KV_HEADS):
        for base, ko, vo in ((P16_KVS, OFF_NKS, OFF_NVS), (P16_KVW, OFF_NKW, OFF_NVW)):
            o = base + kh * LANE
            c16[o:o + HEAD_DIM] = ko + kh * HEAD_DIM + np.arange(HEAD_DIM)
            c16[o + HEAD_DIM:o + LANE] = vo + kh * HEAD_DIM + np.arange(HEAD_DIM)
    for g in range(len(DIL_PAIRS)):
        base = P16_DIL + g * DIL_GROUP_W
        for i in range(DIL_HPG):
            hd = g * DIL_HPG + i
            c16[base + i * LANE:base + i * LANE + HEAD_DIM] = OFF_DQ + hd * HEAD_DIM + np.arange(HEAD_DIM)
            o = base + DIL_HPG * LANE + i * LANE
            c16[o:o + HEAD_DIM] = OFF_DK + hd * HEAD_DIM + np.arange(HEAD_DIM)
            c16[o + HEAD_DIM:o + LANE] = OFF_DV + hd * HEAD_DIM + np.arange(HEAD_DIM)
    cf = np.full((WF,), zero, np.int64)
    for dst, src in ((PF_HQ, OFF_HQ), (PF_HF, OFF_HF), (PF_HI, OFF_HI), (PF_HG, OFF_HG)):
        cf[dst:dst + HG_W] = src + np.arange(HG_W)
    cf[PF_GATE:PF_GATE + 3 * NSA_HEADS] = OFF_GATE + np.arange(3 * NSA_HEADS)
    cc = np.concatenate([OFF_NKC + np.arange(NSA_KV_W), OFF_NVC + np.arange(NSA_KV_W)])
    return c16, cf, cc


def _outproj_rows():
    zero = NSA_Q_W + DIL_W + HG_W
    rn = np.full((NSA_HEADS * LANE,), zero, np.int64)
    for hd in range(NSA_HEADS):
        rn[hd * LANE + HEAD_DIM:(hd + 1) * LANE] = hd * HEAD_DIM + np.arange(HEAD_DIM)
    rd = np.full((len(DIL_PAIRS), MIX_DIL_W), zero, np.int64)
    for g in range(len(DIL_PAIRS)):
        for i in range(DIL_HPG):
            rd[g, i * LANE + HEAD_DIM:(i + 1) * LANE] = NSA_Q_W + (g * DIL_HPG + i) * HEAD_DIM + np.arange(HEAD_DIM)
    return rn, rd


def _compress_weights(w1):
    half = (CMP_BLOCK // 2) * HEAD_DIM
    parts = w1.reshape(2, 2, CMP_STRIDE, HEAD_DIM, CMP_HIDDEN)
    out = []
    for h in range(NSA_KV_HEADS):
        pad = ((0, 0), (0, 0), (0, 0), (h * HEAD_DIM, (NSA_KV_HEADS - 1 - h) * HEAD_DIM), (0, 0))
        out.append(jnp.pad(parts, pad).reshape(2, 2, 2 * half, CMP_HIDDEN))
    return jnp.stack(out, axis=1).astype(BF16)


def kernel(x, p, w_in, w_out, cmp_pos, cmp_w1, cmp_w2, hg_lb, g_pre_mix, g_post_mix,
           g_pre_mlp, g_post_mlp, w_up, w_down, g_ple, w_ple_gate, w_ple_proj):
    B, S, D = x.shape
    T = B * S
    tm = 512 if S % 512 == 0 else 256
    tm_mlp = 512
    tq_nsa, tk_nsa = 256, 256
    hg_chunk = 128
    assert D == D_MODEL and S % tm == 0 and S // SLC_BLOCK <= LANE
    nsa_slopes, dil_slopes = _alibi_slopes()
    c16, cf, cc = _inproj_columns()
    rn, rd = _outproj_rows()
    n_cmp = (S - CMP_BLOCK) // CMP_STRIDE + 1
    nr = S // CMP_STRIDE
    ncp = -(-nr // LANE) * LANE

    key_scale = np.ones((1, IN_TOTAL + 1), np.float32)
    key_scale[0, OFF_NKS:OFF_NKS + NSA_KV_W] = LOG2E
    key_scale[0, OFF_NKW:OFF_NKW + NSA_KV_W] = LOG2E

    h = x.reshape(T, D)
    for l in range(DEPTH):
        w_ext = jnp.concatenate([w_in[l], jnp.zeros((D, 1), w_in.dtype)], axis=1) * key_scale
        w16 = w_ext[:, c16].astype(BF16)
        wd_in = jnp.stack([w16[:, WA:WA + DIL_GROUP_W], w16[:, WA + DIL_GROUP_W:]])
        pa, pf, pd1, pd2, xk, xv = _inproj(h, g_pre_mix[l][None, :], w16[:, :WA], w_ext[:, cf].astype(BF16),
                                           wd_in, w_ext[:, cc].astype(BF16), tm)
        pa3 = pa.reshape(B, S, WA)
        pf3 = pf.reshape(B, S, WF)

        pos8 = jnp.pad(cmp_pos[l].reshape(2, 1, CMP_BLOCK * HEAD_DIM), ((0, 0), (0, 7), (0, 0))).astype(BF16)
        w2p = jnp.stack([jnp.pad(cmp_w2[l, 0], ((0, 0), (0, HEAD_DIM))),
                         jnp.pad(cmp_w2[l, 1], ((0, 0), (HEAD_DIM, 0)))]).astype(BF16)
        cmp_kv = _compress(xk.reshape(B, nr, CMP_ROW_W), xv.reshape(B, nr, CMP_ROW_W), pos8,
                           cmp_w1[l].astype(BF16), _compress_weights(cmp_w1[l]), w2p, ncp, n_cmp)

        o_nsa = _nsa(pa3, pf3, cmp_kv, S, tq_nsa, tk_nsa, nsa_slopes).reshape(T, NSA_HEADS * LANE)
        dil = [_dilated(pa, P16_DIL, B, S, tm, 0, dil_slopes, 128),
               _dilated(pd1, 0, B, S, tm, 1, dil_slopes, 128),
               _dilated(pd2, 0, B, S, tm, 2, dil_slopes, 128)]
        o_hg = _hgrn2(pf3, hg_lb, l, S, hg_chunk).reshape(T, HG_W)

        wo_ext = jnp.concatenate([w_out[l], jnp.zeros((1, D), w_out.dtype)], axis=0)
        wn = wo_ext[rn].astype(BF16)
        wd = wo_ext[rd].astype(BF16)
        wh = w_out[l, NSA_Q_W + DIL_W:].astype(BF16)
        h = _outproj(o_nsa, [d[0] for d in dil], [d[1] for d in dil], o_hg, h, wn, wd, wh,
                     g_post_mix[l][None, :], tm)
        h = _mlp(h, p[l].reshape(T, PLE_DIM), g_pre_mlp[l][None, :], w_up[l].astype(BF16),
                 w_down[l].astype(BF16), g_post_mlp[l][None, :], g_ple[l][None, :],
                 w_ple_gate[l].astype(BF16), w_ple_proj[l].astype(BF16), tm_mlp)
    return h.reshape(B, S, D)
```

```python
import functools

import numpy as np
import jax
import jax.numpy as jnp
from jax import lax
from jax.experimental import pallas as pl
from jax.experimental.pallas import tpu as pltpu

F32 = jnp.float32
BF16 = jnp.bfloat16

D_MODEL = 1024
DEPTH = 2
HEAD_DIM = 64
NSA_HEADS = 6
NSA_KV_HEADS = 2
NSA_GROUP = NSA_HEADS // NSA_KV_HEADS
CMP_BLOCK = 32
CMP_STRIDE = 16
CMP_HIDDEN = 256
SLC_BLOCK = 64
SLC_TOPN = 8
WIN = 512
FORCE_SCORE = 1e9
DIL_PAIRS = ((128, 1), (512, 4), (2048, 16))
DIL_HPG = 2
DIL_HEADS = DIL_HPG * len(DIL_PAIRS)
HG_HEADS = 4
HG_DIM = 64
LB_TINY = 1e-30
D_FF = 4 * D_MODEL
PLE_DIM = 256
EPS = 1e-6
NEG = -1e30
LOG2E = 1.4426950408889634

NSA_Q_W = NSA_HEADS * HEAD_DIM
NSA_KV_W = NSA_KV_HEADS * HEAD_DIM
DIL_W = DIL_HEADS * HEAD_DIM
HG_W = HG_HEADS * HG_DIM
MIX_W = NSA_Q_W + DIL_W + HG_W
IN_WIDTHS = (NSA_Q_W,) + (NSA_KV_W,) * 6 + (3 * NSA_HEADS,) + (DIL_W,) * 3 + (HG_W,) * 4
IN_TOTAL = sum(IN_WIDTHS)
IN_OFF = tuple(int(v) for v in np.cumsum((0,) + IN_WIDTHS))
(OFF_NQ, OFF_NKC, OFF_NVC, OFF_NKS, OFF_NVS, OFF_NKW, OFF_NVW, OFF_GATE,
 OFF_DQ, OFF_DK, OFF_DV, OFF_HQ, OFF_HF, OFF_HI, OFF_HG) = IN_OFF[:-1]

LANE = 128
VMEM_LIMIT = 56 * 1024 * 1024

DIL_Q_W = DIL_HPG * HEAD_DIM
DIL_KV_W = DIL_HPG * LANE
PA_NQ = 0
PA_DQ = PA_NQ + NSA_Q_W
PA_KVS = PA_DQ + DIL_Q_W
PA_KVW = PA_KVS + NSA_KV_HEADS * LANE
PA_DKV = PA_KVW + NSA_KV_HEADS * LANE
WA = PA_DKV + DIL_KV_W
PD_KV, PD_Q = 0, DIL_KV_W
WD = DIL_KV_W + DIL_Q_W
PF_HQ, PF_HF, PF_HI, PF_HG, PF_GATE = 0, HG_W, 2 * HG_W, 3 * HG_W, 4 * HG_W
WF = PF_GATE + LANE
WC = 2 * NSA_KV_W
CMP_ROW_W = CMP_STRIDE * NSA_KV_W


def _dot(a, b):
    return jnp.dot(a, b, preferred_element_type=F32)


def _dot_nt(a, b):
    return lax.dot_general(a, b, (((1,), (1,)), ((), ())), preferred_element_type=F32)


def _rms(x, g):
    return x * lax.rsqrt(jnp.mean(x * x, axis=-1, keepdims=True) + EPS) * g


def _log2(n):
    l = int(n).bit_length() - 1
    assert (1 << l) == n, n
    return l


def _alibi_slopes():
    n = NSA_HEADS + DIL_HEADS
    s = 2.0 ** (-8.0 * np.arange(1, n + 1) / n)
    quads = s.reshape(-1, 4)
    nsa = [float(np.float32(v)) for v in quads[:, 2:].reshape(-1)]
    dil = [float(np.float32(v)) for v in quads[:, :2].reshape(-1)]
    return nsa, dil


def _layer(shape, l):
    zeros = (0,) * len(shape)
    return pl.BlockSpec((None,) + tuple(shape), lambda *_: (l,) + zeros)


def _inproj_body(x_ref, g_ref, wa_ref, wf_ref, wd_ref, wc_ref,
                 oa_ref, of_ref, od1_ref, od2_ref, xk_ref, xv_ref, hn_scr, *, tm):
    hn32 = _rms(x_ref[...], g_ref[...])
    hn = hn32.astype(BF16)
    for c in range(0, WA, 256):
        oa_ref[:, c:c + 256] = _dot(hn, wa_ref[:, c:c + 256]).astype(BF16)
    for c in range(0, WF, 384):
        of_ref[:, c:c + 384] = _dot(hn, wf_ref[:, c:c + 384])
    nlb = D_MODEL // LANE
    for c in range(nlb):
        hn_scr[c] = hn32[:, c * LANE:(c + 1) * LANE]

    def by_residue(d):
        parts = [jnp.concatenate([hn_scr[c, pl.ds(r, tm // d, stride=d), :] for c in range(nlb)], axis=1)
                 for r in range(d)]
        return jnp.concatenate(parts, axis=0).astype(BF16)

    for gi, o_ref in ((1, od1_ref), (2, od2_ref)):
        hp = by_residue(DIL_PAIRS[gi][1])
        for c0, c1 in ((0, DIL_KV_W), (DIL_KV_W, WD)):
            o_ref[:, c0:c1] = _dot(hp, wd_ref[gi - 1, :, c0:c1]).astype(BF16)
    assert DIL_PAIRS[2][1] == CMP_STRIDE
    cc = _dot(hp, wc_ref[...])
    nr = tm // CMP_STRIDE
    for j in range(CMP_STRIDE):
        xk_ref[:, j * NSA_KV_W:(j + 1) * NSA_KV_W] = cc[j * nr:(j + 1) * nr, 0:NSA_KV_W]
        xv_ref[:, j * NSA_KV_W:(j + 1) * NSA_KV_W] = cc[j * nr:(j + 1) * nr, NSA_KV_W:2 * NSA_KV_W]


def _inproj(x2, g, wa, wf, wd, wc, l, tm):
    T = x2.shape[0]
    row = lambda i: (i, 0)
    nr = tm // CMP_STRIDE
    return pl.pallas_call(
        functools.partial(_inproj_body, tm=tm),
        grid=(T // tm,),
        in_specs=[pl.BlockSpec((tm, D_MODEL), row), _layer((1, D_MODEL), l),
                  _layer((D_MODEL, WA), l), _layer((D_MODEL, WF), l),
                  _layer((2, D_MODEL, WD), l), _layer((D_MODEL, WC), l)],
        out_specs=[pl.BlockSpec((tm, WA), row), pl.BlockSpec((tm, WF), row),
                   pl.BlockSpec((tm, WD), row), pl.BlockSpec((tm, WD), row),
                   pl.BlockSpec((nr, CMP_ROW_W), row), pl.BlockSpec((nr, CMP_ROW_W), row)],
        out_shape=[jax.ShapeDtypeStruct((T, WA), BF16), jax.ShapeDtypeStruct((T, WF), F32),
                   jax.ShapeDtypeStruct((T, WD), BF16), jax.ShapeDtypeStruct((T, WD), BF16),
                   jax.ShapeDtypeStruct((T // CMP_STRIDE, CMP_ROW_W), F32),
                   jax.ShapeDtypeStruct((T // CMP_STRIDE, CMP_ROW_W), F32)],
        scratch_shapes=[pltpu.VMEM((D_MODEL // LANE, tm, LANE), F32)],
        compiler_params=pltpu.CompilerParams(dimension_semantics=("arbitrary",),
                                             vmem_limit_bytes=VMEM_LIMIT),
        name="inproj",
    )(x2, g, wa, wf, wd, wc)


def _compress_body(xk_ref, xv_ref, pos_ref, w1_ref, w1x_ref, w2_ref, out_ref, *, n_cmp):
    nr = xk_ref.shape[1]
    rows = lax.broadcasted_iota(jnp.int32, (nr, 1), 0)
    out_ref[...] = jnp.zeros(out_ref.shape, out_ref.dtype)
    xs = (xk_ref[0].astype(BF16), xv_ref[0].astype(BF16))
    for h in range(NSA_KV_HEADS):
        acc = jnp.zeros((nr, LANE), F32)
        for ten in range(2):
            first = _dot(xs[ten], w1x_ref[ten, h, 0])
            second = _dot(xs[ten], w1x_ref[ten, h, 1])
            posb = _dot(pos_ref[ten], w1_ref[ten])[0:1, :]
            hid = first + pltpu.roll(second, nr - 1, 0) + posb
            act = hid * jax.nn.sigmoid(hid)
            acc = acc + _dot(act.astype(BF16), w2_ref[h, ten])
        acc = jnp.where(rows < n_cmp, acc, 0.0)
        out_ref[0, 0:nr, h * LANE:(h + 1) * LANE] = acc.astype(BF16)


def _compress(xk, xv, pos8, w1, w1x, w2p, l, ncp, n_cmp):
    B, nr, kw = xk.shape
    seq = lambda b: (b, 0, 0)
    return pl.pallas_call(
        functools.partial(_compress_body, n_cmp=n_cmp),
        grid=(B,),
        in_specs=[pl.BlockSpec((1, nr, kw), seq)] * 2
        + [_layer(pos8.shape[1:], l), _layer(w1.shape[1:], l), _layer(w1x.shape[1:], l), _layer(w2p.shape[1:], l)],
        out_specs=pl.BlockSpec((1, ncp, NSA_KV_HEADS * LANE), seq),
        out_shape=jax.ShapeDtypeStruct((B, ncp, NSA_KV_HEADS * LANE), BF16),
        compiler_params=pltpu.CompilerParams(dimension_semantics=("arbitrary",),
                                             vmem_limit_bytes=VMEM_LIMIT),
        name="nsa_compress",
    )(xk, xv, pos8, w1, w1x, w2p)


NSA_PAT_BLK = HEAD_DIM
NSA_PAT_POS = HEAD_DIM + 32


def _nsa_pattern(seq):
    pos = np.arange(seq)
    pat = np.zeros((seq, LANE), np.float32)
    pat[pos, NSA_PAT_BLK + pos // SLC_BLOCK] = 1.0
    pat[:, NSA_PAT_POS:NSA_PAT_POS + 3] = (SLC_BLOCK * (pos // SLC_BLOCK))[:, None]
    pat[:, NSA_PAT_POS + 3:NSA_PAT_POS + 6] = (pos % SLC_BLOCK)[:, None]
    return jnp.asarray(np.stack([pat, np.roll(pat, HEAD_DIM, axis=1)]), BF16)


def _nsa_slope_rows(nsa_slopes):
    rows = np.zeros((NSA_KV_HEADS, 8, LANE), np.float32)
    for kh in range(NSA_KV_HEADS):
        for g in range(NSA_GROUP):
            rest = np.float32(nsa_slopes[kh * NSA_GROUP + g])
            for part in range(3):
                piece = np.float32(np.asarray(rest, dtype=BF16))
                rows[kh, g, NSA_PAT_POS + part] = piece
                rows[kh, g, NSA_PAT_POS + 3 + part] = piece
                rest = np.float32(rest - piece)
    rows[1] = np.roll(rows[1], HEAD_DIM, axis=1)
    return jnp.asarray(rows)


def _nsa_body(q_ref, kvs_ref, kvw_ref, kvc_ref, gate_ref, pat_ref, sl_ref, out_ref,
              ks_scr, vs_scr, kw_scr, vw_scr, s_scr, *, seq, tq, tk, n_cmp, n_slc, k_top, slopes):
    qi = pl.program_id(1)
    G = NSA_GROUP
    KH = NSA_KV_HEADS
    t0 = qi * tq
    ncp = kvc_ref.shape[1]
    ltk = _log2(tk)
    lane = lax.broadcasted_iota(jnp.int32, (1, LANE), 1)
    lo_half = lane < HEAD_DIM
    hi_half = lane >= HEAD_DIM
    mine = (lo_half, hi_half)
    other = (hi_half, lo_half)

    @pl.when(qi == 0)
    def _():
        one = jnp.ones((seq, LANE), BF16)
        for kh in range(KH):
            cols = slice(kh * LANE, (kh + 1) * LANE)
            kvs = kvs_ref[0, :, cols]
            kvw = kvw_ref[0, :, cols]
            ks_scr[kh] = jnp.where(mine[kh], kvs, pat_ref[kh])
            kw_scr[kh] = jnp.where(mine[kh], kvw, pat_ref[kh])
            vs_scr[kh] = jnp.where(mine[kh], one, kvs)
            vw_scr[kh] = jnp.where(mine[kh], one, kvw)

    ii = lax.broadcasted_iota(jnp.int32, (tq, 1), 0)
    row_t = t0 + ii
    nn = lax.broadcasted_iota(jnp.int32, (1, ncp), 1)
    maskc = ((nn * CMP_STRIDE + (CMP_BLOCK - 1)) <= row_t) & (nn < n_cmp)
    absd = jnp.abs(row_t.astype(F32) - (nn.astype(F32) * CMP_STRIDE + 0.5 * (CMP_BLOCK - 1)))
    mi = lax.broadcasted_iota(jnp.int32, (LANE, 1), 0)
    ov_t = ((nn * CMP_STRIDE < mi * SLC_BLOCK + SLC_BLOCK) & (nn * CMP_STRIDE + CMP_BLOCK > mi * SLC_BLOCK)
            & (nn < n_cmp) & (mi < n_slc))
    ov_t = jnp.where(ov_t, 1.0, 0.0).astype(BF16)
    nsp = -(-n_slc // 8) * 8
    mi_s = mi[0:nsp]
    cur_l = (t0 + lax.broadcasted_iota(jnp.int32, (1, tq), 1)) >> _log2(SLC_BLOCK)
    in_rng = mi_s < n_slc
    valid = (mi_s <= cur_l) & in_rng
    forced = (mi_s == 0) | (mi_s == cur_l) | (mi_s == cur_l - 1)

    def select(kh):
        qs = [q_ref[0, :, g * LANE:(g + 1) * LANE] * (HEAD_DIM ** -0.5) for g in range(G)]
        kvc = kvc_ref[0, :, kh * LANE:(kh + 1) * LANE]
        s_c = _dot_nt(jnp.concatenate([jnp.where(mine[kh], q, 0.0) for q in qs], axis=0), kvc)
        ps = []
        for g in range(G):
            s = jnp.where(maskc, s_c[g * tq:(g + 1) * tq] - slopes[kh * G + g] * absd, NEG)
            m = jnp.max(s, axis=-1, keepdims=True)
            e = jnp.where(maskc, jnp.exp(s - m), 0.0)
            ps.append(e / jnp.maximum(jnp.sum(e, axis=-1, keepdims=True), 1e-30))
        o_cmp = _dot(jnp.concatenate(ps, axis=0).astype(BF16), kvc)
        psum = ps[0] + ps[1] + ps[2]
        p_hi = psum.astype(BF16)
        p_lo = (psum - p_hi.astype(F32)).astype(BF16)
        imp_t = _dot_nt(ov_t, p_hi) + _dot_nt(ov_t, p_lo)
        score = jnp.where(valid, jnp.where(forced, FORCE_SCORE, imp_t[0:nsp]), -FORCE_SCORE)
        score = jnp.where(in_rng, score, -3.0 * FORCE_SCORE)
        rank = jnp.zeros((nsp, tq), F32)
        for mp in range(n_slc):
            row = score[mp:mp + 1, :]
            beats = (row > score) | ((row == score) & (mp < mi_s))
            rank = rank + jnp.where(beats, 1.0, 0.0)
        neg_t = jnp.where((rank < k_top) & valid, 0.0, NEG)
        lead = NSA_PAT_BLK if kh == 0 else 0
        pieces = [jnp.zeros((lead, tq), F32), neg_t, jnp.zeros((LANE - lead - nsp, tq), F32)]
        neg_t = jnp.concatenate([x for x in pieces if x.shape[0]], axis=0)
        return qs, o_cmp, neg_t.T

    def extended(kh, qs, neg):
        out = []
        for g in range(G):
            extra = sl_ref[kh, g:g + 1, :] + (0.0 if neg is None else neg)
            out.append(jnp.where(mine[kh], qs[g], extra.astype(BF16)))
        return jnp.concatenate(out, axis=0)

    def normalize(acc, kh):
        den = pltpu.roll(acc, HEAD_DIM, 1)
        return jnp.where(other[kh], acc / jnp.where(other[kh], den, 1.0), 0.0)

    n_kt = (t0 + tq + tk - 1) >> ltk
    nwt = min(-(-(WIN + tq) // tk), seq // tk)
    wk = nwt * tk
    kt0 = jnp.clip(n_kt - nwt, 0, seq // tk - nwt)
    k0 = pl.multiple_of(kt0 * tk, tk)
    span = pl.ds(k0, wk)
    rel = row_t - (k0 + lax.broadcasted_iota(jnp.int32, (1, wk), 1))
    ok_causal = jnp.where(rel >= 0, 0.0, NEG)
    ok_win = jnp.where(rel < WIN, ok_causal, NEG)

    def masked(s, bias):
        return (s.reshape(G, tq, wk) + bias[None]).reshape(G * tq, wk)

    o_cmp, o_win, q_sel, s_near, m_near = [], [], [], [], []
    picked = [select(kh) for kh in range(KH)]
    for kh in range(KH):
        qs, oc, neg = picked[kh]
        o_cmp.append(oc)
        q_sel.append(extended(kh, qs, neg))
        s_near.append(masked(_dot_nt(q_sel[kh], ks_scr[kh, span, :]), ok_causal))
        m_near.append(jnp.max(s_near[kh], axis=-1, keepdims=True))
    for kh in range(KH):
        s = masked(_dot_nt(extended(kh, picked[kh][0], None), kw_scr[kh, span, :]), ok_win)
        p = jnp.exp2(s - jnp.max(s, axis=-1, keepdims=True))
        o_win.append(normalize(_dot(p.astype(BF16), vw_scr[kh, span, :]), kh))

    def far_scores(kt, m_acc):
        out = []
        for kh in range(KH):
            s = _dot_nt(q_sel[kh], ks_scr[kh, pl.ds(pl.multiple_of(kt * tk, tk), tk), :])
            s_scr[kh, kt] = s
            m = m_acc[kh]
            for c in range(0, tk, LANE):
                m = jnp.maximum(m, s[:, c:c + LANE])
            out.append(m)
        return tuple(out)

    m_far = lax.fori_loop(0, kt0, far_scores, tuple(jnp.full((G * tq, LANE), NEG, F32) for _ in range(KH)))
    m_row = [jnp.maximum(jnp.max(m_far[kh], axis=-1, keepdims=True), m_near[kh]) for kh in range(KH)]
    acc0 = tuple(_dot(jnp.exp2(s_near[kh] - m_row[kh]).astype(BF16), vs_scr[kh, span, :]) for kh in range(KH))

    def far_accum(kt, acc):
        out = []
        for kh in range(KH):
            p = jnp.exp2(s_scr[kh, kt] - m_row[kh])
            out.append(acc[kh] + _dot(p.astype(BF16), vs_scr[kh, pl.ds(pl.multiple_of(kt * tk, tk), tk), :]))
        return tuple(out)

    acc = lax.fori_loop(0, kt0, far_accum, acc0)

    gates = jax.nn.sigmoid(gate_ref[0])
    heads = [[], []]
    for kh in range(KH):
        o_sel = normalize(acc[kh], kh)
        for g in range(G):
            hd = kh * G + g
            rows = slice(g * tq, (g + 1) * tq)
            heads[kh].append(gates[:, hd * 3:hd * 3 + 1] * o_cmp[kh][rows]
                             + gates[:, hd * 3 + 1:hd * 3 + 2] * o_sel[rows]
                             + gates[:, hd * 3 + 2:hd * 3 + 3] * o_win[kh][rows])
    for g in range(G):
        out_ref[0, :, g * LANE:(g + 1) * LANE] = jnp.where(lo_half, heads[1][g], heads[0][g]).astype(BF16)


def _nsa(pa, pf, cmp_kv, seq, tq, tk, nsa_slopes):
    B = pa.shape[0]
    n_cmp = (seq - CMP_BLOCK) // CMP_STRIDE + 1
    n_slc = seq // SLC_BLOCK
    ncp = cmp_kv.shape[1]
    kvw = NSA_KV_HEADS * LANE
    pat = _nsa_pattern(seq)
    sl_rows = _nsa_slope_rows([s * LOG2E for s in nsa_slopes])
    body = functools.partial(_nsa_body, seq=seq, tq=tq, tk=tk, n_cmp=n_cmp, n_slc=n_slc,
                             k_top=min(SLC_TOPN, n_slc), slopes=tuple(nsa_slopes))
    return pl.pallas_call(
        body,
        grid=(B, seq // tq),
        in_specs=[pl.BlockSpec((1, tq, NSA_Q_W), lambda b, i: (b, i, PA_NQ // NSA_Q_W)),
                  pl.BlockSpec((1, seq, kvw), lambda b, i: (b, 0, PA_KVS // kvw)),
                  pl.BlockSpec((1, seq, kvw), lambda b, i: (b, 0, PA_KVW // kvw)),
                  pl.BlockSpec((1, ncp, kvw), lambda b, i: (b, 0, 0)),
                  pl.BlockSpec((1, tq, LANE), lambda b, i: (b, i, PF_GATE // LANE)),
                  pl.BlockSpec((NSA_KV_HEADS, seq, LANE), lambda b, i: (0, 0, 0)),
                  pl.BlockSpec((NSA_KV_HEADS, 8, LANE), lambda b, i: (0, 0, 0))],
        out_specs=pl.BlockSpec((1, tq, NSA_Q_W), lambda b, i: (b, i, 0)),
        out_shape=jax.ShapeDtypeStruct((B, seq, NSA_Q_W), BF16),
        scratch_shapes=[pltpu.VMEM((NSA_KV_HEADS, seq, LANE), BF16)] * 4
        + [pltpu.VMEM((NSA_KV_HEADS, seq // tk, NSA_GROUP * tq, tk), F32)],
        compiler_params=pltpu.CompilerParams(dimension_semantics=("arbitrary",) * 2,
                                             vmem_limit_bytes=VMEM_LIMIT),
        name="nsa_attention",
    )(pa, pa, pa, cmp_kv, pf, pat, sl_rows)


def _dilated_body(q_ref, kv_ref, o_ref, lse_ref, *, nres, blk, ls, dil, win_keys, slopes, tq):
    ii = lax.broadcasted_iota(jnp.int32, (tq, 1), 0)
    jj = lax.broadcasted_iota(jnp.int32, (1, 2 * tq), 1)
    rel2 = ii - jj + tq
    ok2 = (rel2 >= 0) & (rel2 <= win_keys)
    lo_half = lax.broadcasted_iota(jnp.int32, (1, LANE), 1) < HEAD_DIM
    mine = (lo_half, jnp.logical_not(lo_half))
    bias2 = [(slopes[h] * float(dil)) * rel2.astype(F32) for h in range(DIL_HPG)]

    def rows(ref, r, t, lanes):
        if blk >= tq:
            a, b = divmod(t * tq, blk)
            return ref[0, a, r, b:b + tq, lanes]
        n = tq // blk
        return ref[0, t * n:(t + 1) * n, r, :, lanes].reshape(tq, LANE)

    def put(ref, r, t, val):
        if blk >= tq:
            a, b = divmod(t * tq, blk)
            ref[0, a, r, b:b + tq, :] = val
        else:
            n = tq // blk
            ref[0, t * n:(t + 1) * n, r, :, :] = val.reshape(n, blk, LANE)

    for r in range(nres):
        for t in range(ls // tq):
            q_pair = rows(q_ref, r, t, slice(0, LANE))
            outs, lses = [], []
            for h in range(DIL_HPG):
                lanes = slice(h * LANE, (h + 1) * LANE)
                qt = jnp.where(mine[h], q_pair, 0.0) * (HEAD_DIM ** -0.5)
                if t == 0:
                    kv = jnp.concatenate([rows(kv_ref, r, 0, lanes)] * 2, axis=0)
                    s = jnp.where(ok2 & (jj >= tq), _dot_nt(qt, kv) - bias2[h], NEG)
                else:
                    kv = jnp.concatenate([rows(kv_ref, r, t - 1, lanes), rows(kv_ref, r, t, lanes)], axis=0)
                    s = jnp.where(ok2, _dot_nt(qt, kv) - bias2[h], NEG)
                m = jnp.max(s, axis=-1, keepdims=True)
                e = jnp.exp(s - m)
                l = jnp.sum(e, axis=-1, keepdims=True)
                outs.append(_dot(e.astype(BF16), kv) / l)
                lses.append(jnp.broadcast_to(m + jnp.log(l), (tq, LANE)))
            put(o_ref, r, t, jnp.where(lo_half, outs[1], outs[0]).astype(BF16))
            put(lse_ref, r, t, jnp.where(lo_half, lses[1], lses[0]))


def _dilated(src, qcol, kvcol, batch, seq, tm, group, slopes, tq):
    win, dil = DIL_PAIRS[group]
    ls = seq // dil
    tq = min(tq, ls)
    if dil == 1:
        ntile, blk = 1, seq
    else:
        ntile, blk = seq // tm, tm // dil
    nres = max(1, min(dil, 16 * tq // ls))
    view = src.reshape(batch, ntile, dil, blk, src.shape[1])
    body = functools.partial(_dilated_body, nres=nres, blk=blk, ls=ls, dil=dil, win_keys=win // dil,
                             slopes=tuple(slopes[group * DIL_HPG:(group + 1) * DIL_HPG]), tq=tq)
    shp = (1, ntile, nres, blk)
    o, lse = pl.pallas_call(
        body,
        grid=(batch, dil // nres),
        in_specs=[pl.BlockSpec(shp + (DIL_Q_W,), lambda b, r: (b, 0, r, 0, qcol // DIL_Q_W)),
                  pl.BlockSpec(shp + (DIL_KV_W,), lambda b, r: (b, 0, r, 0, kvcol // DIL_KV_W))],
        out_specs=[pl.BlockSpec(shp + (DIL_Q_W,), lambda b, r: (b, 0, r, 0, 0))] * 2,
        out_shape=[jax.ShapeDtypeStruct((batch, ntile, dil, blk, DIL_Q_W), BF16),
                   jax.ShapeDtypeStruct((batch, ntile, dil, blk, DIL_Q_W), F32)],
        compiler_params=pltpu.CompilerParams(dimension_semantics=("arbitrary",) * 2,
                                             vmem_limit_bytes=VMEM_LIMIT),
        name=f"dilated_attention_g{group}",
    )(view, view)
    return o.reshape(batch * seq, DIL_Q_W), lse.reshape(batch * seq, DIL_Q_W)


def _hgrn2_body(q_ref, f_ref, i_ref, g_ref, lb_ref, o_ref, *, layer, seq, chunk):
    c = chunk
    sub = 8
    npair = HG_W // LANE
    lbs = lb_ref[...].astype(F32)
    mx = jnp.max(lbs, axis=0, keepdims=True)
    ex = jnp.exp(lbs - mx)
    sm = ex / jnp.sum(ex, axis=0, keepdims=True)
    lower = jnp.maximum(jnp.sum(sm[0:layer + 1], axis=0, keepdims=True) - sm[0:1], 0.0)
    log_lb = jnp.log(lower + LB_TINY)
    log_1m = jnp.log1p(-lower)

    lane = lax.broadcasted_iota(jnp.int32, (1, LANE), 1)
    head0 = lane < HG_DIM
    ri = lax.broadcasted_iota(jnp.int32, (c, 1), 0)
    ci = lax.broadcasted_iota(jnp.int32, (1, c), 1)
    tri = jnp.where(ci <= ri, 1.0, 0.0).astype(BF16)
    di = lax.broadcasted_iota(jnp.int32, (LANE, 1), 0)
    same_head = (di >= HG_DIM) == (lane >= HG_DIM)
    ones_blk = jnp.where(same_head, 1.0, 0.0).astype(BF16)
    gcol = lax.broadcasted_iota(jnp.int32, (1, sub * c), 1)
    gsum = jnp.where(((gcol >> _log2(c)) == (ri & (sub - 1)))
                     & (((gcol & (c - 1)) >> 3) == (ri >> 3)), 1.0, 0.0).astype(BF16)
    sp = lax.broadcasted_iota(jnp.int32, (1, sub, 1), 1)
    levels = []
    w = sub
    while w < c:
        same = (ri >> _log2(2 * w)) == (ci >> _log2(2 * w))
        levels.append((w, same & ((ri & (2 * w - 1)) >= w) & ((ci & (2 * w - 1)) < w)))
        w *= 2

    def split3(x):
        hi = x.astype(BF16)
        r1 = x - hi.astype(F32)
        mid = r1.astype(BF16)
        lo = (r1 - mid.astype(F32)).astype(BF16)
        return hi, mid, lo

    def pair_chunk(q, v, x, gt, llb, l1m, state_t):
        log_sig = jnp.minimum(x, 0.0) - jnp.log1p(jnp.exp(-jnp.abs(x)))
        t2 = l1m + log_sig
        lf = jnp.maximum(llb, t2) + jnp.log1p(jnp.exp(-jnp.abs(llb - t2)))
        kk = 1.0 - jnp.exp(lf)
        hi, mid, lo = split3(lf)
        b = _dot(tri, hi) + _dot(tri, mid) + _dot(tri, lo)
        vb = v.astype(BF16)

        q3 = q.reshape(c // sub, sub, LANE)
        k3 = kk.reshape(c // sub, sub, LANE)
        b3 = b.reshape(c // sub, sub, LANE)
        parts = []
        for tp in range(sub):
            dec = jnp.exp(jnp.minimum(b3[:, tp:tp + 1, :] - b3, 0.0))
            parts.append(jnp.where(sp <= tp, q3[:, tp:tp + 1, :] * k3 * dec, 0.0).reshape(c, LANE))
        wall = jnp.concatenate(parts, axis=0)
        a_rep = _dot(wall.astype(BF16), ones_blk)
        z = a_rep * jnp.concatenate([v] * sub, axis=0)
        o = _dot(gsum, z.astype(BF16))

        a0 = jnp.zeros((c, c), F32)
        a1 = jnp.zeros((c, c), F32)
        for w, lmask in levels:
            b_r = b.reshape(c // (2 * w), 2 * w, LANE)
            bnd = jnp.broadcast_to(b_r[:, w - 1:w, :], b_r.shape).reshape(c, LANE)
            qe = q * jnp.exp(jnp.minimum(b - bnd, 0.0))
            ke = (kk * jnp.exp(jnp.minimum(bnd - b, 0.0))).astype(BF16)
            a0 = a0 + jnp.where(lmask, _dot_nt(jnp.where(head0, qe, 0.0).astype(BF16), ke), 0.0)
            a1 = a1 + jnp.where(lmask, _dot_nt(jnp.where(head0, 0.0, qe).astype(BF16), ke), 0.0)
        o = o + jnp.where(head0, _dot(a0.astype(BF16), vb), _dot(a1.astype(BF16), vb))

        o = o + _dot_nt((q * jnp.exp(b)).astype(BF16), state_t.astype(BF16))
        b_last = b[c - 1:c, :]
        khat = (kk * jnp.exp(b_last - b)).astype(BF16)
        upd = lax.dot_general(vb, khat, (((0,), (0,)), ((), ())), preferred_element_type=F32)
        state_t = jnp.exp(b_last) * state_t + jnp.where(same_head, upd, 0.0)

        o2 = o * o
        ms0 = jnp.sum(jnp.where(head0, o2, 0.0), axis=-1, keepdims=True)
        ms1 = jnp.sum(jnp.where(head0, 0.0, o2), axis=-1, keepdims=True)
        o = o * lax.rsqrt(jnp.where(head0, ms0, ms1) * (1.0 / HG_DIM) + EPS)
        return (o * (gt * jax.nn.sigmoid(gt))).astype(BF16), state_t

    def step(ic, states):
        rows = pl.ds(pl.multiple_of(ic * c, c), c)
        out = []
        for p in range(npair):
            cols = slice(p * LANE, (p + 1) * LANE)
            o, st = pair_chunk(q_ref[0, rows, cols], i_ref[0, rows, cols], f_ref[0, rows, cols],
                               g_ref[0, rows, cols], log_lb[:, cols], log_1m[:, cols], states[p])
            o_ref[0, rows, cols] = o
            out.append(st)
        return tuple(out)

    lax.fori_loop(0, seq // c, step, tuple(jnp.zeros((LANE, LANE), F32) for _ in range(npair)))


def _hgrn2(pf, hg_lb, layer, seq, chunk):
    B = pf.shape[0]
    sec = lambda off: (lambda b: (b, 0, off // HG_W))
    return pl.pallas_call(
        functools.partial(_hgrn2_body, layer=layer, seq=seq, chunk=chunk),
        grid=(B,),
        in_specs=[pl.BlockSpec((1, seq, HG_W), sec(PF_HQ)),
                  pl.BlockSpec((1, seq, HG_W), sec(PF_HF)),
                  pl.BlockSpec((1, seq, HG_W), sec(PF_HI)),
                  pl.BlockSpec((1, seq, HG_W), sec(PF_HG)),
                  pl.BlockSpec((DEPTH, HG_W), lambda b: (0, 0))],
        out_specs=pl.BlockSpec((1, seq, HG_W), lambda b: (b, 0, 0)),
        out_shape=jax.ShapeDtypeStruct((B, seq, HG_W), BF16),
        compiler_params=pltpu.CompilerParams(dimension_semantics=("arbitrary",),
                                             vmem_limit_bytes=VMEM_LIMIT),
        name="hgrn2",
    )(pf, pf, pf, pf, hg_lb)


def _outproj_body(nsa_ref, d0_ref, d1_ref, d2_ref, l0_ref, l1_ref, l2_ref, hg_ref, h_ref,
                  w_ref, g_ref, out_ref, tok_scr, *, tm):
    def token_order(ref, slot, d):
        if d == 1:
            return ref[...].astype(F32)
        n = tm // d
        for r in range(d):
            tok_scr[slot, pl.ds(r, n, stride=d), :] = ref[r * n:(r + 1) * n, :].astype(F32)
        return tok_scr[slot]

    dils = [d for _, d in DIL_PAIRS]
    os_ = [token_order(ref, 2 * g, dils[g]) for g, ref in enumerate((d0_ref, d1_ref, d2_ref))]
    ls = [token_order(ref, 2 * g + 1, dils[g]) for g, ref in enumerate((l0_ref, l1_ref, l2_ref))]
    lm = jnp.maximum(jnp.maximum(ls[0], ls[1]), ls[2])
    es = [jnp.exp(l - lm) for l in ls]
    inv = 1.0 / (es[0] + es[1] + es[2])
    acc = _dot(nsa_ref[...], w_ref[0:NSA_Q_W, :])
    for g in range(len(DIL_PAIRS)):
        r0 = NSA_Q_W + g * DIL_Q_W
        acc = acc + _dot((os_[g] * (es[g] * inv)).astype(BF16), w_ref[r0:r0 + DIL_Q_W, :])
    acc = acc + _dot(hg_ref[...], w_ref[NSA_Q_W + DIL_W:MIX_W, :])
    out_ref[...] = h_ref[...] + _rms(acc, g_ref[...])


def _outproj(nsa, dil_o, dil_l, hg, h2, w, g, l, tm):
    T = h2.shape[0]
    row = lambda i: (i, 0)
    return pl.pallas_call(
        functools.partial(_outproj_body, tm=tm),
        grid=(T // tm,),
        in_specs=[pl.BlockSpec((tm, NSA_Q_W), row)]
        + [pl.BlockSpec((tm, DIL_Q_W), row)] * 6
        + [pl.BlockSpec((tm, HG_W), row), pl.BlockSpec((tm, D_MODEL), row),
           _layer((MIX_W, D_MODEL), l), _layer((1, D_MODEL), l)],
        out_specs=pl.BlockSpec((tm, D_MODEL), row),
        out_shape=jax.ShapeDtypeStruct((T, D_MODEL), F32),
        scratch_shapes=[pltpu.VMEM((2 * len(DIL_PAIRS), tm, DIL_Q_W), F32)],
        compiler_params=pltpu.CompilerParams(dimension_semantics=("arbitrary",),
                                             vmem_limit_bytes=VMEM_LIMIT),
        name="outproj",
    )(nsa, *dil_o, *dil_l, hg, h2, w, g)


def _mlp_body(h_ref, p_ref, gpre_ref, wup_ref, wdn_ref, gpost_ref, gple_ref, wg_ref, wp_ref, out_ref, *, fc):
    h = h_ref[...]
    hn = _rms(h, gpre_ref[...]).astype(BF16)
    acc = jnp.zeros(h.shape, F32)
    for c in range(0, D_FF, fc):
        u = jnp.maximum(_dot(hn, wup_ref[:, c:c + fc]), 0.0)
        acc = acc + _dot((u * u).astype(BF16), wdn_ref[c:c + fc, :])
    h = h + _rms(acc, gpost_ref[...])
    gate = jax.nn.sigmoid(_dot(_rms(h, gple_ref[...]).astype(BF16), wg_ref[...]))
    out_ref[...] = h + _dot(p_ref[...].astype(BF16), wp_ref[...]) * gate


def _mlp(h2, p3, gpre, wup, wdn, gpost, gple, wg, wp, l, tm):
    T = h2.shape[0]
    row = lambda i: (i, 0)
    vec = _layer((1, D_MODEL), l)

    def resident(shape):
        return pl.BlockSpec((None,) + tuple(shape), lambda i: (l, 0, 0), pipeline_mode=pl.Buffered(1))

    return pl.pallas_call(
        functools.partial(_mlp_body, fc=512),
        grid=(T // tm,),
        in_specs=[pl.BlockSpec((tm, D_MODEL), row), pl.BlockSpec((None, tm, PLE_DIM), lambda i: (l, i, 0)), vec,
                  resident(wup.shape[1:]), resident(wdn.shape[1:]), vec, vec,
                  resident(wg.shape[1:]), resident(wp.shape[1:])],
        out_specs=pl.BlockSpec((tm, D_MODEL), row),
        out_shape=jax.ShapeDtypeStruct((T, D_MODEL), F32),
        compiler_params=pltpu.CompilerParams(dimension_semantics=("arbitrary",),
                                             vmem_limit_bytes=VMEM_LIMIT),
        name="mlp_ple",
    )(h2, p3, gpre, wup, wdn, gpost, gple, wg, wp)


def _inproj_columns():
    hd_cols = np.arange(HEAD_DIM)
    ca = np.zeros((WA,), np.int64)
    for g in range(NSA_GROUP):
        for kh in range(NSA_KV_HEADS):
            o = PA_NQ + g * LANE + kh * HEAD_DIM
            ca[o:o + HEAD_DIM] = OFF_NQ + (kh * NSA_GROUP + g) * HEAD_DIM + hd_cols
    ca[PA_DQ:PA_DQ + DIL_Q_W] = OFF_DQ + np.arange(DIL_Q_W)
    for kh in range(NSA_KV_HEADS):
        for base, ko, vo in ((PA_KVS, OFF_NKS, OFF_NVS), (PA_KVW, OFF_NKW, OFF_NVW)):
            o = base + kh * LANE
            first, second = (ko, vo) if kh == 0 else (vo, ko)
            ca[o:o + HEAD_DIM] = first + kh * HEAD_DIM + hd_cols
            ca[o + HEAD_DIM:o + LANE] = second + kh * HEAD_DIM + hd_cols

    def dil_kv(dst, base, g):
        for i in range(DIL_HPG):
            hd = g * DIL_HPG + i
            o = base + i * LANE
            first, second = (OFF_DK, OFF_DV) if i == 0 else (OFF_DV, OFF_DK)
            dst[o:o + HEAD_DIM] = first + hd * HEAD_DIM + hd_cols
            dst[o + HEAD_DIM:o + LANE] = second + hd * HEAD_DIM + hd_cols

    dil_kv(ca, PA_DKV, 0)
    cd = np.zeros((len(DIL_PAIRS) - 1, WD), np.int64)
    for g in range(1, len(DIL_PAIRS)):
        dil_kv(cd[g - 1], PD_KV, g)
        cd[g - 1, PD_Q:PD_Q + DIL_Q_W] = OFF_DQ + g * DIL_Q_W + np.arange(DIL_Q_W)
    cf = np.full((WF,), IN_TOTAL, np.int64)
    for dst, src in ((PF_HQ, OFF_HQ), (PF_HF, OFF_HF), (PF_HI, OFF_HI), (PF_HG, OFF_HG)):
        cf[dst:dst + HG_W] = src + np.arange(HG_W)
    cf[PF_GATE:PF_GATE + 3 * NSA_HEADS] = OFF_GATE + np.arange(3 * NSA_HEADS)
    cc = np.concatenate([OFF_NKC + np.arange(NSA_KV_W), OFF_NVC + np.arange(NSA_KV_W)])
    return ca, cd, cf, cc


def _outproj_rows():
    hd_cols = np.arange(HEAD_DIM)
    rows = np.arange(MIX_W)
    for g in range(NSA_GROUP):
        for slot, kh in enumerate((1, 0)):
            o = g * LANE + slot * HEAD_DIM
            rows[o:o + HEAD_DIM] = (kh * NSA_GROUP + g) * HEAD_DIM + hd_cols
    for g in range(len(DIL_PAIRS)):
        for slot, i in enumerate((1, 0)):
            o = NSA_Q_W + g * DIL_Q_W + slot * HEAD_DIM
            rows[o:o + HEAD_DIM] = NSA_Q_W + (g * DIL_HPG + i) * HEAD_DIM + hd_cols
    return rows


def _compress_weights(w1):
    half = (CMP_BLOCK // 2) * HEAD_DIM
    parts = w1.reshape(DEPTH, 2, 2, CMP_STRIDE, HEAD_DIM, CMP_HIDDEN)
    out = []
    for h in range(NSA_KV_HEADS):
        pad = ((0, 0),) * 4 + ((h * HEAD_DIM, (NSA_KV_HEADS - 1 - h) * HEAD_DIM), (0, 0))
        out.append(jnp.pad(parts, pad).reshape(DEPTH, 2, 2, 2 * half, CMP_HIDDEN))
    return jnp.stack(out, axis=2).astype(BF16)


def kernel(x, p, w_in, w_out, cmp_pos, cmp_w1, cmp_w2, hg_lb, g_pre_mix, g_post_mix,
           g_pre_mlp, g_post_mlp, w_up, w_down, g_ple, w_ple_gate, w_ple_proj):
    B, S, D = x.shape
    T = B * S
    tm = 512 if S % 512 == 0 else 256
    tm_mlp = 512
    tq_nsa, tk_nsa = 256, 256
    hg_chunk = 128
    assert D == D_MODEL and S % tm == 0 and S // SLC_BLOCK <= LANE
    nsa_slopes, dil_slopes = _alibi_slopes()
    ca, cd, cf, cc = _inproj_columns()
    n_cmp = (S - CMP_BLOCK) // CMP_STRIDE + 1
    nr = S // CMP_STRIDE
    ncp = -(-nr // LANE) * LANE

    key_scale = np.ones((1, 1, IN_TOTAL + 1), np.float32)
    key_scale[..., OFF_NKS:OFF_NKS + NSA_KV_W] = LOG2E
    key_scale[..., OFF_NKW:OFF_NKW + NSA_KV_W] = LOG2E
    w_ext = jnp.concatenate([w_in, jnp.zeros((DEPTH, D, 1), w_in.dtype)], axis=2) * key_scale
    wa = w_ext[:, :, ca].astype(BF16)
    wd_in = jnp.stack([w_ext[:, :, cd[g]] for g in range(cd.shape[0])], axis=1).astype(BF16)
    wf = w_ext[:, :, cf].astype(BF16)
    wc = w_ext[:, :, cc].astype(BF16)
    pos8 = jnp.pad(cmp_pos.reshape(DEPTH, 2, 1, CMP_BLOCK * HEAD_DIM), ((0, 0), (0, 0), (0, 7), (0, 0))).astype(BF16)
    lo_pad, hi_pad = ((0, 0), (0, 0), (0, HEAD_DIM)), ((0, 0), (0, 0), (HEAD_DIM, 0))
    w2p = jnp.stack([jnp.stack([jnp.pad(cmp_w2[:, 0], lo_pad), jnp.pad(cmp_w2[:, 1], hi_pad)], axis=1),
                     jnp.stack([jnp.pad(cmp_w2[:, 0], hi_pad), jnp.pad(cmp_w2[:, 1], lo_pad)], axis=1)],
                    axis=1).astype(BF16)
    w1 = cmp_w1.astype(BF16)
    w1x = _compress_weights(cmp_w1)
    wo = w_out[:, _outproj_rows(), :].astype(BF16)
    wup, wdn = w_up.astype(BF16), w_down.astype(BF16)
    wg, wp = w_ple_gate.astype(BF16), w_ple_proj.astype(BF16)
    vec = lambda g: g.reshape(DEPTH, 1, D)
    g_pre_mix, g_post_mix, g_pre_mlp, g_post_mlp, g_ple = map(vec, (g_pre_mix, g_post_mix, g_pre_mlp, g_post_mlp, g_ple))
    p3 = p.reshape(DEPTH, T, PLE_DIM)

    h = x.reshape(T, D)
    for l in range(DEPTH):
        pa, pf, pd1, pd2, xk, xv = _inproj(h, g_pre_mix, wa, wf, wd_in, wc, l, tm)
        pa3 = pa.reshape(B, S, WA)
        pf3 = pf.reshape(B, S, WF)
        cmp_kv = _compress(xk.reshape(B, nr, CMP_ROW_W), xv.reshape(B, nr, CMP_ROW_W), pos8, w1, w1x, w2p,
                           l, ncp, n_cmp)
        o_nsa = _nsa(pa3, pf3, cmp_kv, S, tq_nsa, tk_nsa, nsa_slopes).reshape(T, NSA_Q_W)
        dil = [_dilated(pa, PA_DQ, PA_DKV, B, S, tm, 0, dil_slopes, 128),
               _dilated(pd1, PD_Q, PD_KV, B, S, tm, 1, dil_slopes, 128),
               _dilated(pd2, PD_Q, PD_KV, B, S, tm, 2, dil_slopes, 128)]
        o_hg = _hgrn2(pf3, hg_lb, l, S, hg_chunk).reshape(T, HG_W)
        h = _outproj(o_nsa, [d[0] for d in dil], [d[1] for d in dil], o_hg, h, wo, g_post_mix, l, tm)
        h = _mlp(h, p3, g_pre_mlp, wup, wdn, g_post_mlp, g_ple, wg, wp, l, tm_mlp)
    return h.reshape(B, S, D)
```

```python
import functools

import numpy as np
import jax
import jax.numpy as jnp
from jax import lax
from jax.experimental import pallas as pl
from jax.experimental.pallas import tpu as pltpu

F32 = jnp.float32
BF16 = jnp.bfloat16

D_MODEL = 1024
DEPTH = 2
HEAD_DIM = 64
NSA_HEADS = 6
NSA_KV_HEADS = 2
NSA_GROUP = NSA_HEADS // NSA_KV_HEADS
CMP_BLOCK = 32
CMP_STRIDE = 16
CMP_HIDDEN = 256
SLC_BLOCK = 64
SLC_TOPN = 8
WIN = 512
FORCE_SCORE = 1e9
DIL_PAIRS = ((128, 1), (512, 4), (2048, 16))
DIL_HPG = 2
DIL_HEADS = DIL_HPG * len(DIL_PAIRS)
HG_HEADS = 4
HG_DIM = 64
LB_TINY = 1e-30
D_FF = 4 * D_MODEL
PLE_DIM = 256
EPS = 1e-6
NEG = -1e30
LOG2E = 1.4426950408889634

NSA_Q_W = NSA_HEADS * HEAD_DIM
NSA_KV_W = NSA_KV_HEADS * HEAD_DIM
DIL_W = DIL_HEADS * HEAD_DIM
HG_W = HG_HEADS * HG_DIM
MIX_W = NSA_Q_W + DIL_W + HG_W
IN_WIDTHS = (NSA_Q_W,) + (NSA_KV_W,) * 6 + (3 * NSA_HEADS,) + (DIL_W,) * 3 + (HG_W,) * 4
IN_TOTAL = sum(IN_WIDTHS)
IN_OFF = tuple(int(v) for v in np.cumsum((0,) + IN_WIDTHS))
(OFF_NQ, OFF_NKC, OFF_NVC, OFF_NKS, OFF_NVS, OFF_NKW, OFF_NVW, OFF_GATE,
 OFF_DQ, OFF_DK, OFF_DV, OFF_HQ, OFF_HF, OFF_HI, OFF_HG) = IN_OFF[:-1]

LANE = 128
VMEM_LIMIT = 56 * 1024 * 1024

DIL_Q_W = DIL_HPG * HEAD_DIM
DIL_KV_W = DIL_HPG * LANE
PA_NQ = 0
PA_DQ = PA_NQ + NSA_Q_W
PA_KVS = PA_DQ + DIL_Q_W
PA_KVW = PA_KVS + NSA_KV_HEADS * LANE
PA_DKV = PA_KVW + NSA_KV_HEADS * LANE
WA = PA_DKV + DIL_KV_W
PD_KV, PD_Q = 0, DIL_KV_W
WD = DIL_KV_W + DIL_Q_W
PF_HQ, PF_HF, PF_HI, PF_HG, PF_GATE = 0, HG_W, 2 * HG_W, 3 * HG_W, 4 * HG_W
WF = PF_GATE + LANE
WC = 2 * NSA_KV_W
CMP_ROW_W = CMP_STRIDE * NSA_KV_W


def _dot(a, b):
    return jnp.dot(a, b, preferred_element_type=F32)


def _dot_nt(a, b):
    return lax.dot_general(a, b, (((1,), (1,)), ((), ())), preferred_element_type=F32)


def _rms(x, g):
    return x * lax.rsqrt(jnp.mean(x * x, axis=-1, keepdims=True) + EPS) * g


def _log2(n):
    l = int(n).bit_length() - 1
    assert (1 << l) == n, n
    return l


def _alibi_slopes():
    n = NSA_HEADS + DIL_HEADS
    s = 2.0 ** (-8.0 * np.arange(1, n + 1) / n)
    quads = s.reshape(-1, 4)
    nsa = [float(np.float32(v)) for v in quads[:, 2:].reshape(-1)]
    dil = [float(np.float32(v)) for v in quads[:, :2].reshape(-1)]
    return nsa, dil


def _layer(shape, l):
    zeros = (0,) * len(shape)
    return pl.BlockSpec((None,) + tuple(shape), lambda *_: (l,) + zeros)


def _inproj_body(x_ref, g_ref, wa_ref, wf_ref, wd_ref, wc_ref,
                 oa_ref, of_ref, od1_ref, od2_ref, xk_ref, xv_ref, hn_scr, *, tm):
    hn32 = _rms(x_ref[...], g_ref[...])
    hn = hn32.astype(BF16)
    for c in range(0, WA, 256):
        oa_ref[:, c:c + 256] = _dot(hn, wa_ref[:, c:c + 256]).astype(BF16)
    for c in range(0, WF, 256):
        c1 = min(c + 256, WF)
        of_ref[:, c:c1] = _dot(hn, wf_ref[:, c:c1])
    nlb = D_MODEL // LANE
    for c in range(nlb):
        hn_scr[c] = hn32[:, c * LANE:(c + 1) * LANE]

    def by_residue(d):
        parts = [jnp.concatenate([hn_scr[c, pl.ds(r, tm // d, stride=d), :] for c in range(nlb)], axis=1)
                 for r in range(d)]
        return jnp.concatenate(parts, axis=0).astype(BF16)

    for gi, o_ref in ((1, od1_ref), (2, od2_ref)):
        hp = by_residue(DIL_PAIRS[gi][1])
        for c0, c1 in ((0, DIL_KV_W), (DIL_KV_W, WD)):
            o_ref[:, c0:c1] = _dot(hp, wd_ref[gi - 1, :, c0:c1]).astype(BF16)
    assert DIL_PAIRS[2][1] == CMP_STRIDE
    cc = _dot(hp, wc_ref[...])
    nr = tm // CMP_STRIDE
    for j in range(CMP_STRIDE):
        xk_ref[:, j * NSA_KV_W:(j + 1) * NSA_KV_W] = cc[j * nr:(j + 1) * nr, 0:NSA_KV_W]
        xv_ref[:, j * NSA_KV_W:(j + 1) * NSA_KV_W] = cc[j * nr:(j + 1) * nr, NSA_KV_W:2 * NSA_KV_W]


def _inproj(x2, g, wa, wf, wd, wc, l, tm):
    T = x2.shape[0]
    row = lambda i: (i, 0)
    nr = tm // CMP_STRIDE
    return pl.pallas_call(
        functools.partial(_inproj_body, tm=tm),
        grid=(T // tm,),
        in_specs=[pl.BlockSpec((tm, D_MODEL), row), _layer((1, D_MODEL), l),
                  _layer((D_MODEL, WA), l), _layer((D_MODEL, WF), l),
                  _layer((2, D_MODEL, WD), l), _layer((D_MODEL, WC), l)],
        out_specs=[pl.BlockSpec((tm, WA), row), pl.BlockSpec((tm, WF), row),
                   pl.BlockSpec((tm, WD), row), pl.BlockSpec((tm, WD), row),
                   pl.BlockSpec((nr, CMP_ROW_W), row), pl.BlockSpec((nr, CMP_ROW_W), row)],
        out_shape=[jax.ShapeDtypeStruct((T, WA), BF16), jax.ShapeDtypeStruct((T, WF), F32),
                   jax.ShapeDtypeStruct((T, WD), BF16), jax.ShapeDtypeStruct((T, WD), BF16),
                   jax.ShapeDtypeStruct((T // CMP_STRIDE, CMP_ROW_W), F32),
                   jax.ShapeDtypeStruct((T // CMP_STRIDE, CMP_ROW_W), F32)],
        scratch_shapes=[pltpu.VMEM((D_MODEL // LANE, tm, LANE), F32)],
        compiler_params=pltpu.CompilerParams(dimension_semantics=("arbitrary",),
                                             vmem_limit_bytes=VMEM_LIMIT),
        name="inproj",
    )(x2, g, wa, wf, wd, wc)


def _compress_body(xk_ref, xv_ref, pos_ref, w1_ref, w1x_ref, w2_ref, out_ref, *, n_cmp):
    nr = xk_ref.shape[1]
    rows = lax.broadcasted_iota(jnp.int32, (nr, 1), 0)
    out_ref[...] = jnp.zeros(out_ref.shape, out_ref.dtype)
    xs = (xk_ref[0].astype(BF16), xv_ref[0].astype(BF16))
    for h in range(NSA_KV_HEADS):
        acc = jnp.zeros((nr, LANE), F32)
        for ten in range(2):
            first = _dot(xs[ten], w1x_ref[ten, h, 0])
            second = _dot(xs[ten], w1x_ref[ten, h, 1])
            posb = _dot(pos_ref[ten], w1_ref[ten])[0:1, :]
            hid = first + pltpu.roll(second, nr - 1, 0) + posb
            act = hid * jax.nn.sigmoid(hid)
            acc = acc + _dot(act.astype(BF16), w2_ref[h, ten])
        acc = jnp.where(rows < n_cmp, acc, 0.0)
        out_ref[0, 0:nr, h * LANE:(h + 1) * LANE] = acc.astype(BF16)


def _compress(xk, xv, pos8, w1, w1x, w2p, l, ncp, n_cmp):
    B, nr, kw = xk.shape
    seq = lambda b: (b, 0, 0)
    return pl.pallas_call(
        functools.partial(_compress_body, n_cmp=n_cmp),
        grid=(B,),
        in_specs=[pl.BlockSpec((1, nr, kw), seq)] * 2
        + [_layer(pos8.shape[1:], l), _layer(w1.shape[1:], l), _layer(w1x.shape[1:], l), _layer(w2p.shape[1:], l)],
        out_specs=pl.BlockSpec((1, ncp, NSA_KV_HEADS * LANE), seq),
        out_shape=jax.ShapeDtypeStruct((B, ncp, NSA_KV_HEADS * LANE), BF16),
        compiler_params=pltpu.CompilerParams(dimension_semantics=("arbitrary",),
                                             vmem_limit_bytes=VMEM_LIMIT),
        name="nsa_compress",
    )(xk, xv, pos8, w1, w1x, w2p)


NSA_PAT_BLK = HEAD_DIM
NSA_PAT_POS = HEAD_DIM + 32


def _nsa_pattern(seq):
    pos = np.arange(seq)
    pat = np.zeros((seq, LANE), np.float32)
    pat[pos, NSA_PAT_BLK + pos // SLC_BLOCK] = 1.0
    pat[:, NSA_PAT_POS:NSA_PAT_POS + 3] = (SLC_BLOCK * (pos // SLC_BLOCK))[:, None]
    pat[:, NSA_PAT_POS + 3:NSA_PAT_POS + 6] = (pos % SLC_BLOCK)[:, None]
    return jnp.asarray(np.stack([pat, np.roll(pat, HEAD_DIM, axis=1)]), BF16)


def _nsa_slope_rows(nsa_slopes):
    rows = np.zeros((NSA_KV_HEADS, 8, LANE), np.float32)
    for kh in range(NSA_KV_HEADS):
        for g in range(NSA_GROUP):
            rest = np.float32(nsa_slopes[kh * NSA_GROUP + g])
            for part in range(3):
                piece = np.float32(np.asarray(rest, dtype=BF16))
                rows[kh, g, NSA_PAT_POS + part] = piece
                rows[kh, g, NSA_PAT_POS + 3 + part] = piece
                rest = np.float32(rest - piece)
    rows[1] = np.roll(rows[1], HEAD_DIM, axis=1)
    return jnp.asarray(rows)


def _nsa_body(q_ref, kvs_ref, kvw_ref, kvc_ref, gate_ref, pat_ref, sl_ref, out_ref,
              ks_scr, vs_scr, kw_scr, vw_scr, s_scr, *, seq, tq, tk, n_cmp, n_slc, k_top, slopes):
    qi = pl.program_id(1)
    G = NSA_GROUP
    KH = NSA_KV_HEADS
    t0 = qi * tq
    ncp = kvc_ref.shape[1]
    ltk = _log2(tk)
    lane = lax.broadcasted_iota(jnp.int32, (1, LANE), 1)
    lo_half = lane < HEAD_DIM
    hi_half = lane >= HEAD_DIM
    mine = (lo_half, hi_half)
    other = (hi_half, lo_half)

    @pl.when(qi == 0)
    def _():
        one = jnp.ones((seq, LANE), BF16)
        for kh in range(KH):
            cols = slice(kh * LANE, (kh + 1) * LANE)
            kvs = kvs_ref[0, :, cols]
            kvw = kvw_ref[0, :, cols]
            ks_scr[kh] = jnp.where(mine[kh], kvs, pat_ref[kh])
            kw_scr[kh] = jnp.where(mine[kh], kvw, pat_ref[kh])
            vs_scr[kh] = jnp.where(mine[kh], one, kvs)
            vw_scr[kh] = jnp.where(mine[kh], one, kvw)

    ii = lax.broadcasted_iota(jnp.int32, (tq, 1), 0)
    row_t = t0 + ii
    nn = lax.broadcasted_iota(jnp.int32, (1, ncp), 1)
    maskc = ((nn * CMP_STRIDE + (CMP_BLOCK - 1)) <= row_t) & (nn < n_cmp)
    absd = jnp.abs(row_t.astype(F32) - (nn.astype(F32) * CMP_STRIDE + 0.5 * (CMP_BLOCK - 1)))
    mi = lax.broadcasted_iota(jnp.int32, (LANE, 1), 0)
    ov_t = ((nn * CMP_STRIDE < mi * SLC_BLOCK + SLC_BLOCK) & (nn * CMP_STRIDE + CMP_BLOCK > mi * SLC_BLOCK)
            & (nn < n_cmp) & (mi < n_slc))
    ov_t = jnp.where(ov_t, 1.0, 0.0).astype(BF16)
    nsp = -(-n_slc // 8) * 8
    mi_s = mi[0:nsp]
    cur_l = (t0 + lax.broadcasted_iota(jnp.int32, (1, tq), 1)) >> _log2(SLC_BLOCK)
    in_rng = mi_s < n_slc
    valid = (mi_s <= cur_l) & in_rng
    forced = (mi_s == 0) | (mi_s == cur_l) | (mi_s == cur_l - 1)

    def select(kh):
        qs = [q_ref[0, :, g * LANE:(g + 1) * LANE] * (HEAD_DIM ** -0.5) for g in range(G)]
        kvc = kvc_ref[0, :, kh * LANE:(kh + 1) * LANE]
        s_c = _dot_nt(jnp.concatenate([jnp.where(mine[kh], q, 0.0) for q in qs], axis=0), kvc)
        ps = []
        for g in range(G):
            s = jnp.where(maskc, s_c[g * tq:(g + 1) * tq] - slopes[kh * G + g] * absd, NEG)
            m = jnp.max(s, axis=-1, keepdims=True)
            e = jnp.where(maskc, jnp.exp(s - m), 0.0)
            ps.append(e / jnp.maximum(jnp.sum(e, axis=-1, keepdims=True), 1e-30))
        o_cmp = _dot(jnp.concatenate(ps, axis=0).astype(BF16), kvc)
        psum = ps[0] + ps[1] + ps[2]
        p_hi = psum.astype(BF16)
        p_lo = (psum - p_hi.astype(F32)).astype(BF16)
        imp_t = _dot_nt(ov_t, p_hi) + _dot_nt(ov_t, p_lo)
        score = jnp.where(valid, jnp.where(forced, FORCE_SCORE, imp_t[0:nsp]), -FORCE_SCORE)
        score = jnp.where(in_rng, score, -3.0 * FORCE_SCORE)
        rank = jnp.zeros((nsp, tq), F32)
        for mp in range(n_slc):
            row = score[mp:mp + 1, :]
            beats = (row > score) | ((row == score) & (mp < mi_s))
            rank = rank + jnp.where(beats, 1.0, 0.0)
        neg_t = jnp.where((rank < k_top) & valid, 0.0, NEG)
        lead = NSA_PAT_BLK if kh == 0 else 0
        pieces = [jnp.zeros((lead, tq), F32), neg_t, jnp.zeros((LANE - lead - nsp, tq), F32)]
        neg_t = jnp.concatenate([x for x in pieces if x.shape[0]], axis=0)
        return qs, o_cmp, neg_t.T

    def extended(kh, qs, neg):
        out = []
        for g in range(G):
            extra = sl_ref[kh, g:g + 1, :] + (0.0 if neg is None else neg)
            out.append(jnp.where(mine[kh], qs[g], extra.astype(BF16)))
        return jnp.concatenate(out, axis=0)

    def normalize(acc, kh):
        den = pltpu.roll(acc, HEAD_DIM, 1)
        return jnp.where(other[kh], acc / jnp.where(other[kh], den, 1.0), 0.0)

    n_kt = (t0 + tq + tk - 1) >> ltk
    nwt = min(-(-(WIN + tq) // tk), seq // tk)
    wk = nwt * tk
    kt0 = jnp.clip(n_kt - nwt, 0, seq // tk - nwt)
    k0 = pl.multiple_of(kt0 * tk, tk)
    span = pl.ds(k0, wk)
    rel = row_t - (k0 + lax.broadcasted_iota(jnp.int32, (1, wk), 1))
    ok_causal = jnp.where(rel >= 0, 0.0, NEG)
    ok_win = jnp.where(rel < WIN, ok_causal, NEG)

    def masked(s, bias):
        return (s.reshape(G, tq, wk) + bias[None]).reshape(G * tq, wk)

    o_cmp, o_win, q_sel, s_near, m_near = [], [], [], [], []
    picked = [select(kh) for kh in range(KH)]
    for kh in range(KH):
        qs, oc, neg = picked[kh]
        o_cmp.append(oc)
        q_sel.append(extended(kh, qs, neg))
        s_near.append(masked(_dot_nt(q_sel[kh], ks_scr[kh, span, :]), ok_causal))
        m_near.append(jnp.max(s_near[kh], axis=-1, keepdims=True))
    for kh in range(KH):
        s = masked(_dot_nt(extended(kh, picked[kh][0], None), kw_scr[kh, span, :]), ok_win)
        p = jnp.exp2(s - jnp.max(s, axis=-1, keepdims=True))
        o_win.append(normalize(_dot(p.astype(BF16), vw_scr[kh, span, :]), kh))

    def far_scores(kt, m_acc):
        out = []
        for kh in range(KH):
            s = _dot_nt(q_sel[kh], ks_scr[kh, pl.ds(pl.multiple_of(kt * tk, tk), tk), :])
            s_scr[kh, kt] = s
            m = m_acc[kh]
            for c in range(0, tk, LANE):
                m = jnp.maximum(m, s[:, c:c + LANE])
            out.append(m)
        return tuple(out)

    m_far = lax.fori_loop(0, kt0, far_scores, tuple(jnp.full((G * tq, LANE), NEG, F32) for _ in range(KH)))
    m_row = [jnp.maximum(jnp.max(m_far[kh], axis=-1, keepdims=True), m_near[kh]) for kh in range(KH)]
    acc0 = tuple(_dot(jnp.exp2(s_near[kh] - m_row[kh]).astype(BF16), vs_scr[kh, span, :]) for kh in range(KH))

    def far_accum(kt, acc):
        out = []
        for kh in range(KH):
            p = jnp.exp2(s_scr[kh, kt] - m_row[kh])
            out.append(acc[kh] + _dot(p.astype(BF16), vs_scr[kh, pl.ds(pl.multiple_of(kt * tk, tk), tk), :]))
        return tuple(out)

    acc = lax.fori_loop(0, kt0, far_accum, acc0)

    gates = jax.nn.sigmoid(gate_ref[0])
    heads = [[], []]
    for kh in range(KH):
        o_sel = normalize(acc[kh], kh)
        for g in range(G):
            hd = kh * G + g
            rows = slice(g * tq, (g + 1) * tq)
            heads[kh].append(gates[:, hd * 3:hd * 3 + 1] * o_cmp[kh][rows]
                             + gates[:, hd * 3 + 1:hd * 3 + 2] * o_sel[rows]
                             + gates[:, hd * 3 + 2:hd * 3 + 3] * o_win[kh][rows])
    for g in range(G):
        out_ref[0, :, g * LANE:(g + 1) * LANE] = jnp.where(lo_half, heads[1][g], heads[0][g]).astype(BF16)


def _nsa(pa, pf, cmp_kv, seq, tq, tk, nsa_slopes):
    B = pa.shape[0]
    n_cmp = (seq - CMP_BLOCK) // CMP_STRIDE + 1
    n_slc = seq // SLC_BLOCK
    ncp = cmp_kv.shape[1]
    kvw = NSA_KV_HEADS * LANE
    pat = _nsa_pattern(seq)
    sl_rows = _nsa_slope_rows([s * LOG2E for s in nsa_slopes])
    body = functools.partial(_nsa_body, seq=seq, tq=tq, tk=tk, n_cmp=n_cmp, n_slc=n_slc,
                             k_top=min(SLC_TOPN, n_slc), slopes=tuple(nsa_slopes))
    return pl.pallas_call(
        body,
        grid=(B, seq // tq),
        in_specs=[pl.BlockSpec((1, tq, NSA_Q_W), lambda b, i: (b, i, PA_NQ // NSA_Q_W)),
                  pl.BlockSpec((1, seq, kvw), lambda b, i: (b, 0, PA_KVS // kvw)),
                  pl.BlockSpec((1, seq, kvw), lambda b, i: (b, 0, PA_KVW // kvw)),
                  pl.BlockSpec((1, ncp, kvw), lambda b, i: (b, 0, 0)),
                  pl.BlockSpec((1, tq, LANE), lambda b, i: (b, i, PF_GATE // LANE)),
                  pl.BlockSpec((NSA_KV_HEADS, seq, LANE), lambda b, i: (0, 0, 0)),
                  pl.BlockSpec((NSA_KV_HEADS, 8, LANE), lambda b, i: (0, 0, 0))],
        out_specs=pl.BlockSpec((1, tq, NSA_Q_W), lambda b, i: (b, i, 0)),
        out_shape=jax.ShapeDtypeStruct((B, seq, NSA_Q_W), BF16),
        scratch_shapes=[pltpu.VMEM((NSA_KV_HEADS, seq, LANE), BF16)] * 4
        + [pltpu.VMEM((NSA_KV_HEADS, seq // tk, NSA_GROUP * tq, tk), F32)],
        compiler_params=pltpu.CompilerParams(dimension_semantics=("arbitrary",) * 2,
                                             vmem_limit_bytes=VMEM_LIMIT),
        name="nsa_attention",
    )(pa, pa, pa, cmp_kv, pf, pat, sl_rows)


def _dilated_body(q_ref, kv_ref, o_ref, lse_ref, *, nres, blk, ls, dil, win_keys, slopes, tq):
    ii = lax.broadcasted_iota(jnp.int32, (tq, 1), 0)
    jj = lax.broadcasted_iota(jnp.int32, (1, 2 * tq), 1)
    rel2 = ii - jj + tq
    ok2 = (rel2 >= 0) & (rel2 <= win_keys)
    lo_half = lax.broadcasted_iota(jnp.int32, (1, LANE), 1) < HEAD_DIM
    mine = (lo_half, jnp.logical_not(lo_half))
    bias2 = [(slopes[h] * float(dil)) * rel2.astype(F32) for h in range(DIL_HPG)]

    def rows(ref, r, t, lanes):
        if blk >= tq:
            a, b = divmod(t * tq, blk)
            return ref[0, a, r, b:b + tq, lanes]
        n = tq // blk
        return ref[0, t * n:(t + 1) * n, r, :, lanes].reshape(tq, LANE)

    def put(ref, r, t, val):
        if blk >= tq:
            a, b = divmod(t * tq, blk)
            ref[0, a, r, b:b + tq, :] = val
        else:
            n = tq // blk
            ref[0, t * n:(t + 1) * n, r, :, :] = val.reshape(n, blk, LANE)

    for r in range(nres):
        for t in range(ls // tq):
            q_pair = rows(q_ref, r, t, slice(0, LANE))
            outs, lses = [], []
            for h in range(DIL_HPG):
                lanes = slice(h * LANE, (h + 1) * LANE)
                qt = jnp.where(mine[h], q_pair, 0.0) * (HEAD_DIM ** -0.5)
                if t == 0:
                    kv = jnp.concatenate([rows(kv_ref, r, 0, lanes)] * 2, axis=0)
                    s = jnp.where(ok2 & (jj >= tq), _dot_nt(qt, kv) - bias2[h], NEG)
                else:
                    kv = jnp.concatenate([rows(kv_ref, r, t - 1, lanes), rows(kv_ref, r, t, lanes)], axis=0)
                    s = jnp.where(ok2, _dot_nt(qt, kv) - bias2[h], NEG)
                m = jnp.max(s, axis=-1, keepdims=True)
                e = jnp.exp(s - m)
                l = jnp.sum(e, axis=-1, keepdims=True)
                outs.append(_dot(e.astype(BF16), kv) / l)
                lses.append(jnp.broadcast_to(m + jnp.log(l), (tq, LANE)))
            put(o_ref, r, t, jnp.where(lo_half, outs[1], outs[0]).astype(BF16))
            put(lse_ref, r, t, jnp.where(lo_half, lses[1], lses[0]))


def _dilated(src, qcol, kvcol, batch, seq, tm, group, slopes, tq):
    win, dil = DIL_PAIRS[group]
    ls = seq // dil
    tq = min(tq, ls)
    if dil == 1:
        ntile, blk = 1, seq
    else:
        ntile, blk = seq // tm, tm // dil
    nres = max(1, min(dil, 16 * tq // ls))
    view = src.reshape(batch, ntile, dil, blk, src.shape[1])
    body = functools.partial(_dilated_body, nres=nres, blk=blk, ls=ls, dil=dil, win_keys=win // dil,
                             slopes=tuple(slopes[group * DIL_HPG:(group + 1) * DIL_HPG]), tq=tq)
    shp = (1, ntile, nres, blk)
    o, lse = pl.pallas_call(
        body,
        grid=(batch, dil // nres),
        in_specs=[pl.BlockSpec(shp + (DIL_Q_W,), lambda b, r: (b, 0, r, 0, qcol // DIL_Q_W)),
                  pl.BlockSpec(shp + (DIL_KV_W,), lambda b, r: (b, 0, r, 0, kvcol // DIL_KV_W))],
        out_specs=[pl.BlockSpec(shp + (DIL_Q_W,), lambda b, r: (b, 0, r, 0, 0))] * 2,
        out_shape=[jax.ShapeDtypeStruct((batch, ntile, dil, blk, DIL_Q_W), BF16),
                   jax.ShapeDtypeStruct((batch, ntile, dil, blk, DIL_Q_W), F32)],
        compiler_params=pltpu.CompilerParams(dimension_semantics=("arbitrary",) * 2,
                                             vmem_limit_bytes=VMEM_LIMIT),
        name=f"dilated_attention_g{group}",
    )(view, view)
    return o.reshape(batch * seq, DIL_Q_W), lse.reshape(batch * seq, DIL_Q_W)


def _hgrn2_body(q_ref, f_ref, i_ref, g_ref, lb_ref, o_ref, *, layer, seq, chunk):
    c = chunk
    sub = 8
    npair = HG_W // LANE
    lbs = lb_ref[...].astype(F32)
    mx = jnp.max(lbs, axis=0, keepdims=True)
    ex = jnp.exp(lbs - mx)
    sm = ex / jnp.sum(ex, axis=0, keepdims=True)
    lower = jnp.maximum(jnp.sum(sm[0:layer + 1], axis=0, keepdims=True) - sm[0:1], 0.0)
    log_lb = jnp.log(lower + LB_TINY)
    log_1m = jnp.log1p(-lower)

    lane = lax.broadcasted_iota(jnp.int32, (1, LANE), 1)
    head0 = lane < HG_DIM
    ri = lax.broadcasted_iota(jnp.int32, (c, 1), 0)
    ci = lax.broadcasted_iota(jnp.int32, (1, c), 1)
    tri = jnp.where(ci <= ri, 1.0, 0.0).astype(BF16)
    di = lax.broadcasted_iota(jnp.int32, (LANE, 1), 0)
    same_head = (di >= HG_DIM) == (lane >= HG_DIM)
    ones_blk = jnp.where(same_head, 1.0, 0.0).astype(BF16)
    gcol = lax.broadcasted_iota(jnp.int32, (1, sub * c), 1)
    gsum = jnp.where(((gcol >> _log2(c)) == (ri & (sub - 1)))
                     & (((gcol & (c - 1)) >> 3) == (ri >> 3)), 1.0, 0.0).astype(BF16)
    sp = lax.broadcasted_iota(jnp.int32, (1, sub, 1), 1)
    levels = []
    w = sub
    while w < c:
        same = (ri >> _log2(2 * w)) == (ci >> _log2(2 * w))
        levels.append((w, same & ((ri & (2 * w - 1)) >= w) & ((ci & (2 * w - 1)) < w)))
        w *= 2

    def split3(x):
        hi = x.astype(BF16)
        r1 = x - hi.astype(F32)
        mid = r1.astype(BF16)
        lo = (r1 - mid.astype(F32)).astype(BF16)
        return hi, mid, lo

    def pair_chunk(q, v, x, gt, llb, l1m, state_t):
        log_sig = jnp.minimum(x, 0.0) - jnp.log1p(jnp.exp(-jnp.abs(x)))
        t2 = l1m + log_sig
        lf = jnp.maximum(llb, t2) + jnp.log1p(jnp.exp(-jnp.abs(llb - t2)))
        kk = 1.0 - jnp.exp(lf)
        hi, mid, lo = split3(lf)
        b = _dot(tri, hi) + _dot(tri, mid) + _dot(tri, lo)
        vb = v.astype(BF16)

        q3 = q.reshape(c // sub, sub, LANE)
        k3 = kk.reshape(c // sub, sub, LANE)
        b3 = b.reshape(c // sub, sub, LANE)
        parts = []
        for tp in range(sub):
            dec = jnp.exp(jnp.minimum(b3[:, tp:tp + 1, :] - b3, 0.0))
            parts.append(jnp.where(sp <= tp, q3[:, tp:tp + 1, :] * k3 * dec, 0.0).reshape(c, LANE))
        wall = jnp.concatenate(parts, axis=0)
        a_rep = _dot(wall.astype(BF16), ones_blk)
        z = a_rep * jnp.concatenate([v] * sub, axis=0)
        o = _dot(gsum, z.astype(BF16))

        a0 = jnp.zeros((c, c), F32)
        a1 = jnp.zeros((c, c), F32)
        for w, lmask in levels:
            b_r = b.reshape(c // (2 * w), 2 * w, LANE)
            bnd = jnp.broadcast_to(b_r[:, w - 1:w, :], b_r.shape).reshape(c, LANE)
            qe = q * jnp.exp(jnp.minimum(b - bnd, 0.0))
            ke = (kk * jnp.exp(jnp.minimum(bnd - b, 0.0))).astype(BF16)
            q2 = jnp.concatenate([jnp.where(head0, qe, 0.0), jnp.where(head0, 0.0, qe)], axis=0).astype(BF16)
            a01 = _dot_nt(q2, ke)
            a0 = a0 + jnp.where(lmask, a01[0:c], 0.0)
            a1 = a1 + jnp.where(lmask, a01[c:2 * c], 0.0)
        av = _dot(jnp.concatenate([a0, a1], axis=0).astype(BF16), vb)
        o = o + jnp.where(head0, av[0:c], av[c:2 * c])

        o = o + _dot_nt((q * jnp.exp(b)).astype(BF16), state_t.astype(BF16))
        b_last = b[c - 1:c, :]
        khat = (kk * jnp.exp(b_last - b)).astype(BF16)
        upd = lax.dot_general(vb, khat, (((0,), (0,)), ((), ())), preferred_element_type=F32)
        state_t = jnp.exp(b_last) * state_t + jnp.where(same_head, upd, 0.0)

        o2 = o * o
        ms0 = jnp.sum(jnp.where(head0, o2, 0.0), axis=-1, keepdims=True)
        ms1 = jnp.sum(jnp.where(head0, 0.0, o2), axis=-1, keepdims=True)
        o = o * lax.rsqrt(jnp.where(head0, ms0, ms1) * (1.0 / HG_DIM) + EPS)
        return (o * (gt * jax.nn.sigmoid(gt))).astype(BF16), state_t

    def step(ic, states):
        rows = pl.ds(pl.multiple_of(ic * c, c), c)
        out = []
        for p in range(npair):
            cols = slice(p * LANE, (p + 1) * LANE)
            o, st = pair_chunk(q_ref[0, rows, cols], i_ref[0, rows, cols], f_ref[0, rows, cols],
                               g_ref[0, rows, cols], log_lb[:, cols], log_1m[:, cols], states[p])
            o_ref[0, rows, cols] = o
            out.append(st)
        return tuple(out)

    lax.fori_loop(0, seq // c, step, tuple(jnp.zeros((LANE, LANE), F32) for _ in range(npair)))


def _hgrn2(pf, hg_lb, layer, seq, chunk):
    B = pf.shape[0]
    sec = lambda off: (lambda b: (b, 0, off // HG_W))
    return pl.pallas_call(
        functools.partial(_hgrn2_body, layer=layer, seq=seq, chunk=chunk),
        grid=(B,),
        in_specs=[pl.BlockSpec((1, seq, HG_W), sec(PF_HQ)),
                  pl.BlockSpec((1, seq, HG_W), sec(PF_HF)),
                  pl.BlockSpec((1, seq, HG_W), sec(PF_HI)),
                  pl.BlockSpec((1, seq, HG_W), sec(PF_HG)),
                  pl.BlockSpec((DEPTH, HG_W), lambda b: (0, 0))],
        out_specs=pl.BlockSpec((1, seq, HG_W), lambda b: (b, 0, 0)),
        out_shape=jax.ShapeDtypeStruct((B, seq, HG_W), BF16),
        compiler_params=pltpu.CompilerParams(dimension_semantics=("arbitrary",),
                                             vmem_limit_bytes=VMEM_LIMIT),
        name="hgrn2",
    )(pf, pf, pf, pf, hg_lb)


def _outproj_body(nsa_ref, d0_ref, d1_ref, d2_ref, l0_ref, l1_ref, l2_ref, hg_ref, h_ref,
                  w_ref, g_ref, out_ref, tok_scr, *, tm):
    def token_order(ref, slot, d):
        if d == 1:
            return ref[...].astype(F32)
        n = tm // d
        for r in range(d):
            tok_scr[slot, pl.ds(r, n, stride=d), :] = ref[r * n:(r + 1) * n, :].astype(F32)
        return tok_scr[slot]

    dils = [d for _, d in DIL_PAIRS]
    os_ = [token_order(ref, 2 * g, dils[g]) for g, ref in enumerate((d0_ref, d1_ref, d2_ref))]
    ls = [token_order(ref, 2 * g + 1, dils[g]) for g, ref in enumerate((l0_ref, l1_ref, l2_ref))]
    lm = jnp.maximum(jnp.maximum(ls[0], ls[1]), ls[2])
    es = [jnp.exp(l - lm) for l in ls]
    inv = 1.0 / (es[0] + es[1] + es[2])
    mix = jnp.concatenate([nsa_ref[...]] + [(os_[g] * (es[g] * inv)).astype(BF16) for g in range(len(DIL_PAIRS))]
                          + [hg_ref[...]], axis=1)
    out_ref[...] = h_ref[...] + _rms(_dot(mix, w_ref[...]), g_ref[...])


def _outproj(nsa, dil_o, dil_l, hg, h2, w, g, l, tm):
    T = h2.shape[0]
    row = lambda i: (i, 0)
    return pl.pallas_call(
        functools.partial(_outproj_body, tm=tm),
        grid=(T // tm,),
        in_specs=[pl.BlockSpec((tm, NSA_Q_W), row)]
        + [pl.BlockSpec((tm, DIL_Q_W), row)] * 6
        + [pl.BlockSpec((tm, HG_W), row), pl.BlockSpec((tm, D_MODEL), row),
           _layer((MIX_W, D_MODEL), l), _layer((1, D_MODEL), l)],
        out_specs=pl.BlockSpec((tm, D_MODEL), row),
        out_shape=jax.ShapeDtypeStruct((T, D_MODEL), F32),
        scratch_shapes=[pltpu.VMEM((2 * len(DIL_PAIRS), tm, DIL_Q_W), F32)],
        compiler_params=pltpu.CompilerParams(dimension_semantics=("arbitrary",),
                                             vmem_limit_bytes=VMEM_LIMIT),
        name="outproj",
    )(nsa, *dil_o, *dil_l, hg, h2, w, g)


def _mlp_body(h_ref, p_ref, gpre_ref, wup_ref, wdn_ref, gpost_ref, gple_ref, wg_ref, wp_ref, out_ref, *, fc):
    h = h_ref[...]
    hn = _rms(h, gpre_ref[...]).astype(BF16)
    acc = jnp.zeros(h.shape, F32)
    for c in range(0, D_FF, fc):
        u = jnp.maximum(_dot(hn, wup_ref[:, c:c + fc]), 0.0)
        acc = acc + _dot((u * u).astype(BF16), wdn_ref[c:c + fc, :])
    h = h + _rms(acc, gpost_ref[...])
    gate = jax.nn.sigmoid(_dot(_rms(h, gple_ref[...]).astype(BF16), wg_ref[...]))
    out_ref[...] = h + _dot(p_ref[...].astype(BF16), wp_ref[...]) * gate


def _mlp(h2, p3, gpre, wup, wdn, gpost, gple, wg, wp, l, tm):
    T = h2.shape[0]
    row = lambda i: (i, 0)
    vec = _layer((1, D_MODEL), l)

    def resident(shape):
        return pl.BlockSpec((None,) + tuple(shape), lambda i: (l, 0, 0), pipeline_mode=pl.Buffered(1))

    return pl.pallas_call(
        functools.partial(_mlp_body, fc=512),
        grid=(T // tm,),
        in_specs=[pl.BlockSpec((tm, D_MODEL), row), pl.BlockSpec((None, tm, PLE_DIM), lambda i: (l, i, 0)), vec,
                  resident(wup.shape[1:]), resident(wdn.shape[1:]), vec, vec,
                  resident(wg.shape[1:]), resident(wp.shape[1:])],
        out_specs=pl.BlockSpec((tm, D_MODEL), row),
        out_shape=jax.ShapeDtypeStruct((T, D_MODEL), F32),
        compiler_params=pltpu.CompilerParams(dimension_semantics=("arbitrary",),
                                             vmem_limit_bytes=VMEM_LIMIT),
        name="mlp_ple",
    )(h2, p3, gpre, wup, wdn, gpost, gple, wg, wp)


def _inproj_columns():
    hd_cols = np.arange(HEAD_DIM)
    ca = np.zeros((WA,), np.int64)
    for g in range(NSA_GROUP):
        for kh in range(NSA_KV_HEADS):
            o = PA_NQ + g * LANE + kh * HEAD_DIM
            ca[o:o + HEAD_DIM] = OFF_NQ + (kh * NSA_GROUP + g) * HEAD_DIM + hd_cols
    ca[PA_DQ:PA_DQ + DIL_Q_W] = OFF_DQ + np.arange(DIL_Q_W)
    for kh in range(NSA_KV_HEADS):
        for base, ko, vo in ((PA_KVS, OFF_NKS, OFF_NVS), (PA_KVW, OFF_NKW, OFF_NVW)):
            o = base + kh * LANE
            first, second = (ko, vo) if kh == 0 else (vo, ko)
            ca[o:o + HEAD_DIM] = first + kh * HEAD_DIM + hd_cols
            ca[o + HEAD_DIM:o + LANE] = second + kh * HEAD_DIM + hd_cols

    def dil_kv(dst, base, g):
        for i in range(DIL_HPG):
            hd = g * DIL_HPG + i
            o = base + i * LANE
            first, second = (OFF_DK, OFF_DV) if i == 0 else (OFF_DV, OFF_DK)
            dst[o:o + HEAD_DIM] = first + hd * HEAD_DIM + hd_cols
            dst[o + HEAD_DIM:o + LANE] = second + hd * HEAD_DIM + hd_cols

    dil_kv(ca, PA_DKV, 0)
    cd = np.zeros((len(DIL_PAIRS) - 1, WD), np.int64)
    for g in range(1, len(DIL_PAIRS)):
        dil_kv(cd[g - 1], PD_KV, g)
        cd[g - 1, PD_Q:PD_Q + DIL_Q_W] = OFF_DQ + g * DIL_Q_W + np.arange(DIL_Q_W)
    cf = np.full((WF,), IN_TOTAL, np.int64)
    for dst, src in ((PF_HQ, OFF_HQ), (PF_HF, OFF_HF), (PF_HI, OFF_HI), (PF_HG, OFF_HG)):
        cf[dst:dst + HG_W] = src + np.arange(HG_W)
    cf[PF_GATE:PF_GATE + 3 * NSA_HEADS] = OFF_GATE + np.arange(3 * NSA_HEADS)
    cc = np.concatenate([OFF_NKC + np.arange(NSA_KV_W), OFF_NVC + np.arange(NSA_KV_W)])
    return ca, cd, cf, cc


def _take(w, idx, axis, scale=None):
    idx = np.asarray(idx)
    n_src = w.shape[axis]
    bounds = [0] + [i for i in range(1, len(idx))
                    if (idx[i] != idx[i - 1] + 1 and not (idx[i] == n_src and idx[i - 1] == n_src))
                    or (scale is not None and idx[i] < n_src and idx[i - 1] < n_src and scale[idx[i]] != scale[idx[i - 1]])]
    bounds.append(len(idx))
    pieces = []
    for a, b in zip(bounds[:-1], bounds[1:]):
        src = int(idx[a])
        if src == n_src:
            shape = list(w.shape)
            shape[axis] = b - a
            pieces.append(jnp.zeros(shape, w.dtype))
            continue
        piece = lax.slice_in_dim(w, src, src + (b - a), axis=axis)
        if scale is not None and scale[src] != 1.0:
            piece = piece * float(scale[src])
        pieces.append(piece)
    return jnp.concatenate(pieces, axis=axis)


def _outproj_rows():
    hd_cols = np.arange(HEAD_DIM)
    rows = np.arange(MIX_W)
    for g in range(NSA_GROUP):
        for slot, kh in enumerate((1, 0)):
            o = g * LANE + slot * HEAD_DIM
            rows[o:o + HEAD_DIM] = (kh * NSA_GROUP + g) * HEAD_DIM + hd_cols
    for g in range(len(DIL_PAIRS)):
        for slot, i in enumerate((1, 0)):
            o = NSA_Q_W + g * DIL_Q_W + slot * HEAD_DIM
            rows[o:o + HEAD_DIM] = NSA_Q_W + (g * DIL_HPG + i) * HEAD_DIM + hd_cols
    return rows


def _compress_weights(w1):
    half = (CMP_BLOCK // 2) * HEAD_DIM
    parts = w1.reshape(DEPTH, 2, 2, CMP_STRIDE, HEAD_DIM, CMP_HIDDEN)
    out = []
    for h in range(NSA_KV_HEADS):
        pad = ((0, 0),) * 4 + ((h * HEAD_DIM, (NSA_KV_HEADS - 1 - h) * HEAD_DIM), (0, 0))
        out.append(jnp.pad(parts, pad).reshape(DEPTH, 2, 2, 2 * half, CMP_HIDDEN))
    return jnp.stack(out, axis=2).astype(BF16)


def kernel(x, p, w_in, w_out, cmp_pos, cmp_w1, cmp_w2, hg_lb, g_pre_mix, g_post_mix,
           g_pre_mlp, g_post_mlp, w_up, w_down, g_ple, w_ple_gate, w_ple_proj):
    B, S, D = x.shape
    T = B * S
    tm = 512 if S % 512 == 0 else 256
    tm_mlp = 512
    tq_nsa, tk_nsa = 256, 256
    hg_chunk = 128
    assert D == D_MODEL and S % tm == 0 and S // SLC_BLOCK <= LANE
    nsa_slopes, dil_slopes = _alibi_slopes()
    ca, cd, cf, cc = _inproj_columns()
    n_cmp = (S - CMP_BLOCK) // CMP_STRIDE + 1
    nr = S // CMP_STRIDE
    ncp = -(-nr // LANE) * LANE

    key_scale = np.ones((IN_TOTAL,), np.float32)
    key_scale[OFF_NKS:OFF_NKS + NSA_KV_W] = LOG2E
    key_scale[OFF_NKW:OFF_NKW + NSA_KV_W] = LOG2E
    wa = _take(w_in, ca, 2, scale=key_scale).astype(BF16)
    wd_in = jnp.stack([_take(w_in, cd[g], 2) for g in range(cd.shape[0])], axis=1).astype(BF16)
    wf = _take(w_in, cf, 2).astype(BF16)
    wc = _take(w_in, cc, 2).astype(BF16)
    pos8 = jnp.pad(cmp_pos.reshape(DEPTH, 2, 1, CMP_BLOCK * HEAD_DIM), ((0, 0), (0, 0), (0, 7), (0, 0))).astype(BF16)
    lo_pad, hi_pad = ((0, 0), (0, 0), (0, HEAD_DIM)), ((0, 0), (0, 0), (HEAD_DIM, 0))
    w2p = jnp.stack([jnp.stack([jnp.pad(cmp_w2[:, 0], lo_pad), jnp.pad(cmp_w2[:, 1], hi_pad)], axis=1),
                     jnp.stack([jnp.pad(cmp_w2[:, 0], hi_pad), jnp.pad(cmp_w2[:, 1], lo_pad)], axis=1)],
                    axis=1).astype(BF16)
    w1 = cmp_w1.astype(BF16)
    w1x = _compress_weights(cmp_w1)
    wo = _take(w_out, _outproj_rows(), 1).astype(BF16)
    wup, wdn = w_up.astype(BF16), w_down.astype(BF16)
    wg, wp = w_ple_gate.astype(BF16), w_ple_proj.astype(BF16)
    vec = lambda g: g.reshape(DEPTH, 1, D)
    g_pre_mix, g_post_mix, g_pre_mlp, g_post_mlp, g_ple = map(vec, (g_pre_mix, g_post_mix, g_pre_mlp, g_post_mlp, g_ple))
    p3 = p.reshape(DEPTH, T, PLE_DIM)

    h = x.reshape(T, D)
    for l in range(DEPTH):
        pa, pf, pd1, pd2, xk, xv = _inproj(h, g_pre_mix, wa, wf, wd_in, wc, l, tm)
        pa3 = pa.reshape(B, S, WA)
        pf3 = pf.reshape(B, S, WF)
        cmp_kv = _compress(xk.reshape(B, nr, CMP_ROW_W), xv.reshape(B, nr, CMP_ROW_W), pos8, w1, w1x, w2p,
                           l, ncp, n_cmp)
        o_nsa = _nsa(pa3, pf3, cmp_kv, S, tq_nsa, tk_nsa, nsa_slopes).reshape(T, NSA_Q_W)
        dil = [_dilated(pa, PA_DQ, PA_DKV, B, S, tm, 0, dil_slopes, 128),
               _dilated(pd1, PD_Q, PD_KV, B, S, tm, 1, dil_slopes, 128),
               _dilated(pd2, PD_Q, PD_KV, B, S, tm, 2, dil_slopes, 128)]
        o_hg = _hgrn2(pf3, hg_lb, l, S, hg_chunk).reshape(T, HG_W)
        h = _outproj(o_nsa, [d[0] for d in dil], [d[1] for d in dil], o_hg, h, wo, g_post_mix, l, tm)
        h = _mlp(h, p3, g_pre_mlp, wup, wdn, g_post_mlp, g_ple, wg, wp, l, tm_mlp)
    return h.reshape(B, S, D)
```

```python
import functools

import numpy as np
import jax
import jax.numpy as jnp
from jax import lax
from jax.experimental import pallas as pl
from jax.experimental.pallas import tpu as pltpu

F32 = jnp.float32
BF16 = jnp.bfloat16

D_MODEL = 1024
DEPTH = 2
HEAD_DIM = 64
NSA_HEADS = 6
NSA_KV_HEADS = 2
NSA_GROUP = NSA_HEADS // NSA_KV_HEADS
CMP_BLOCK = 32
CMP_STRIDE = 16
CMP_HIDDEN = 256
SLC_BLOCK = 64
SLC_TOPN = 8
WIN = 512
FORCE_SCORE = 1e9
DIL_PAIRS = ((128, 1), (512, 4), (2048, 16))
DIL_HPG = 2
DIL_HEADS = DIL_HPG * len(DIL_PAIRS)
HG_HEADS = 4
HG_DIM = 64
LB_TINY = 1e-30
D_FF = 4 * D_MODEL
PLE_DIM = 256
EPS = 1e-6
NEG = -1e30
LOG2E = 1.4426950408889634

NSA_Q_W = NSA_HEADS * HEAD_DIM
NSA_KV_W = NSA_KV_HEADS * HEAD_DIM
DIL_W = DIL_HEADS * HEAD_DIM
HG_W = HG_HEADS * HG_DIM
MIX_W = NSA_Q_W + DIL_W + HG_W
IN_WIDTHS = (NSA_Q_W,) + (NSA_KV_W,) * 6 + (3 * NSA_HEADS,) + (DIL_W,) * 3 + (HG_W,) * 4
IN_TOTAL = sum(IN_WIDTHS)
IN_OFF = tuple(int(v) for v in np.cumsum((0,) + IN_WIDTHS))
(OFF_NQ, OFF_NKC, OFF_NVC, OFF_NKS, OFF_NVS, OFF_NKW, OFF_NVW, OFF_GATE,
 OFF_DQ, OFF_DK, OFF_DV, OFF_HQ, OFF_HF, OFF_HI, OFF_HG) = IN_OFF[:-1]

LANE = 128
VMEM_LIMIT = 56 * 1024 * 1024

DIL_Q_W = DIL_HPG * HEAD_DIM
DIL_KV_W = DIL_HPG * LANE
PA_NQ = 0
PA_DQ = PA_NQ + NSA_Q_W
PA_KVS = PA_DQ + DIL_Q_W
PA_KVW = PA_KVS + NSA_KV_HEADS * LANE
PA_DKV = PA_KVW + NSA_KV_HEADS * LANE
WA = PA_DKV + DIL_KV_W
PD_KV, PD_Q = 0, DIL_KV_W
WD = DIL_KV_W + DIL_Q_W
PF_HQ, PF_HF, PF_HI, PF_HG, PF_GATE = 0, HG_W, 2 * HG_W, 3 * HG_W, 4 * HG_W
WF = PF_GATE + LANE
WC = 2 * NSA_KV_W
CMP_ROW_W = CMP_STRIDE * NSA_KV_W


def _dot(a, b):
    return jnp.dot(a, b, preferred_element_type=F32)


def _dot_nt(a, b):
    return lax.dot_general(a, b, (((1,), (1,)), ((), ())), preferred_element_type=F32)


def _rms(x, g):
    return x * lax.rsqrt(jnp.mean(x * x, axis=-1, keepdims=True) + EPS) * g


def _log2(n):
    l = int(n).bit_length() - 1
    assert (1 << l) == n, n
    return l


def _alibi_slopes():
    n = NSA_HEADS + DIL_HEADS
    s = 2.0 ** (-8.0 * np.arange(1, n + 1) / n)
    quads = s.reshape(-1, 4)
    nsa = [float(np.float32(v)) for v in quads[:, 2:].reshape(-1)]
    dil = [float(np.float32(v)) for v in quads[:, :2].reshape(-1)]
    return nsa, dil


def _layer(shape, l):
    zeros = (0,) * len(shape)
    return pl.BlockSpec((None,) + tuple(shape), lambda *_: (l,) + zeros)


def _inproj_body(x_ref, g_ref, wa_ref, wf_ref, wd_ref, wc_ref,
                 oa_ref, of_ref, od1_ref, od2_ref, xk_ref, xv_ref, hn_scr, *, tm):
    hn32 = _rms(x_ref[...], g_ref[...])
    hn = hn32.astype(BF16)
    for c in range(0, WA, 256):
        oa_ref[:, c:c + 256] = _dot(hn, wa_ref[:, c:c + 256]).astype(BF16)
    for c in range(0, WF, 256):
        c1 = min(c + 256, WF)
        of_ref[:, c:c1] = _dot(hn, wf_ref[:, c:c1])
    nlb = D_MODEL // LANE
    for c in range(nlb):
        hn_scr[c] = hn32[:, c * LANE:(c + 1) * LANE]

    def by_residue(d):
        parts = [jnp.concatenate([hn_scr[c, pl.ds(r, tm // d, stride=d), :] for c in range(nlb)], axis=1)
                 for r in range(d)]
        return jnp.concatenate(parts, axis=0).astype(BF16)

    for gi, o_ref in ((1, od1_ref), (2, od2_ref)):
        hp = by_residue(DIL_PAIRS[gi][1])
        for c0, c1 in ((0, DIL_KV_W), (DIL_KV_W, WD)):
            o_ref[:, c0:c1] = _dot(hp, wd_ref[gi - 1, :, c0:c1]).astype(BF16)
    assert DIL_PAIRS[2][1] == CMP_STRIDE
    cc = _dot(hp, wc_ref[...])
    nr = tm // CMP_STRIDE
    for j in range(CMP_STRIDE):
        xk_ref[:, j * NSA_KV_W:(j + 1) * NSA_KV_W] = cc[j * nr:(j + 1) * nr, 0:NSA_KV_W]
        xv_ref[:, j * NSA_KV_W:(j + 1) * NSA_KV_W] = cc[j * nr:(j + 1) * nr, NSA_KV_W:2 * NSA_KV_W]


def _inproj(x2, g, wa, wf, wd, wc, l, tm):
    T = x2.shape[0]
    row = lambda i: (i, 0)
    nr = tm // CMP_STRIDE
    return pl.pallas_call(
        functools.partial(_inproj_body, tm=tm),
        grid=(T // tm,),
        in_specs=[pl.BlockSpec((tm, D_MODEL), row), _layer((1, D_MODEL), l),
                  _layer((D_MODEL, WA), l), _layer((D_MODEL, WF), l),
                  _layer((2, D_MODEL, WD), l), _layer((D_MODEL, WC), l)],
        out_specs=[pl.BlockSpec((tm, WA), row), pl.BlockSpec((tm, WF), row),
                   pl.BlockSpec((tm, WD), row), pl.BlockSpec((tm, WD), row),
                   pl.BlockSpec((nr, CMP_ROW_W), row), pl.BlockSpec((nr, CMP_ROW_W), row)],
        out_shape=[jax.ShapeDtypeStruct((T, WA), BF16), jax.ShapeDtypeStruct((T, WF), F32),
                   jax.ShapeDtypeStruct((T, WD), BF16), jax.ShapeDtypeStruct((T, WD), BF16),
                   jax.ShapeDtypeStruct((T // CMP_STRIDE, CMP_ROW_W), F32),
                   jax.ShapeDtypeStruct((T // CMP_STRIDE, CMP_ROW_W), F32)],
        scratch_shapes=[pltpu.VMEM((D_MODEL // LANE, tm, LANE), F32)],
        compiler_params=pltpu.CompilerParams(dimension_semantics=("arbitrary",),
                                             vmem_limit_bytes=VMEM_LIMIT),
        name="inproj",
    )(x2, g, wa, wf, wd, wc)


def _compress_body(xk_ref, xv_ref, pos_ref, w1_ref, w1x_ref, w2_ref, out_ref, *, n_cmp):
    nr = xk_ref.shape[1]
    rows = lax.broadcasted_iota(jnp.int32, (nr, 1), 0)
    out_ref[...] = jnp.zeros(out_ref.shape, out_ref.dtype)
    xs = (xk_ref[0].astype(BF16), xv_ref[0].astype(BF16))
    for h in range(NSA_KV_HEADS):
        acc = jnp.zeros((nr, LANE), F32)
        for ten in range(2):
            first = _dot(xs[ten], w1x_ref[ten, h, 0])
            second = _dot(xs[ten], w1x_ref[ten, h, 1])
            posb = _dot(pos_ref[ten], w1_ref[ten])[0:1, :]
            hid = first + pltpu.roll(second, nr - 1, 0) + posb
            act = hid * jax.nn.sigmoid(hid)
            acc = acc + _dot(act.astype(BF16), w2_ref[h, ten])
        acc = jnp.where(rows < n_cmp, acc, 0.0)
        out_ref[0, 0:nr, h * LANE:(h + 1) * LANE] = acc.astype(BF16)


def _compress(xk, xv, pos8, w1, w1x, w2p, l, ncp, n_cmp):
    B, nr, kw = xk.shape
    seq = lambda b: (b, 0, 0)
    return pl.pallas_call(
        functools.partial(_compress_body, n_cmp=n_cmp),
        grid=(B,),
        in_specs=[pl.BlockSpec((1, nr, kw), seq)] * 2
        + [_layer(pos8.shape[1:], l), _layer(w1.shape[1:], l), _layer(w1x.shape[1:], l), _layer(w2p.shape[1:], l)],
        out_specs=pl.BlockSpec((1, ncp, NSA_KV_HEADS * LANE), seq),
        out_shape=jax.ShapeDtypeStruct((B, ncp, NSA_KV_HEADS * LANE), BF16),
        compiler_params=pltpu.CompilerParams(dimension_semantics=("arbitrary",),
                                             vmem_limit_bytes=VMEM_LIMIT),
        name="nsa_compress",
    )(xk, xv, pos8, w1, w1x, w2p)


NSA_PAT_BLK = HEAD_DIM
NSA_PAT_POS = HEAD_DIM + 32


def _nsa_pattern(seq):
    pos = np.arange(seq)
    pat = np.zeros((seq, LANE), np.float32)
    pat[pos, NSA_PAT_BLK + pos // SLC_BLOCK] = 1.0
    pat[:, NSA_PAT_POS:NSA_PAT_POS + 3] = (SLC_BLOCK * (pos // SLC_BLOCK))[:, None]
    pat[:, NSA_PAT_POS + 3:NSA_PAT_POS + 6] = (pos % SLC_BLOCK)[:, None]
    return jnp.asarray(np.stack([pat, np.roll(pat, HEAD_DIM, axis=1)]), BF16)


def _nsa_slope_rows(nsa_slopes):
    rows = np.zeros((NSA_KV_HEADS, 8, LANE), np.float32)
    for kh in range(NSA_KV_HEADS):
        for g in range(NSA_GROUP):
            rest = np.float32(nsa_slopes[kh * NSA_GROUP + g])
            for part in range(3):
                piece = np.float32(np.asarray(rest, dtype=BF16))
                rows[kh, g, NSA_PAT_POS + part] = piece
                rows[kh, g, NSA_PAT_POS + 3 + part] = piece
                rest = np.float32(rest - piece)
    rows[1] = np.roll(rows[1], HEAD_DIM, axis=1)
    return jnp.asarray(rows)


def _nsa_body(q_ref, kvs_ref, kvw_ref, kvc_ref, gate_ref, pat_ref, sl_ref, out_ref,
              ks_scr, vs_scr, kw_scr, vw_scr, s_scr, *, seq, tq, tk, n_cmp, n_slc, k_top, slopes):
    qi = pl.program_id(1)
    G = NSA_GROUP
    KH = NSA_KV_HEADS
    t0 = qi * tq
    ncp = kvc_ref.shape[1]
    ltk = _log2(tk)
    lane = lax.broadcasted_iota(jnp.int32, (1, LANE), 1)
    lo_half = lane < HEAD_DIM
    hi_half = lane >= HEAD_DIM
    mine = (lo_half, hi_half)
    other = (hi_half, lo_half)

    @pl.when(qi == 0)
    def _():
        one = jnp.ones((seq, LANE), BF16)
        for kh in range(KH):
            cols = slice(kh * LANE, (kh + 1) * LANE)
            kvs = kvs_ref[0, :, cols]
            kvw = kvw_ref[0, :, cols]
            ks_scr[kh] = jnp.where(mine[kh], kvs, pat_ref[kh])
            kw_scr[kh] = jnp.where(mine[kh], kvw, pat_ref[kh])
            vs_scr[kh] = jnp.where(mine[kh], one, kvs)
            vw_scr[kh] = jnp.where(mine[kh], one, kvw)

    ii = lax.broadcasted_iota(jnp.int32, (tq, 1), 0)
    row_t = t0 + ii
    nn = lax.broadcasted_iota(jnp.int32, (1, ncp), 1)
    maskc = ((nn * CMP_STRIDE + (CMP_BLOCK - 1)) <= row_t) & (nn < n_cmp)
    absd = jnp.abs(row_t.astype(F32) - (nn.astype(F32) * CMP_STRIDE + 0.5 * (CMP_BLOCK - 1)))
    mi = lax.broadcasted_iota(jnp.int32, (LANE, 1), 0)
    ov_t = ((nn * CMP_STRIDE < mi * SLC_BLOCK + SLC_BLOCK) & (nn * CMP_STRIDE + CMP_BLOCK > mi * SLC_BLOCK)
            & (nn < n_cmp) & (mi < n_slc))
    ov_t = jnp.where(ov_t, 1.0, 0.0).astype(BF16)
    nsp = -(-n_slc // 8) * 8
    mi_s = mi[0:nsp]
    cur_l = (t0 + lax.broadcasted_iota(jnp.int32, (1, tq), 1)) >> _log2(SLC_BLOCK)
    in_rng = mi_s < n_slc
    valid = (mi_s <= cur_l) & in_rng
    forced = (mi_s == 0) | (mi_s == cur_l) | (mi_s == cur_l - 1)

    def select(kh):
        qs = [q_ref[0, :, g * LANE:(g + 1) * LANE] * (HEAD_DIM ** -0.5) for g in range(G)]
        kvc = kvc_ref[0, :, kh * LANE:(kh + 1) * LANE]
        s_c = _dot_nt(jnp.concatenate([jnp.where(mine[kh], q, 0.0) for q in qs], axis=0), kvc)
        ps = []
        for g in range(G):
            s = jnp.where(maskc, s_c[g * tq:(g + 1) * tq] - slopes[kh * G + g] * absd, NEG)
            m = jnp.max(s, axis=-1, keepdims=True)
            e = jnp.where(maskc, jnp.exp(s - m), 0.0)
            ps.append(e / jnp.maximum(jnp.sum(e, axis=-1, keepdims=True), 1e-30))
        o_cmp = _dot(jnp.concatenate(ps, axis=0).astype(BF16), kvc)
        psum = ps[0] + ps[1] + ps[2]
        p_hi = psum.astype(BF16)
        p_lo = (psum - p_hi.astype(F32)).astype(BF16)
        imp_t = _dot_nt(ov_t, p_hi) + _dot_nt(ov_t, p_lo)
        score = jnp.where(valid, jnp.where(forced, FORCE_SCORE, imp_t[0:nsp]), -FORCE_SCORE)
        score = jnp.where(in_rng, score, -3.0 * FORCE_SCORE)
        rank = jnp.zeros((nsp, tq), F32)
        for mp in range(n_slc):
            row = score[mp:mp + 1, :]
            beats = (row > score) | ((row == score) & (mp < mi_s))
            rank = rank + jnp.where(beats, 1.0, 0.0)
        neg_t = jnp.where((rank < k_top) & valid, 0.0, NEG)
        lead = NSA_PAT_BLK if kh == 0 else 0
        pieces = [jnp.zeros((lead, tq), F32), neg_t, jnp.zeros((LANE - lead - nsp, tq), F32)]
        neg_t = jnp.concatenate([x for x in pieces if x.shape[0]], axis=0)
        return qs, o_cmp, neg_t.T

    def extended(kh, qs, neg):
        out = []
        for g in range(G):
            extra = sl_ref[kh, g:g + 1, :] + (0.0 if neg is None else neg)
            out.append(jnp.where(mine[kh], qs[g], extra.astype(BF16)))
        return jnp.concatenate(out, axis=0)

    def normalize(acc, kh):
        den = pltpu.roll(acc, HEAD_DIM, 1)
        return jnp.where(other[kh], acc / jnp.where(other[kh], den, 1.0), 0.0)

    n_kt = (t0 + tq + tk - 1) >> ltk
    nwt = min(-(-(WIN + tq) // tk), seq // tk)
    wk = nwt * tk
    kt0 = jnp.clip(n_kt - nwt, 0, seq // tk - nwt)
    k0 = pl.multiple_of(kt0 * tk, tk)
    span = pl.ds(k0, wk)
    rel = row_t - (k0 + lax.broadcasted_iota(jnp.int32, (1, wk), 1))
    ok_causal = jnp.where(rel >= 0, 0.0, NEG)
    ok_win = jnp.where(rel < WIN, ok_causal, NEG)

    def masked(s, bias):
        return (s.reshape(G, tq, wk) + bias[None]).reshape(G * tq, wk)

    o_cmp, o_win, q_sel, s_near, m_near = [], [], [], [], []
    picked = [select(kh) for kh in range(KH)]
    for kh in range(KH):
        qs, oc, neg = picked[kh]
        o_cmp.append(oc)
        q_sel.append(extended(kh, qs, neg))
        s_near.append(masked(_dot_nt(q_sel[kh], ks_scr[kh, span, :]), ok_causal))
        m_near.append(jnp.max(s_near[kh], axis=-1, keepdims=True))
    for kh in range(KH):
        s = masked(_dot_nt(extended(kh, picked[kh][0], None), kw_scr[kh, span, :]), ok_win)
        p = jnp.exp2(s - jnp.max(s, axis=-1, keepdims=True))
        o_win.append(normalize(_dot(p.astype(BF16), vw_scr[kh, span, :]), kh))

    def far_scores(kt, m_acc):
        out = []
        for kh in range(KH):
            s = _dot_nt(q_sel[kh], ks_scr[kh, pl.ds(pl.multiple_of(kt * tk, tk), tk), :])
            s_scr[kh, kt] = s
            m = m_acc[kh]
            for c in range(0, tk, LANE):
                m = jnp.maximum(m, s[:, c:c + LANE])
            out.append(m)
        return tuple(out)

    m_far = lax.fori_loop(0, kt0, far_scores, tuple(jnp.full((G * tq, LANE), NEG, F32) for _ in range(KH)))
    m_row = [jnp.maximum(jnp.max(m_far[kh], axis=-1, keepdims=True), m_near[kh]) for kh in range(KH)]
    acc0 = tuple(_dot(jnp.exp2(s_near[kh] - m_row[kh]).astype(BF16), vs_scr[kh, span, :]) for kh in range(KH))

    def far_accum(kt, acc):
        out = []
        for kh in range(KH):
            p = jnp.exp2(s_scr[kh, kt] - m_row[kh])
            out.append(acc[kh] + _dot(p.astype(BF16), vs_scr[kh, pl.ds(pl.multiple_of(kt * tk, tk), tk), :]))
        return tuple(out)

    acc = lax.fori_loop(0, kt0, far_accum, acc0)

    gates = jax.nn.sigmoid(gate_ref[0])
    heads = [[], []]
    for kh in range(KH):
        o_sel = normalize(acc[kh], kh)
        for g in range(G):
            hd = kh * G + g
            rows = slice(g * tq, (g + 1) * tq)
            heads[kh].append(gates[:, hd * 3:hd * 3 + 1] * o_cmp[kh][rows]
                             + gates[:, hd * 3 + 1:hd * 3 + 2] * o_sel[rows]
                             + gates[:, hd * 3 + 2:hd * 3 + 3] * o_win[kh][rows])
    for g in range(G):
        out_ref[0, :, g * LANE:(g + 1) * LANE] = jnp.where(lo_half, heads[1][g], heads[0][g]).astype(BF16)


def _nsa(pa, pf, cmp_kv, seq, tq, tk, nsa_slopes):
    B = pa.shape[0]
    n_cmp = (seq - CMP_BLOCK) // CMP_STRIDE + 1
    n_slc = seq // SLC_BLOCK
    ncp = cmp_kv.shape[1]
    kvw = NSA_KV_HEADS * LANE
    pat = _nsa_pattern(seq)
    sl_rows = _nsa_slope_rows([s * LOG2E for s in nsa_slopes])
    body = functools.partial(_nsa_body, seq=seq, tq=tq, tk=tk, n_cmp=n_cmp, n_slc=n_slc,
                             k_top=min(SLC_TOPN, n_slc), slopes=tuple(nsa_slopes))
    return pl.pallas_call(
        body,
        grid=(B, seq // tq),
        in_specs=[pl.BlockSpec((1, tq, NSA_Q_W), lambda b, i: (b, i, PA_NQ // NSA_Q_W)),
                  pl.BlockSpec((1, seq, kvw), lambda b, i: (b, 0, PA_KVS // kvw)),
                  pl.BlockSpec((1, seq, kvw), lambda b, i: (b, 0, PA_KVW // kvw)),
                  pl.BlockSpec((1, ncp, kvw), lambda b, i: (b, 0, 0)),
                  pl.BlockSpec((1, tq, LANE), lambda b, i: (b, i, PF_GATE // LANE)),
                  pl.BlockSpec((NSA_KV_HEADS, seq, LANE), lambda b, i: (0, 0, 0)),
                  pl.BlockSpec((NSA_KV_HEADS, 8, LANE), lambda b, i: (0, 0, 0))],
        out_specs=pl.BlockSpec((1, tq, NSA_Q_W), lambda b, i: (b, i, 0)),
        out_shape=jax.ShapeDtypeStruct((B, seq, NSA_Q_W), BF16),
        scratch_shapes=[pltpu.VMEM((NSA_KV_HEADS, seq, LANE), BF16)] * 4
        + [pltpu.VMEM((NSA_KV_HEADS, seq // tk, NSA_GROUP * tq, tk), F32)],
        compiler_params=pltpu.CompilerParams(dimension_semantics=("arbitrary",) * 2,
                                             vmem_limit_bytes=VMEM_LIMIT),
        name="nsa_attention",
    )(pa, pa, pa, cmp_kv, pf, pat, sl_rows)


def _dilated_body(q_ref, kv_ref, o_ref, lse_ref, *, nres, blk, ls, dil, win_keys, slopes, tq):
    ii = lax.broadcasted_iota(jnp.int32, (tq, 1), 0)
    jj = lax.broadcasted_iota(jnp.int32, (1, 2 * tq), 1)
    rel2 = ii - jj + tq
    ok2 = (rel2 >= 0) & (rel2 <= win_keys)
    lo_half = lax.broadcasted_iota(jnp.int32, (1, LANE), 1) < HEAD_DIM
    mine = (lo_half, jnp.logical_not(lo_half))
    bias2 = [(slopes[h] * float(dil)) * rel2.astype(F32) for h in range(DIL_HPG)]

    def rows(ref, r, t, lanes):
        if blk >= tq:
            a, b = divmod(t * tq, blk)
            return ref[0, a, r, b:b + tq, lanes]
        n = tq // blk
        return ref[0, t * n:(t + 1) * n, r, :, lanes].reshape(tq, LANE)

    def put(ref, r, t, val):
        if blk >= tq:
            a, b = divmod(t * tq, blk)
            ref[0, a, r, b:b + tq, :] = val
        else:
            n = tq // blk
            ref[0, t * n:(t + 1) * n, r, :, :] = val.reshape(n, blk, LANE)

    for r in range(nres):
        for t in range(ls // tq):
            q_pair = rows(q_ref, r, t, slice(0, LANE))
            outs, lses = [], []
            for h in range(DIL_HPG):
                lanes = slice(h * LANE, (h + 1) * LANE)
                qt = jnp.where(mine[h], q_pair, 0.0) * (HEAD_DIM ** -0.5)
                if t == 0:
                    kv = jnp.concatenate([rows(kv_ref, r, 0, lanes)] * 2, axis=0)
                    s = jnp.where(ok2 & (jj >= tq), _dot_nt(qt, kv) - bias2[h], NEG)
                else:
                    kv = jnp.concatenate([rows(kv_ref, r, t - 1, lanes), rows(kv_ref, r, t, lanes)], axis=0)
                    s = jnp.where(ok2, _dot_nt(qt, kv) - bias2[h], NEG)
                m = jnp.max(s, axis=-1, keepdims=True)
                e = jnp.exp(s - m)
                l = jnp.sum(e, axis=-1, keepdims=True)
                outs.append(_dot(e.astype(BF16), kv) / l)
                lses.append(jnp.broadcast_to(m + jnp.log(l), (tq, LANE)))
            put(o_ref, r, t, jnp.where(lo_half, outs[1], outs[0]).astype(BF16))
            put(lse_ref, r, t, jnp.where(lo_half, lses[1], lses[0]))


def _dilated(src, qcol, kvcol, batch, seq, tm, group, slopes, tq):
    win, dil = DIL_PAIRS[group]
    ls = seq // dil
    tq = min(tq, ls)
    if dil == 1:
        ntile, blk = 1, seq
    else:
        ntile, blk = seq // tm, tm // dil
    nres = max(1, min(dil, 16 * tq // ls))
    view = src.reshape(batch, ntile, dil, blk, src.shape[1])
    body = functools.partial(_dilated_body, nres=nres, blk=blk, ls=ls, dil=dil, win_keys=win // dil,
                             slopes=tuple(slopes[group * DIL_HPG:(group + 1) * DIL_HPG]), tq=tq)
    shp = (1, ntile, nres, blk)
    o, lse = pl.pallas_call(
        body,
        grid=(batch, dil // nres),
        in_specs=[pl.BlockSpec(shp + (DIL_Q_W,), lambda b, r: (b, 0, r, 0, qcol // DIL_Q_W)),
                  pl.BlockSpec(shp + (DIL_KV_W,), lambda b, r: (b, 0, r, 0, kvcol // DIL_KV_W))],
        out_specs=[pl.BlockSpec(shp + (DIL_Q_W,), lambda b, r: (b, 0, r, 0, 0))] * 2,
        out_shape=[jax.ShapeDtypeStruct((batch, ntile, dil, blk, DIL_Q_W), BF16),
                   jax.ShapeDtypeStruct((batch, ntile, dil, blk, DIL_Q_W), F32)],
        compiler_params=pltpu.CompilerParams(dimension_semantics=("arbitrary",) * 2,
                                             vmem_limit_bytes=VMEM_LIMIT),
        name=f"dilated_attention_g{group}",
    )(view, view)
    return o.reshape(batch * seq, DIL_Q_W), lse.reshape(batch * seq, DIL_Q_W)


def _hgrn2_body(q_ref, f_ref, i_ref, g_ref, lb_ref, o_ref, *, layer, seq, chunk):
    c = chunk
    sub = 8
    npair = HG_W // LANE
    lbs = lb_ref[...].astype(F32)
    mx = jnp.max(lbs, axis=0, keepdims=True)
    ex = jnp.exp(lbs - mx)
    sm = ex / jnp.sum(ex, axis=0, keepdims=True)
    lower = jnp.maximum(jnp.sum(sm[0:layer + 1], axis=0, keepdims=True) - sm[0:1], 0.0)
    log_lb = jnp.log(lower + LB_TINY)
    log_1m = jnp.log1p(-lower)

    lane = lax.broadcasted_iota(jnp.int32, (1, LANE), 1)
    head0 = lane < HG_DIM
    ri = lax.broadcasted_iota(jnp.int32, (c, 1), 0)
    ci = lax.broadcasted_iota(jnp.int32, (1, c), 1)
    tri = jnp.where(ci <= ri, 1.0, 0.0).astype(BF16)
    di = lax.broadcasted_iota(jnp.int32, (LANE, 1), 0)
    same_head = (di >= HG_DIM) == (lane >= HG_DIM)
    ones_blk = jnp.where(same_head, 1.0, 0.0).astype(BF16)
    gcol = lax.broadcasted_iota(jnp.int32, (1, sub * c), 1)
    gsum = jnp.where(((gcol >> _log2(c)) == (ri & (sub - 1)))
                     & (((gcol & (c - 1)) >> 3) == (ri >> 3)), 1.0, 0.0).astype(BF16)
    sp = lax.broadcasted_iota(jnp.int32, (1, sub, 1), 1)
    levels = []
    w = sub
    while w < c:
        same = (ri >> _log2(2 * w)) == (ci >> _log2(2 * w))
        levels.append((w, same & ((ri & (2 * w - 1)) >= w) & ((ci & (2 * w - 1)) < w)))
        w *= 2

    def split3(x):
        hi = x.astype(BF16)
        r1 = x - hi.astype(F32)
        mid = r1.astype(BF16)
        lo = (r1 - mid.astype(F32)).astype(BF16)
        return hi, mid, lo

    def pair_chunk(q, v, x, gt, llb, l1m, state_t):
        log_sig = jnp.minimum(x, 0.0) - jnp.log1p(jnp.exp(-jnp.abs(x)))
        t2 = l1m + log_sig
        lf = jnp.maximum(llb, t2) + jnp.log1p(jnp.exp(-jnp.abs(llb - t2)))
        kk = 1.0 - jnp.exp(lf)
        hi, mid, lo = split3(lf * LOG2E)
        b = _dot(tri, hi) + _dot(tri, mid) + _dot(tri, lo)
        vb = v.astype(BF16)

        q3 = q.reshape(c // sub, sub, LANE)
        k3 = kk.reshape(c // sub, sub, LANE)
        b3 = b.reshape(c // sub, sub, LANE)
        parts = []
        for tp in range(sub):
            dec = jnp.exp2(jnp.minimum(b3[:, tp:tp + 1, :] - b3, 0.0))
            parts.append(jnp.where(sp <= tp, q3[:, tp:tp + 1, :] * k3 * dec, 0.0).reshape(c, LANE))
        wall = jnp.concatenate(parts, axis=0)
        a_rep = _dot(wall.astype(BF16), ones_blk)
        z = a_rep * jnp.concatenate([v] * sub, axis=0)
        o = _dot(gsum, z.astype(BF16))

        a0 = jnp.zeros((c, c), F32)
        a1 = jnp.zeros((c, c), F32)
        for w, lmask in levels:
            b_r = b.reshape(c // (2 * w), 2 * w, LANE)
            bnd = jnp.broadcast_to(b_r[:, w - 1:w, :], b_r.shape).reshape(c, LANE)
            qe = q * jnp.exp2(jnp.minimum(b - bnd, 0.0))
            ke = (kk * jnp.exp2(jnp.minimum(bnd - b, 0.0))).astype(BF16)
            q2 = jnp.concatenate([jnp.where(head0, qe, 0.0), jnp.where(head0, 0.0, qe)], axis=0).astype(BF16)
            a01 = _dot_nt(q2, ke)
            a0 = a0 + jnp.where(lmask, a01[0:c], 0.0)
            a1 = a1 + jnp.where(lmask, a01[c:2 * c], 0.0)
        av = _dot(jnp.concatenate([a0, a1], axis=0).astype(BF16), vb)
        o = o + jnp.where(head0, av[0:c], av[c:2 * c])

        o = o + _dot_nt((q * jnp.exp2(b)).astype(BF16), state_t.astype(BF16))
        b_last = b[c - 1:c, :]
        khat = (kk * jnp.exp2(b_last - b)).astype(BF16)
        upd = lax.dot_general(vb, khat, (((0,), (0,)), ((), ())), preferred_element_type=F32)
        state_t = jnp.exp2(b_last) * state_t + jnp.where(same_head, upd, 0.0)

        o2 = o * o
        ms0 = jnp.sum(jnp.where(head0, o2, 0.0), axis=-1, keepdims=True)
        ms1 = jnp.sum(jnp.where(head0, 0.0, o2), axis=-1, keepdims=True)
        o = o * lax.rsqrt(jnp.where(head0, ms0, ms1) * (1.0 / HG_DIM) + EPS)
        return (o * (gt * jax.nn.sigmoid(gt))).astype(BF16), state_t

    def step(ic, states):
        rows = pl.ds(pl.multiple_of(ic * c, c), c)
        out = []
        for p in range(npair):
            cols = slice(p * LANE, (p + 1) * LANE)
            o, st = pair_chunk(q_ref[0, rows, cols], i_ref[0, rows, cols], f_ref[0, rows, cols],
                               g_ref[0, rows, cols], log_lb[:, cols], log_1m[:, cols], states[p])
            o_ref[0, rows, cols] = o
            out.append(st)
        return tuple(out)

    lax.fori_loop(0, seq // c, step, tuple(jnp.zeros((LANE, LANE), F32) for _ in range(npair)),
                  unroll=min(4, seq // c))


def _hgrn2(pf, hg_lb, layer, seq, chunk):
    B = pf.shape[0]
    sec = lambda off: (lambda b: (b, 0, off // HG_W))
    return pl.pallas_call(
        functools.partial(_hgrn2_body, layer=layer, seq=seq, chunk=chunk),
        grid=(B,),
        in_specs=[pl.BlockSpec((1, seq, HG_W), sec(PF_HQ)),
                  pl.BlockSpec((1, seq, HG_W), sec(PF_HF)),
                  pl.BlockSpec((1, seq, HG_W), sec(PF_HI)),
                  pl.BlockSpec((1, seq, HG_W), sec(PF_HG)),
                  pl.BlockSpec((DEPTH, HG_W), lambda b: (0, 0))],
        out_specs=pl.BlockSpec((1, seq, HG_W), lambda b: (b, 0, 0)),
        out_shape=jax.ShapeDtypeStruct((B, seq, HG_W), BF16),
        compiler_params=pltpu.CompilerParams(dimension_semantics=("arbitrary",),
                                             vmem_limit_bytes=VMEM_LIMIT),
        name="hgrn2",
    )(pf, pf, pf, pf, hg_lb)


def _outproj_body(nsa_ref, d0_ref, d1_ref, d2_ref, l0_ref, l1_ref, l2_ref, hg_ref, h_ref,
                  w_ref, g_ref, out_ref, tok_scr, *, tm):
    def token_order(ref, slot, d):
        if d == 1:
            return ref[...].astype(F32)
        n = tm // d
        for r in range(d):
            tok_scr[slot, pl.ds(r, n, stride=d), :] = ref[r * n:(r + 1) * n, :].astype(F32)
        return tok_scr[slot]

    dils = [d for _, d in DIL_PAIRS]
    os_ = [token_order(ref, 2 * g, dils[g]) for g, ref in enumerate((d0_ref, d1_ref, d2_ref))]
    ls = [token_order(ref, 2 * g + 1, dils[g]) for g, ref in enumerate((l0_ref, l1_ref, l2_ref))]
    lm = jnp.maximum(jnp.maximum(ls[0], ls[1]), ls[2])
    es = [jnp.exp(l - lm) for l in ls]
    inv = 1.0 / (es[0] + es[1] + es[2])
    mix = jnp.concatenate([nsa_ref[...]] + [(os_[g] * (es[g] * inv)).astype(BF16) for g in range(len(DIL_PAIRS))]
                          + [hg_ref[...]], axis=1)
    out_ref[...] = h_ref[...] + _rms(_dot(mix, w_ref[...]), g_ref[...])


def _outproj(nsa, dil_o, dil_l, hg, h2, w, g, l, tm):
    T = h2.shape[0]
    row = lambda i: (i, 0)
    return pl.pallas_call(
        functools.partial(_outproj_body, tm=tm),
        grid=(T // tm,),
        in_specs=[pl.BlockSpec((tm, NSA_Q_W), row)]
        + [pl.BlockSpec((tm, DIL_Q_W), row)] * 6
        + [pl.BlockSpec((tm, HG_W), row), pl.BlockSpec((tm, D_MODEL), row),
           _layer((MIX_W, D_MODEL), l), _layer((1, D_MODEL), l)],
        out_specs=pl.BlockSpec((tm, D_MODEL), row),
        out_shape=jax.ShapeDtypeStruct((T, D_MODEL), F32),
        scratch_shapes=[pltpu.VMEM((2 * len(DIL_PAIRS), tm, DIL_Q_W), F32)],
        compiler_params=pltpu.CompilerParams(dimension_semantics=("arbitrary",),
                                             vmem_limit_bytes=VMEM_LIMIT),
        name="outproj",
    )(nsa, *dil_o, *dil_l, hg, h2, w, g)


def _mlp_body(h_ref, p_ref, gpre_ref, wup_ref, wdn_ref, gpost_ref, gple_ref, wg_ref, wp_ref, out_ref, *, fc):
    h = h_ref[...]
    hn = _rms(h, gpre_ref[...]).astype(BF16)
    acc = jnp.zeros(h.shape, F32)
    for c in range(0, D_FF, fc):
        u = jnp.maximum(_dot(hn, wup_ref[:, c:c + fc]), 0.0)
        acc = acc + _dot((u * u).astype(BF16), wdn_ref[c:c + fc, :])
    h = h + _rms(acc, gpost_ref[...])
    gate = jax.nn.sigmoid(_dot(_rms(h, gple_ref[...]).astype(BF16), wg_ref[...]))
    out_ref[...] = h + _dot(p_ref[...].astype(BF16), wp_ref[...]) * gate


def _mlp(h2, p3, gpre, wup, wdn, gpost, gple, wg, wp, l, tm):
    T = h2.shape[0]
    row = lambda i: (i, 0)
    vec = _layer((1, D_MODEL), l)

    def resident(shape):
        return pl.BlockSpec((None,) + tuple(shape), lambda i: (l, 0, 0), pipeline_mode=pl.Buffered(1))

    return pl.pallas_call(
        functools.partial(_mlp_body, fc=512),
        grid=(T // tm,),
        in_specs=[pl.BlockSpec((tm, D_MODEL), row), pl.BlockSpec((None, tm, PLE_DIM), lambda i: (l, i, 0)), vec,
                  resident(wup.shape[1:]), resident(wdn.shape[1:]), vec, vec,
                  resident(wg.shape[1:]), resident(wp.shape[1:])],
        out_specs=pl.BlockSpec((tm, D_MODEL), row),
        out_shape=jax.ShapeDtypeStruct((T, D_MODEL), F32),
        compiler_params=pltpu.CompilerParams(dimension_semantics=("arbitrary",),
                                             vmem_limit_bytes=VMEM_LIMIT),
        name="mlp_ple",
    )(h2, p3, gpre, wup, wdn, gpost, gple, wg, wp)


def _inproj_columns():
    hd_cols = np.arange(HEAD_DIM)
    ca = np.zeros((WA,), np.int64)
    for g in range(NSA_GROUP):
        for kh in range(NSA_KV_HEADS):
            o = PA_NQ + g * LANE + kh * HEAD_DIM
            ca[o:o + HEAD_DIM] = OFF_NQ + (kh * NSA_GROUP + g) * HEAD_DIM + hd_cols
    ca[PA_DQ:PA_DQ + DIL_Q_W] = OFF_DQ + np.arange(DIL_Q_W)
    for kh in range(NSA_KV_HEADS):
        for base, ko, vo in ((PA_KVS, OFF_NKS, OFF_NVS), (PA_KVW, OFF_NKW, OFF_NVW)):
            o = base + kh * LANE
            first, second = (ko, vo) if kh == 0 else (vo, ko)
            ca[o:o + HEAD_DIM] = first + kh * HEAD_DIM + hd_cols
            ca[o + HEAD_DIM:o + LANE] = second + kh * HEAD_DIM + hd_cols

    def dil_kv(dst, base, g):
        for i in range(DIL_HPG):
            hd = g * DIL_HPG + i
            o = base + i * LANE
            first, second = (OFF_DK, OFF_DV) if i == 0 else (OFF_DV, OFF_DK)
            dst[o:o + HEAD_DIM] = first + hd * HEAD_DIM + hd_cols
            dst[o + HEAD_DIM:o + LANE] = second + hd * HEAD_DIM + hd_cols

    dil_kv(ca, PA_DKV, 0)
    cd = np.zeros((len(DIL_PAIRS) - 1, WD), np.int64)
    for g in range(1, len(DIL_PAIRS)):
        dil_kv(cd[g - 1], PD_KV, g)
        cd[g - 1, PD_Q:PD_Q + DIL_Q_W] = OFF_DQ + g * DIL_Q_W + np.arange(DIL_Q_W)
    cf = np.full((WF,), IN_TOTAL, np.int64)
    for dst, src in ((PF_HQ, OFF_HQ), (PF_HF, OFF_HF), (PF_HI, OFF_HI), (PF_HG, OFF_HG)):
        cf[dst:dst + HG_W] = src + np.arange(HG_W)
    cf[PF_GATE:PF_GATE + 3 * NSA_HEADS] = OFF_GATE + np.arange(3 * NSA_HEADS)
    cc = np.concatenate([OFF_NKC + np.arange(NSA_KV_W), OFF_NVC + np.arange(NSA_KV_W)])
    return ca, cd, cf, cc


def _take(w, idx, axis, scale=None):
    idx = np.asarray(idx)
    n_src = w.shape[axis]
    bounds = [0] + [i for i in range(1, len(idx))
                    if (idx[i] != idx[i - 1] + 1 and not (idx[i] == n_src and idx[i - 1] == n_src))
                    or (scale is not None and idx[i] < n_src and idx[i - 1] < n_src and scale[idx[i]] != scale[idx[i - 1]])]
    bounds.append(len(idx))
    pieces = []
    for a, b in zip(bounds[:-1], bounds[1:]):
        src = int(idx[a])
        if src == n_src:
            shape = list(w.shape)
            shape[axis] = b - a
            pieces.append(jnp.zeros(shape, w.dtype))
            continue
        piece = lax.slice_in_dim(w, src, src + (b - a), axis=axis)
        if scale is not None and scale[src] != 1.0:
            piece = piece * float(scale[src])
        pieces.append(piece)
    return jnp.concatenate(pieces, axis=axis)


def _outproj_rows():
    hd_cols = np.arange(HEAD_DIM)
    rows = np.arange(MIX_W)
    for g in range(NSA_GROUP):
        for slot, kh in enumerate((1, 0)):
            o = g * LANE + slot * HEAD_DIM
            rows[o:o + HEAD_DIM] = (kh * NSA_GROUP + g) * HEAD_DIM + hd_cols
    for g in range(len(DIL_PAIRS)):
        for slot, i in enumerate((1, 0)):
            o = NSA_Q_W + g * DIL_Q_W + slot * HEAD_DIM
            rows[o:o + HEAD_DIM] = NSA_Q_W + (g * DIL_HPG + i) * HEAD_DIM + hd_cols
    return rows


def _compress_weights(w1):
    half = (CMP_BLOCK // 2) * HEAD_DIM
    parts = w1.reshape(DEPTH, 2, 2, CMP_STRIDE, HEAD_DIM, CMP_HIDDEN)
    out = []
    for h in range(NSA_KV_HEADS):
        pad = ((0, 0),) * 4 + ((h * HEAD_DIM, (NSA_KV_HEADS - 1 - h) * HEAD_DIM), (0, 0))
        out.append(jnp.pad(parts, pad).reshape(DEPTH, 2, 2, 2 * half, CMP_HIDDEN))
    return jnp.stack(out, axis=2).astype(BF16)


def kernel(x, p, w_in, w_out, cmp_pos, cmp_w1, cmp_w2, hg_lb, g_pre_mix, g_post_mix,
           g_pre_mlp, g_post_mlp, w_up, w_down, g_ple, w_ple_gate, w_ple_proj):
    B, S, D = x.shape
    T = B * S
    tm = 512 if S % 512 == 0 else 256
    tm_mlp = 512
    tq_nsa, tk_nsa = 256, 256
    hg_chunk = 128
    assert D == D_MODEL and S % tm == 0 and S // SLC_BLOCK <= LANE
    nsa_slopes, dil_slopes = _alibi_slopes()
    ca, cd, cf, cc = _inproj_columns()
    n_cmp = (S - CMP_BLOCK) // CMP_STRIDE + 1
    nr = S // CMP_STRIDE
    ncp = -(-nr // LANE) * LANE

    key_scale = np.ones((IN_TOTAL,), np.float32)
    key_scale[OFF_NKS:OFF_NKS + NSA_KV_W] = LOG2E
    key_scale[OFF_NKW:OFF_NKW + NSA_KV_W] = LOG2E
    wa = _take(w_in, ca, 2, scale=key_scale).astype(BF16)
    wd_in = jnp.stack([_take(w_in, cd[g], 2) for g in range(cd.shape[0])], axis=1).astype(BF16)
    wf = _take(w_in, cf, 2).astype(BF16)
    wc = _take(w_in, cc, 2).astype(BF16)
    pos8 = jnp.pad(cmp_pos.reshape(DEPTH, 2, 1, CMP_BLOCK * HEAD_DIM), ((0, 0), (0, 0), (0, 7), (0, 0))).astype(BF16)
    lo_pad, hi_pad = ((0, 0), (0, 0), (0, HEAD_DIM)), ((0, 0), (0, 0), (HEAD_DIM, 0))
    w2p = jnp.stack([jnp.stack([jnp.pad(cmp_w2[:, 0], lo_pad), jnp.pad(cmp_w2[:, 1], hi_pad)], axis=1),
                     jnp.stack([jnp.pad(cmp_w2[:, 0], hi_pad), jnp.pad(cmp_w2[:, 1], lo_pad)], axis=1)],
                    axis=1).astype(BF16)
    w1 = cmp_w1.astype(BF16)
    w1x = _compress_weights(cmp_w1)
    wo = _take(w_out, _outproj_rows(), 1).astype(BF16)
    wup, wdn = w_up.astype(BF16), w_down.astype(BF16)
    wg, wp = w_ple_gate.astype(BF16), w_ple_proj.astype(BF16)
    vec = lambda g: g.reshape(DEPTH, 1, D)
    g_pre_mix, g_post_mix, g_pre_mlp, g_post_mlp, g_ple = map(vec, (g_pre_mix, g_post_mix, g_pre_mlp, g_post_mlp, g_ple))
    p3 = p.reshape(DEPTH, T, PLE_DIM)

    h = x.reshape(T, D)
    for l in range(DEPTH):
        pa, pf, pd1, pd2, xk, xv = _inproj(h, g_pre_mix, wa, wf, wd_in, wc, l, tm)
        pa3 = pa.reshape(B, S, WA)
        pf3 = pf.reshape(B, S, WF)
        cmp_kv = _compress(xk.reshape(B, nr, CMP_ROW_W), xv.reshape(B, nr, CMP_ROW_W), pos8, w1, w1x, w2p,
                           l, ncp, n_cmp)
        o_nsa = _nsa(pa3, pf3, cmp_kv, S, tq_nsa, tk_nsa, nsa_slopes).reshape(T, NSA_Q_W)
        dil = [_dilated(pa, PA_DQ, PA_DKV, B, S, tm, 0, dil_slopes, 128),
               _dilated(pd1, PD_Q, PD_KV, B, S, tm, 1, dil_slopes, 128),
               _dilated(pd2, PD_Q, PD_KV, B, S, tm, 2, dil_slopes, 128)]
        o_hg = _hgrn2(pf3, hg_lb, l, S, hg_chunk).reshape(T, HG_W)
        h = _outproj(o_nsa, [d[0] for d in dil], [d[1] for d in dil], o_hg, h, wo, g_post_mix, l, tm)
        h = _mlp(h, p3, g_pre_mlp, wup, wdn, g_post_mlp, g_ple, wg, wp, l, tm_mlp)
    return h.reshape(B, S, D)
```

```python
import functools

import numpy as np
import jax
import jax.numpy as jnp
from jax import lax
from jax.experimental import pallas as pl
from jax.experimental.pallas import tpu as pltpu

F32 = jnp.float32
BF16 = jnp.bfloat16

D_MODEL = 1024
DEPTH = 2
HEAD_DIM = 64
NSA_HEADS = 6
NSA_KV_HEADS = 2
NSA_GROUP = NSA_HEADS // NSA_KV_HEADS
CMP_BLOCK = 32
CMP_STRIDE = 16
CMP_HIDDEN = 256
SLC_BLOCK = 64
SLC_TOPN = 8
WIN = 512
FORCE_SCORE = 1e9
DIL_PAIRS = ((128, 1), (512, 4), (2048, 16))
DIL_HPG = 2
DIL_HEADS = DIL_HPG * len(DIL_PAIRS)
HG_HEADS = 4
HG_DIM = 64
LB_TINY = 1e-30
D_FF = 4 * D_MODEL
PLE_DIM = 256
EPS = 1e-6
NEG = -1e30
LOG2E = 1.4426950408889634

NSA_Q_W = NSA_HEADS * HEAD_DIM
NSA_KV_W = NSA_KV_HEADS * HEAD_DIM
DIL_W = DIL_HEADS * HEAD_DIM
HG_W = HG_HEADS * HG_DIM
MIX_W = NSA_Q_W + DIL_W + HG_W
IN_WIDTHS = (NSA_Q_W,) + (NSA_KV_W,) * 6 + (3 * NSA_HEADS,) + (DIL_W,) * 3 + (HG_W,) * 4
IN_TOTAL = sum(IN_WIDTHS)
IN_OFF = tuple(int(v) for v in np.cumsum((0,) + IN_WIDTHS))
(OFF_NQ, OFF_NKC, OFF_NVC, OFF_NKS, OFF_NVS, OFF_NKW, OFF_NVW, OFF_GATE,
 OFF_DQ, OFF_DK, OFF_DV, OFF_HQ, OFF_HF, OFF_HI, OFF_HG) = IN_OFF[:-1]

LANE = 128
VMEM_LIMIT = 56 * 1024 * 1024

DIL_Q_W = DIL_HPG * HEAD_DIM
DIL_KV_W = DIL_HPG * LANE
PA_NQ = 0
PA_DQ = PA_NQ + NSA_Q_W
PA_KVS = PA_DQ + DIL_Q_W
PA_KVW = PA_KVS + NSA_KV_HEADS * LANE
PA_DKV = PA_KVW + NSA_KV_HEADS * LANE
WA = PA_DKV + DIL_KV_W
PD_KV, PD_Q = 0, DIL_KV_W
WD = DIL_KV_W + DIL_Q_W
PF_HQ, PF_HF, PF_HI, PF_HG, PF_GATE = 0, HG_W, 2 * HG_W, 3 * HG_W, 4 * HG_W
WF = PF_GATE + LANE
WC = 2 * NSA_KV_W
CMP_ROW_W = CMP_STRIDE * HEAD_DIM


def _dot(a, b):
    return jnp.dot(a, b, preferred_element_type=F32)


def _dot_nt(a, b):
    return lax.dot_general(a, b, (((1,), (1,)), ((), ())), preferred_element_type=F32)


def _rms(x, g):
    return x * lax.rsqrt(jnp.mean(x * x, axis=-1, keepdims=True) + EPS) * g


def _log2(n):
    l = int(n).bit_length() - 1
    assert (1 << l) == n, n
    return l


def _alibi_slopes():
    n = NSA_HEADS + DIL_HEADS
    s = 2.0 ** (-8.0 * np.arange(1, n + 1) / n)
    quads = s.reshape(-1, 4)
    nsa = [float(np.float32(v)) for v in quads[:, 2:].reshape(-1)]
    dil = [float(np.float32(v)) for v in quads[:, :2].reshape(-1)]
    return nsa, dil


def _layer(shape, l):
    zeros = (0,) * len(shape)
    return pl.BlockSpec((None,) + tuple(shape), lambda *_: (l,) + zeros)


def _inproj_body(x_ref, g_ref, wa_ref, wf_ref, wd_ref, wc_ref,
                 oa_ref, of_ref, od1_ref, od2_ref, xk0_ref, xk1_ref, xv0_ref, xv1_ref, hn_scr, *, tm):
    hn32 = _rms(x_ref[...], g_ref[...])
    hn = hn32.astype(BF16)
    for c in range(0, WA, 256):
        oa_ref[:, c:c + 256] = _dot(hn, wa_ref[:, c:c + 256]).astype(BF16)
    for c in range(0, WF, 256):
        c1 = min(c + 256, WF)
        of_ref[:, c:c1] = _dot(hn, wf_ref[:, c:c1])
    nlb = D_MODEL // LANE
    for c in range(nlb):
        hn_scr[c] = hn32[:, c * LANE:(c + 1) * LANE]

    def by_residue(d):
        parts = [jnp.concatenate([hn_scr[c, pl.ds(r, tm // d, stride=d), :] for c in range(nlb)], axis=1)
                 for r in range(d)]
        return jnp.concatenate(parts, axis=0).astype(BF16)

    for gi, o_ref in ((1, od1_ref), (2, od2_ref)):
        hp = by_residue(DIL_PAIRS[gi][1])
        for c0, c1 in ((0, DIL_KV_W), (DIL_KV_W, WD)):
            o_ref[:, c0:c1] = _dot(hp, wd_ref[gi - 1, :, c0:c1]).astype(BF16)
    assert DIL_PAIRS[2][1] == CMP_STRIDE
    cc = _dot(hp, wc_ref[...])
    nr = tm // CMP_STRIDE
    for j in range(CMP_STRIDE):
        for i, x_ref in enumerate((xk0_ref, xk1_ref, xv0_ref, xv1_ref)):
            x_ref[:, j * HEAD_DIM:(j + 1) * HEAD_DIM] = cc[j * nr:(j + 1) * nr, i * HEAD_DIM:(i + 1) * HEAD_DIM]


def _inproj(x2, g, wa, wf, wd, wc, l, tm):
    T = x2.shape[0]
    row = lambda i: (i, 0)
    nr = tm // CMP_STRIDE
    return pl.pallas_call(
        functools.partial(_inproj_body, tm=tm),
        grid=(T // tm,),
        in_specs=[pl.BlockSpec((tm, D_MODEL), row), _layer((1, D_MODEL), l),
                  _layer((D_MODEL, WA), l), _layer((D_MODEL, WF), l),
                  _layer((2, D_MODEL, WD), l), _layer((D_MODEL, WC), l)],
        out_specs=[pl.BlockSpec((tm, WA), row), pl.BlockSpec((tm, WF), row),
                   pl.BlockSpec((tm, WD), row), pl.BlockSpec((tm, WD), row),
                   ] + [pl.BlockSpec((nr, CMP_ROW_W), row)] * 4,
        out_shape=[jax.ShapeDtypeStruct((T, WA), BF16), jax.ShapeDtypeStruct((T, WF), F32),
                   jax.ShapeDtypeStruct((T, WD), BF16), jax.ShapeDtypeStruct((T, WD), BF16),
                   ] + [jax.ShapeDtypeStruct((T // CMP_STRIDE, CMP_ROW_W), F32)] * 4,
        scratch_shapes=[pltpu.VMEM((D_MODEL // LANE, tm, LANE), F32)],
        compiler_params=pltpu.CompilerParams(dimension_semantics=("arbitrary",),
                                             vmem_limit_bytes=VMEM_LIMIT),
        name="inproj",
    )(x2, g, wa, wf, wd, wc)


def _compress_body(xk0_ref, xk1_ref, xv0_ref, xv1_ref, pos_ref, w1_ref, w2_ref, out_ref, *, n_cmp):
    nr = xk0_ref.shape[1]
    half = (CMP_BLOCK // 2) * HEAD_DIM
    rows = lax.broadcasted_iota(jnp.int32, (nr, 1), 0)
    out_ref[...] = jnp.zeros(out_ref.shape, out_ref.dtype)
    srcs = ((xk0_ref, xv0_ref), (xk1_ref, xv1_ref))
    for h in range(NSA_KV_HEADS):
        acc = jnp.zeros((nr, LANE), F32)
        for ten in range(2):
            x = srcs[h][ten][0].astype(BF16)
            first = _dot(x, w1_ref[ten, 0:half, :])
            second = _dot(x, w1_ref[ten, half:2 * half, :])
            posb = _dot(pos_ref[ten], w1_ref[ten])[0:1, :]
            hid = first + pltpu.roll(second, nr - 1, 0) + posb
            act = hid * jax.nn.sigmoid(hid)
            acc = acc + _dot(act.astype(BF16), w2_ref[h, ten])
        acc = jnp.where(rows < n_cmp, acc, 0.0)
        out_ref[0, 0:nr, h * LANE:(h + 1) * LANE] = acc.astype(BF16)


def _compress(xs, pos8, w1, w2p, l, ncp, n_cmp):
    B, nr, kw = xs[0].shape
    seq = lambda b: (b, 0, 0)
    return pl.pallas_call(
        functools.partial(_compress_body, n_cmp=n_cmp),
        grid=(B,),
        in_specs=[pl.BlockSpec((1, nr, kw), seq)] * 4
        + [_layer(pos8.shape[1:], l), _layer(w1.shape[1:], l), _layer(w2p.shape[1:], l)],
        out_specs=pl.BlockSpec((1, ncp, NSA_KV_HEADS * LANE), seq),
        out_shape=jax.ShapeDtypeStruct((B, ncp, NSA_KV_HEADS * LANE), BF16),
        compiler_params=pltpu.CompilerParams(dimension_semantics=("arbitrary",),
                                             vmem_limit_bytes=VMEM_LIMIT),
        name="nsa_compress",
    )(*xs, pos8, w1, w2p)


NSA_PAT_BLK = HEAD_DIM
NSA_PAT_POS = HEAD_DIM + 32


def _nsa_pattern(seq):
    pos = np.arange(seq)
    pat = np.zeros((seq, LANE), np.float32)
    pat[pos, NSA_PAT_BLK + pos // SLC_BLOCK] = 1.0
    pat[:, NSA_PAT_POS:NSA_PAT_POS + 3] = (SLC_BLOCK * (pos // SLC_BLOCK))[:, None]
    pat[:, NSA_PAT_POS + 3:NSA_PAT_POS + 6] = (pos % SLC_BLOCK)[:, None]
    return jnp.asarray(np.stack([pat, np.roll(pat, HEAD_DIM, axis=1)]), BF16)


def _nsa_slope_rows(nsa_slopes):
    rows = np.zeros((NSA_KV_HEADS, 8, LANE), np.float32)
    for kh in range(NSA_KV_HEADS):
        for g in range(NSA_GROUP):
            rest = np.float32(nsa_slopes[kh * NSA_GROUP + g])
            for part in range(3):
                piece = np.float32(np.asarray(rest, dtype=BF16))
                rows[kh, g, NSA_PAT_POS + part] = piece
                rows[kh, g, NSA_PAT_POS + 3 + part] = piece
                rest = np.float32(rest - piece)
    rows[1] = np.roll(rows[1], HEAD_DIM, axis=1)
    return jnp.asarray(rows)


def _nsa_body(q_ref, kvs_ref, kvw_ref, kvc_ref, gate_ref, pat_ref, sl_ref, out_ref,
              ks_scr, vs_scr, kw_scr, vw_scr, s_scr, *, seq, tq, tk, n_cmp, n_slc, k_top, slopes):
    qi = pl.program_id(1)
    G = NSA_GROUP
    KH = NSA_KV_HEADS
    t0 = qi * tq
    ncp = kvc_ref.shape[1]
    ltk = _log2(tk)
    lane = lax.broadcasted_iota(jnp.int32, (1, LANE), 1)
    lo_half = lane < HEAD_DIM
    hi_half = lane >= HEAD_DIM
    mine = (lo_half, hi_half)
    other = (hi_half, lo_half)

    @pl.when(qi == 0)
    def _():
        one = jnp.ones((seq, LANE), BF16)
        for kh in range(KH):
            cols = slice(kh * LANE, (kh + 1) * LANE)
            kvs = kvs_ref[0, :, cols]
            kvw = kvw_ref[0, :, cols]
            ks_scr[kh] = jnp.where(mine[kh], kvs, pat_ref[kh])
            kw_scr[kh] = jnp.where(mine[kh], kvw, pat_ref[kh])
            vs_scr[kh] = jnp.where(mine[kh], one, kvs)
            vw_scr[kh] = jnp.where(mine[kh], one, kvw)

    ii = lax.broadcasted_iota(jnp.int32, (tq, 1), 0)
    row_t = t0 + ii
    nn = lax.broadcasted_iota(jnp.int32, (1, ncp), 1)
    maskc = ((nn * CMP_STRIDE + (CMP_BLOCK - 1)) <= row_t) & (nn < n_cmp)
    absd = jnp.abs(row_t.astype(F32) - (nn.astype(F32) * CMP_STRIDE + 0.5 * (CMP_BLOCK - 1)))
    mi = lax.broadcasted_iota(jnp.int32, (LANE, 1), 0)
    ov_t = ((nn * CMP_STRIDE < mi * SLC_BLOCK + SLC_BLOCK) & (nn * CMP_STRIDE + CMP_BLOCK > mi * SLC_BLOCK)
            & (nn < n_cmp) & (mi < n_slc))
    ov_t = jnp.where(ov_t, 1.0, 0.0).astype(BF16)
    nsp = -(-n_slc // 8) * 8
    mi_s = mi[0:nsp]
    cur_l = (t0 + lax.broadcasted_iota(jnp.int32, (1, tq), 1)) >> _log2(SLC_BLOCK)
    in_rng = mi_s < n_slc
    valid = (mi_s <= cur_l) & in_rng
    forced = (mi_s == 0) | (mi_s == cur_l) | (mi_s == cur_l - 1)

    def select(kh):
        qs = [q_ref[0, :, g * LANE:(g + 1) * LANE] * (HEAD_DIM ** -0.5) for g in range(G)]
        kvc = kvc_ref[0, :, kh * LANE:(kh + 1) * LANE]
        s_c = _dot_nt(jnp.concatenate([jnp.where(mine[kh], q, 0.0) for q in qs], axis=0), kvc)
        ps = []
        for g in range(G):
            s = jnp.where(maskc, s_c[g * tq:(g + 1) * tq] - slopes[kh * G + g] * absd, NEG)
            m = jnp.max(s, axis=-1, keepdims=True)
            e = jnp.where(maskc, jnp.exp(s - m), 0.0)
            ps.append(e / jnp.maximum(jnp.sum(e, axis=-1, keepdims=True), 1e-30))
        o_cmp = _dot(jnp.concatenate(ps, axis=0).astype(BF16), kvc)
        psum = ps[0] + ps[1] + ps[2]
        p_hi = psum.astype(BF16)
        p_lo = (psum - p_hi.astype(F32)).astype(BF16)
        imp_t = _dot_nt(ov_t, p_hi) + _dot_nt(ov_t, p_lo)
        score = jnp.where(valid, jnp.where(forced, FORCE_SCORE, imp_t[0:nsp]), -FORCE_SCORE)
        score = jnp.where(in_rng, score, -3.0 * FORCE_SCORE)
        rank = jnp.zeros((nsp, tq), F32)
        for mp in range(n_slc):
            row = score[mp:mp + 1, :]
            beats = (row > score) | ((row == score) & (mp < mi_s))
            rank = rank + jnp.where(beats, 1.0, 0.0)
        neg_t = jnp.where((rank < k_top) & valid, 0.0, NEG)
        lead = NSA_PAT_BLK if kh == 0 else 0
        pieces = [jnp.zeros((lead, tq), F32), neg_t, jnp.zeros((LANE - lead - nsp, tq), F32)]
        neg_t = jnp.concatenate([x for x in pieces if x.shape[0]], axis=0)
        return qs, o_cmp, neg_t.T

    def extended(kh, qs, neg):
        out = []
        for g in range(G):
            extra = sl_ref[kh, g:g + 1, :] + (0.0 if neg is None else neg)
            out.append(jnp.where(mine[kh], qs[g], extra.astype(BF16)))
        return jnp.concatenate(out, axis=0)

    def normalize(acc, kh):
        den = pltpu.roll(acc, HEAD_DIM, 1)
        return jnp.where(other[kh], acc / jnp.where(other[kh], den, 1.0), 0.0)

    n_kt = (t0 + tq + tk - 1) >> ltk
    nwt = min(-(-(WIN + tq) // tk), seq // tk)
    wk = nwt * tk
    kt0 = jnp.clip(n_kt - nwt, 0, seq // tk - nwt)
    k0 = pl.multiple_of(kt0 * tk, tk)
    span = pl.ds(k0, wk)
    rel = row_t - (k0 + lax.broadcasted_iota(jnp.int32, (1, wk), 1))
    ok_causal = jnp.where(rel >= 0, 0.0, NEG)
    ok_win = jnp.where(rel < WIN, ok_causal, NEG)

    def masked(s, bias):
        return (s.reshape(G, tq, wk) + bias[None]).reshape(G * tq, wk)

    o_cmp, o_win, q_sel, s_near, m_near = [], [], [], [], []
    picked = [select(kh) for kh in range(KH)]
    for kh in range(KH):
        qs, oc, neg = picked[kh]
        o_cmp.append(oc)
        q_sel.append(extended(kh, qs, neg))
        s_near.append(masked(_dot_nt(q_sel[kh], ks_scr[kh, span, :]), ok_causal))
        m_near.append(jnp.max(s_near[kh], axis=-1, keepdims=True))
    for kh in range(KH):
        s = masked(_dot_nt(extended(kh, picked[kh][0], None), kw_scr[kh, span, :]), ok_win)
        p = jnp.exp2(s - jnp.max(s, axis=-1, keepdims=True))
        o_win.append(normalize(_dot(p.astype(BF16), vw_scr[kh, span, :]), kh))

    def far_scores(kt, m_acc):
        out = []
        for kh in range(KH):
            s = _dot_nt(q_sel[kh], ks_scr[kh, pl.ds(pl.multiple_of(kt * tk, tk), tk), :])
            s_scr[kh, kt] = s
            m = m_acc[kh]
            for c in range(0, tk, LANE):
                m = jnp.maximum(m, s[:, c:c + LANE])
            out.append(m)
        return tuple(out)

    def unrolled(body, init, k):
        def run():
            carry = init
            for kt in range(k):
                carry = body(kt, carry)
            return carry
        return run

    far_counts = range(seq // tk - nwt + 1)
    m_init = tuple(jnp.full((G * tq, LANE), NEG, F32) for _ in range(KH))
    m_far = lax.switch(kt0, [unrolled(far_scores, m_init, k) for k in far_counts])
    m_row = [jnp.maximum(jnp.max(m_far[kh], axis=-1, keepdims=True), m_near[kh]) for kh in range(KH)]
    acc0 = tuple(_dot(jnp.exp2(s_near[kh] - m_row[kh]).astype(BF16), vs_scr[kh, span, :]) for kh in range(KH))

    def far_accum(kt, acc):
        out = []
        for kh in range(KH):
            p = jnp.exp2(s_scr[kh, kt] - m_row[kh])
            out.append(acc[kh] + _dot(p.astype(BF16), vs_scr[kh, pl.ds(pl.multiple_of(kt * tk, tk), tk), :]))
        return tuple(out)

    acc = lax.switch(kt0, [unrolled(far_accum, acc0, k) for k in far_counts])

    gates = jax.nn.sigmoid(gate_ref[0])
    heads = [[], []]
    for kh in range(KH):
        o_sel = normalize(acc[kh], kh)
        for g in range(G):
            hd = kh * G + g
            rows = slice(g * tq, (g + 1) * tq)
            heads[kh].append(gates[:, hd * 3:hd * 3 + 1] * o_cmp[kh][rows]
                             + gates[:, hd * 3 + 1:hd * 3 + 2] * o_sel[rows]
                             + gates[:, hd * 3 + 2:hd * 3 + 3] * o_win[kh][rows])
    for g in range(G):
        out_ref[0, :, g * LANE:(g + 1) * LANE] = jnp.where(lo_half, heads[1][g], heads[0][g]).astype(BF16)


def _nsa(pa, pf, cmp_kv, seq, tq, tk, nsa_slopes):
    B = pa.shape[0]
    n_cmp = (seq - CMP_BLOCK) // CMP_STRIDE + 1
    n_slc = seq // SLC_BLOCK
    ncp = cmp_kv.shape[1]
    kvw = NSA_KV_HEADS * LANE
    pat = _nsa_pattern(seq)
    sl_rows = _nsa_slope_rows([s * LOG2E for s in nsa_slopes])
    body = functools.partial(_nsa_body, seq=seq, tq=tq, tk=tk, n_cmp=n_cmp, n_slc=n_slc,
                             k_top=min(SLC_TOPN, n_slc), slopes=tuple(nsa_slopes))
    return pl.pallas_call(
        body,
        grid=(B, seq // tq),
        in_specs=[pl.BlockSpec((1, tq, NSA_Q_W), lambda b, i: (b, i, PA_NQ // NSA_Q_W)),
                  pl.BlockSpec((1, seq, kvw), lambda b, i: (b, 0, PA_KVS // kvw)),
                  pl.BlockSpec((1, seq, kvw), lambda b, i: (b, 0, PA_KVW // kvw)),
                  pl.BlockSpec((1, ncp, kvw), lambda b, i: (b, 0, 0)),
                  pl.BlockSpec((1, tq, LANE), lambda b, i: (b, i, PF_GATE // LANE)),
                  pl.BlockSpec((NSA_KV_HEADS, seq, LANE), lambda b, i: (0, 0, 0)),
                  pl.BlockSpec((NSA_KV_HEADS, 8, LANE), lambda b, i: (0, 0, 0))],
        out_specs=pl.BlockSpec((1, tq, NSA_Q_W), lambda b, i: (b, i, 0)),
        out_shape=jax.ShapeDtypeStruct((B, seq, NSA_Q_W), BF16),
        scratch_shapes=[pltpu.VMEM((NSA_KV_HEADS, seq, LANE), BF16)] * 4
        + [pltpu.VMEM((NSA_KV_HEADS, seq // tk, NSA_GROUP * tq, tk), F32)],
        compiler_params=pltpu.CompilerParams(dimension_semantics=("arbitrary",) * 2,
                                             vmem_limit_bytes=VMEM_LIMIT),
        name="nsa_attention",
    )(pa, pa, pa, cmp_kv, pf, pat, sl_rows)


def _dilated_body(q_ref, kv_ref, o_ref, lse_ref, *, nres, blk, ls, dil, win_keys, slopes, tq):
    ii = lax.broadcasted_iota(jnp.int32, (tq, 1), 0)
    jj = lax.broadcasted_iota(jnp.int32, (1, 2 * tq), 1)
    rel2 = ii - jj + tq
    ok2 = (rel2 >= 0) & (rel2 <= win_keys)
    lo_half = lax.broadcasted_iota(jnp.int32, (1, LANE), 1) < HEAD_DIM
    mine = (lo_half, jnp.logical_not(lo_half))
    bias2 = [(slopes[h] * float(dil)) * rel2.astype(F32) for h in range(DIL_HPG)]

    def rows(ref, r, t, lanes):
        if blk >= tq:
            a, b = divmod(t * tq, blk)
            return ref[0, a, r, b:b + tq, lanes]
        n = tq // blk
        return ref[0, t * n:(t + 1) * n, r, :, lanes].reshape(tq, LANE)

    def put(ref, r, t, val):
        if blk >= tq:
            a, b = divmod(t * tq, blk)
            ref[0, a, r, b:b + tq, :] = val
        else:
            n = tq // blk
            ref[0, t * n:(t + 1) * n, r, :, :] = val.reshape(n, blk, LANE)

    for r in range(nres):
        for t in range(ls // tq):
            q_pair = rows(q_ref, r, t, slice(0, LANE))
            outs, lses = [], []
            for h in range(DIL_HPG):
                lanes = slice(h * LANE, (h + 1) * LANE)
                qt = jnp.where(mine[h], q_pair, 0.0) * (HEAD_DIM ** -0.5)
                if t == 0:
                    kv = jnp.concatenate([rows(kv_ref, r, 0, lanes)] * 2, axis=0)
                    s = jnp.where(ok2 & (jj >= tq), _dot_nt(qt, kv) - bias2[h], NEG)
                else:
                    kv = jnp.concatenate([rows(kv_ref, r, t - 1, lanes), rows(kv_ref, r, t, lanes)], axis=0)
                    s = jnp.where(ok2, _dot_nt(qt, kv) - bias2[h], NEG)
                m = jnp.max(s, axis=-1, keepdims=True)
                e = jnp.exp(s - m)
                l = jnp.sum(e, axis=-1, keepdims=True)
                outs.append(_dot(e.astype(BF16), kv) / l)
                lses.append(jnp.broadcast_to(m + jnp.log(l), (tq, LANE)))
            put(o_ref, r, t, jnp.where(lo_half, outs[1], outs[0]).astype(BF16))
            put(lse_ref, r, t, jnp.where(lo_half, lses[1], lses[0]))


def _dilated(src, qcol, kvcol, batch, seq, tm, group, slopes, tq):
    win, dil = DIL_PAIRS[group]
    ls = seq // dil
    tq = min(tq, ls)
    if dil == 1:
        ntile, blk = 1, seq
    else:
        ntile, blk = seq // tm, tm // dil
    nres = max(1, min(dil, 16 * tq // ls))
    view = src.reshape(batch, ntile, dil, blk, src.shape[1])
    body = functools.partial(_dilated_body, nres=nres, blk=blk, ls=ls, dil=dil, win_keys=win // dil,
                             slopes=tuple(slopes[group * DIL_HPG:(group + 1) * DIL_HPG]), tq=tq)
    shp = (1, ntile, nres, blk)
    o, lse = pl.pallas_call(
        body,
        grid=(batch, dil // nres),
        in_specs=[pl.BlockSpec(shp + (DIL_Q_W,), lambda b, r: (b, 0, r, 0, qcol // DIL_Q_W)),
                  pl.BlockSpec(shp + (DIL_KV_W,), lambda b, r: (b, 0, r, 0, kvcol // DIL_KV_W))],
        out_specs=[pl.BlockSpec(shp + (DIL_Q_W,), lambda b, r: (b, 0, r, 0, 0))] * 2,
        out_shape=[jax.ShapeDtypeStruct((batch, ntile, dil, blk, DIL_Q_W), BF16),
                   jax.ShapeDtypeStruct((batch, ntile, dil, blk, DIL_Q_W), F32)],
        compiler_params=pltpu.CompilerParams(dimension_semantics=("arbitrary",) * 2,
                                             vmem_limit_bytes=VMEM_LIMIT),
        name=f"dilated_attention_g{group}",
    )(view, view)
    return o.reshape(batch * seq, DIL_Q_W), lse.reshape(batch * seq, DIL_Q_W)


def _hgrn2_body(q_ref, f_ref, i_ref, g_ref, lb_ref, o_ref, *, layer, seq, chunk):
    c = chunk
    sub = 8
    npair = HG_W // LANE
    lbs = lb_ref[...].astype(F32)
    mx = jnp.max(lbs, axis=0, keepdims=True)
    ex = jnp.exp(lbs - mx)
    sm = ex / jnp.sum(ex, axis=0, keepdims=True)
    lower = jnp.maximum(jnp.sum(sm[0:layer + 1], axis=0, keepdims=True) - sm[0:1], 0.0)
    log_lb = jnp.log(lower + LB_TINY)
    log_1m = jnp.log1p(-lower)

    lane = lax.broadcasted_iota(jnp.int32, (1, LANE), 1)
    head0 = lane < HG_DIM
    ri = lax.broadcasted_iota(jnp.int32, (c, 1), 0)
    ci = lax.broadcasted_iota(jnp.int32, (1, c), 1)
    tri = jnp.where(ci <= ri, 1.0, 0.0).astype(BF16)
    di = lax.broadcasted_iota(jnp.int32, (LANE, 1), 0)
    same_head = (di >= HG_DIM) == (lane >= HG_DIM)
    ones_blk = jnp.where(same_head, 1.0, 0.0).astype(BF16)
    gcol = lax.broadcasted_iota(jnp.int32, (1, sub * c), 1)
    gsum = jnp.where(((gcol >> _log2(c)) == (ri & (sub - 1)))
                     & (((gcol & (c - 1)) >> 3) == (ri >> 3)), 1.0, 0.0).astype(BF16)
    sp = lax.broadcasted_iota(jnp.int32, (1, sub, 1), 1)
    levels = []
    w = sub
    while w < c:
        same = (ri >> _log2(2 * w)) == (ci >> _log2(2 * w))
        levels.append((w, same & ((ri & (2 * w - 1)) >= w) & ((ci & (2 * w - 1)) < w)))
        w *= 2

    def split3(x):
        hi = x.astype(BF16)
        r1 = x - hi.astype(F32)
        mid = r1.astype(BF16)
        lo = (r1 - mid.astype(F32)).astype(BF16)
        return hi, mid, lo

    def pair_chunk(q, v, x, gt, llb, l1m, state_t):
        log_sig = jnp.minimum(x, 0.0) - jnp.log1p(jnp.exp(-jnp.abs(x)))
        t2 = l1m + log_sig
        lf = jnp.maximum(llb, t2) + jnp.log1p(jnp.exp(-jnp.abs(llb - t2)))
        kk = 1.0 - jnp.exp(lf)
        hi, mid, lo = split3(lf * LOG2E)
        b = _dot(tri, hi) + _dot(tri, mid) + _dot(tri, lo)
        vb = v.astype(BF16)

        q3 = q.reshape(c // sub, sub, LANE)
        k3 = kk.reshape(c // sub, sub, LANE)
        b3 = b.reshape(c // sub, sub, LANE)
        parts = []
        for tp in range(sub):
            dec = jnp.exp2(jnp.minimum(b3[:, tp:tp + 1, :] - b3, 0.0))
            parts.append(jnp.where(sp <= tp, q3[:, tp:tp + 1, :] * k3 * dec, 0.0).reshape(c, LANE))
        wall = jnp.concatenate(parts, axis=0)
        a_rep = _dot(wall.astype(BF16), ones_blk)
        z = a_rep * jnp.concatenate([v] * sub, axis=0)
        o = _dot(gsum, z.astype(BF16))

        a0 = jnp.zeros((c, c), F32)
        a1 = jnp.zeros((c, c), F32)
        for w, lmask in levels:
            b_r = b.reshape(c // (2 * w), 2 * w, LANE)
            bnd = jnp.broadcast_to(b_r[:, w - 1:w, :], b_r.shape).reshape(c, LANE)
            qe = q * jnp.exp2(jnp.minimum(b - bnd, 0.0))
            ke = (kk * jnp.exp2(jnp.minimum(bnd - b, 0.0))).astype(BF16)
            q2 = jnp.concatenate([jnp.where(head0, qe, 0.0), jnp.where(head0, 0.0, qe)], axis=0).astype(BF16)
            a01 = _dot_nt(q2, ke)
            a0 = a0 + jnp.where(lmask, a01[0:c], 0.0)
            a1 = a1 + jnp.where(lmask, a01[c:2 * c], 0.0)
        av = _dot(jnp.concatenate([a0, a1], axis=0).astype(BF16), vb)
        o = o + jnp.where(head0, av[0:c], av[c:2 * c])

        o = o + _dot_nt((q * jnp.exp2(b)).astype(BF16), state_t.astype(BF16))
        b_last = b[c - 1:c, :]
        khat = (kk * jnp.exp2(b_last - b)).astype(BF16)
        upd = lax.dot_general(vb, khat, (((0,), (0,)), ((), ())), preferred_element_type=F32)
        state_t = jnp.exp2(b_last) * state_t + jnp.where(same_head, upd, 0.0)

        o2 = o * o
        ms0 = jnp.sum(jnp.where(head0, o2, 0.0), axis=-1, keepdims=True)
        ms1 = jnp.sum(jnp.where(head0, 0.0, o2), axis=-1, keepdims=True)
        o = o * lax.rsqrt(jnp.where(head0, ms0, ms1) * (1.0 / HG_DIM) + EPS)
        return (o * (gt * jax.nn.sigmoid(gt))).astype(BF16), state_t

    def step(ic, states):
        rows = pl.ds(pl.multiple_of(ic * c, c), c)
        out = []
        for p in range(npair):
            cols = slice(p * LANE, (p + 1) * LANE)
            o, st = pair_chunk(q_ref[0, rows, cols], i_ref[0, rows, cols], f_ref[0, rows, cols],
                               g_ref[0, rows, cols], log_lb[:, cols], log_1m[:, cols], states[p])
            o_ref[0, rows, cols] = o
            out.append(st)
        return tuple(out)

    lax.fori_loop(0, seq // c, step, tuple(jnp.zeros((LANE, LANE), F32) for _ in range(npair)),
                  unroll=min(8, seq // c))


def _hgrn2(pf, hg_lb, layer, seq, chunk):
    B = pf.shape[0]
    sec = lambda off: (lambda b: (b, 0, off // HG_W))
    return pl.pallas_call(
        functools.partial(_hgrn2_body, layer=layer, seq=seq, chunk=chunk),
        grid=(B,),
        in_specs=[pl.BlockSpec((1, seq, HG_W), sec(PF_HQ)),
                  pl.BlockSpec((1, seq, HG_W), sec(PF_HF)),
                  pl.BlockSpec((1, seq, HG_W), sec(PF_HI)),
                  pl.BlockSpec((1, seq, HG_W), sec(PF_HG)),
                  pl.BlockSpec((DEPTH, HG_W), lambda b: (0, 0))],
        out_specs=pl.BlockSpec((1, seq, HG_W), lambda b: (b, 0, 0)),
        out_shape=jax.ShapeDtypeStruct((B, seq, HG_W), BF16),
        compiler_params=pltpu.CompilerParams(dimension_semantics=("arbitrary",),
                                             vmem_limit_bytes=VMEM_LIMIT),
        name="hgrn2",
    )(pf, pf, pf, pf, hg_lb)


def _outproj_body(nsa_ref, d0_ref, d1_ref, d2_ref, l0_ref, l1_ref, l2_ref, hg_ref, h_ref,
                  w_ref, g_ref, out_ref, tok_scr, *, tm):
    def token_order(ref, slot, d):
        if d == 1:
            return ref[...].astype(F32)
        n = tm // d
        for r in range(d):
            tok_scr[slot, pl.ds(r, n, stride=d), :] = ref[r * n:(r + 1) * n, :].astype(F32)
        return tok_scr[slot]

    dils = [d for _, d in DIL_PAIRS]
    os_ = [token_order(ref, 2 * g, dils[g]) for g, ref in enumerate((d0_ref, d1_ref, d2_ref))]
    ls = [token_order(ref, 2 * g + 1, dils[g]) for g, ref in enumerate((l0_ref, l1_ref, l2_ref))]
    lm = jnp.maximum(jnp.maximum(ls[0], ls[1]), ls[2])
    es = [jnp.exp(l - lm) for l in ls]
    inv = 1.0 / (es[0] + es[1] + es[2])
    mix = jnp.concatenate([nsa_ref[...]] + [(os_[g] * (es[g] * inv)).astype(BF16) for g in range(len(DIL_PAIRS))]
                          + [hg_ref[...]], axis=1)
    out_ref[...] = h_ref[...] + _rms(_dot(mix, w_ref[...]), g_ref[...])


def _outproj(nsa, dil_o, dil_l, hg, h2, w, g, l, tm):
    T = h2.shape[0]
    row = lambda i: (i, 0)
    return pl.pallas_call(
        functools.partial(_outproj_body, tm=tm),
        grid=(T // tm,),
        in_specs=[pl.BlockSpec((tm, NSA_Q_W), row)]
        + [pl.BlockSpec((tm, DIL_Q_W), row)] * 6
        + [pl.BlockSpec((tm, HG_W), row), pl.BlockSpec((tm, D_MODEL), row),
           _layer((MIX_W, D_MODEL), l), _layer((1, D_MODEL), l)],
        out_specs=pl.BlockSpec((tm, D_MODEL), row),
        out_shape=jax.ShapeDtypeStruct((T, D_MODEL), F32),
        scratch_shapes=[pltpu.VMEM((2 * len(DIL_PAIRS), tm, DIL_Q_W), F32)],
        compiler_params=pltpu.CompilerParams(dimension_semantics=("arbitrary",),
                                             vmem_limit_bytes=VMEM_LIMIT),
        name="outproj",
    )(nsa, *dil_o, *dil_l, hg, h2, w, g)


def _mlp_body(h_ref, p_ref, gpre_ref, wup_ref, wdn_ref, gpost_ref, gple_ref, wg_ref, wp_ref, out_ref, *, fc):
    h = h_ref[...]
    hn = _rms(h, gpre_ref[...]).astype(BF16)
    acc = jnp.zeros(h.shape, F32)
    for c in range(0, D_FF, fc):
        u = jnp.maximum(_dot(hn, wup_ref[:, c:c + fc]), 0.0)
        acc = acc + _dot((u * u).astype(BF16), wdn_ref[c:c + fc, :])
    h = h + _rms(acc, gpost_ref[...])
    gate = jax.nn.sigmoid(_dot(_rms(h, gple_ref[...]).astype(BF16), wg_ref[...]))
    out_ref[...] = h + _dot(p_ref[...].astype(BF16), wp_ref[...]) * gate


def _mlp(h2, p3, gpre, wup, wdn, gpost, gple, wg, wp, l, tm):
    T = h2.shape[0]
    row = lambda i: (i, 0)
    vec = _layer((1, D_MODEL), l)

    def resident(shape):
        return pl.BlockSpec((None,) + tuple(shape), lambda i: (l, 0, 0), pipeline_mode=pl.Buffered(1))

    return pl.pallas_call(
        functools.partial(_mlp_body, fc=512),
        grid=(T // tm,),
        in_specs=[pl.BlockSpec((tm, D_MODEL), row), pl.BlockSpec((None, tm, PLE_DIM), lambda i: (l, i, 0)), vec,
                  resident(wup.shape[1:]), resident(wdn.shape[1:]), vec, vec,
                  resident(wg.shape[1:]), resident(wp.shape[1:])],
        out_specs=pl.BlockSpec((tm, D_MODEL), row),
        out_shape=jax.ShapeDtypeStruct((T, D_MODEL), F32),
        compiler_params=pltpu.CompilerParams(dimension_semantics=("arbitrary",),
                                             vmem_limit_bytes=VMEM_LIMIT),
        name="mlp_ple",
    )(h2, p3, gpre, wup, wdn, gpost, gple, wg, wp)


def _inproj_columns():
    hd_cols = np.arange(HEAD_DIM)
    ca = np.zeros((WA,), np.int64)
    for g in range(NSA_GROUP):
        for kh in range(NSA_KV_HEADS):
            o = PA_NQ + g * LANE + kh * HEAD_DIM
            ca[o:o + HEAD_DIM] = OFF_NQ + (kh * NSA_GROUP + g) * HEAD_DIM + hd_cols
    ca[PA_DQ:PA_DQ + DIL_Q_W] = OFF_DQ + np.arange(DIL_Q_W)
    for kh in range(NSA_KV_HEADS):
        for base, ko, vo in ((PA_KVS, OFF_NKS, OFF_NVS), (PA_KVW, OFF_NKW, OFF_NVW)):
            o = base + kh * LANE
            first, second = (ko, vo) if kh == 0 else (vo, ko)
            ca[o:o + HEAD_DIM] = first + kh * HEAD_DIM + hd_cols
            ca[o + HEAD_DIM:o + LANE] = second + kh * HEAD_DIM + hd_cols

    def dil_kv(dst, base, g):
        for i in range(DIL_HPG):
            hd = g * DIL_HPG + i
            o = base + i * LANE
            first, second = (OFF_DK, OFF_DV) if i == 0 else (OFF_DV, OFF_DK)
            dst[o:o + HEAD_DIM] = first + hd * HEAD_DIM + hd_cols
            dst[o + HEAD_DIM:o + LANE] = second + hd * HEAD_DIM + hd_cols

    dil_kv(ca, PA_DKV, 0)
    cd = np.zeros((len(DIL_PAIRS) - 1, WD), np.int64)
    for g in range(1, len(DIL_PAIRS)):
        dil_kv(cd[g - 1], PD_KV, g)
        cd[g - 1, PD_Q:PD_Q + DIL_Q_W] = OFF_DQ + g * DIL_Q_W + np.arange(DIL_Q_W)
    cf = np.full((WF,), IN_TOTAL, np.int64)
    for dst, src in ((PF_HQ, OFF_HQ), (PF_HF, OFF_HF), (PF_HI, OFF_HI), (PF_HG, OFF_HG)):
        cf[dst:dst + HG_W] = src + np.arange(HG_W)
    cf[PF_GATE:PF_GATE + 3 * NSA_HEADS] = OFF_GATE + np.arange(3 * NSA_HEADS)
    cc = np.concatenate([OFF_NKC + np.arange(NSA_KV_W), OFF_NVC + np.arange(NSA_KV_W)])
    return ca, cd, cf, cc


def _take(w, idx, axis, scale=None):
    idx = np.asarray(idx)
    n_src = w.shape[axis]
    bounds = [0] + [i for i in range(1, len(idx))
                    if (idx[i] != idx[i - 1] + 1 and not (idx[i] == n_src and idx[i - 1] == n_src))
                    or (scale is not None and idx[i] < n_src and idx[i - 1] < n_src and scale[idx[i]] != scale[idx[i - 1]])]
    bounds.append(len(idx))
    pieces = []
    for a, b in zip(bounds[:-1], bounds[1:]):
        src = int(idx[a])
        if src == n_src:
            shape = list(w.shape)
            shape[axis] = b - a
            pieces.append(jnp.zeros(shape, w.dtype))
            continue
        piece = lax.slice_in_dim(w, src, src + (b - a), axis=axis)
        if scale is not None and scale[src] != 1.0:
            piece = piece * float(scale[src])
        pieces.append(piece)
    return jnp.concatenate(pieces, axis=axis)


def _outproj_rows():
    hd_cols = np.arange(HEAD_DIM)
    rows = np.arange(MIX_W)
    for g in range(NSA_GROUP):
        for slot, kh in enumerate((1, 0)):
            o = g * LANE + slot * HEAD_DIM
            rows[o:o + HEAD_DIM] = (kh * NSA_GROUP + g) * HEAD_DIM + hd_cols
    for g in range(len(DIL_PAIRS)):
        for slot, i in enumerate((1, 0)):
            o = NSA_Q_W + g * DIL_Q_W + slot * HEAD_DIM
            rows[o:o + HEAD_DIM] = NSA_Q_W + (g * DIL_HPG + i) * HEAD_DIM + hd_cols
    return rows


def kernel(x, p, w_in, w_out, cmp_pos, cmp_w1, cmp_w2, hg_lb, g_pre_mix, g_post_mix,
           g_pre_mlp, g_post_mlp, w_up, w_down, g_ple, w_ple_gate, w_ple_proj):
    B, S, D = x.shape
    T = B * S
    tm = 512 if S % 512 == 0 else 256
    tm_mlp = 512
    tq_nsa, tk_nsa = 256, 256
    hg_chunk = 128
    assert D == D_MODEL and S % tm == 0 and S // SLC_BLOCK <= LANE
    nsa_slopes, dil_slopes = _alibi_slopes()
    ca, cd, cf, cc = _inproj_columns()
    n_cmp = (S - CMP_BLOCK) // CMP_STRIDE + 1
    nr = S // CMP_STRIDE
    ncp = -(-nr // LANE) * LANE

    key_scale = np.ones((IN_TOTAL,), np.float32)
    key_scale[OFF_NKS:OFF_NKS + NSA_KV_W] = LOG2E
    key_scale[OFF_NKW:OFF_NKW + NSA_KV_W] = LOG2E
    wa = _take(w_in, ca, 2, scale=key_scale).astype(BF16)
    wd_in = jnp.stack([_take(w_in, cd[g], 2) for g in range(cd.shape[0])], axis=1).astype(BF16)
    wf = _take(w_in, cf, 2).astype(BF16)
    wc = _take(w_in, cc, 2).astype(BF16)
    pos8 = jnp.pad(cmp_pos.reshape(DEPTH, 2, 1, CMP_BLOCK * HEAD_DIM), ((0, 0), (0, 0), (0, 7), (0, 0))).astype(BF16)
    lo_pad, hi_pad = ((0, 0), (0, 0), (0, HEAD_DIM)), ((0, 0), (0, 0), (HEAD_DIM, 0))
    w2p = jnp.stack([jnp.stack([jnp.pad(cmp_w2[:, 0], lo_pad), jnp.pad(cmp_w2[:, 1], hi_pad)], axis=1),
                     jnp.stack([jnp.pad(cmp_w2[:, 0], hi_pad), jnp.pad(cmp_w2[:, 1], lo_pad)], axis=1)],
                    axis=1).astype(BF16)
    w1 = cmp_w1.astype(BF16)
    wo =_take(w_out, _outproj_rows(), 1).astype(BF16)
    wup, wdn = w_up.astype(BF16), w_down.astype(BF16)
    wg, wp = w_ple_gate.astype(BF16), w_ple_proj.astype(BF16)
    vec = lambda g: g.reshape(DEPTH, 1, D)
    g_pre_mix, g_post_mix, g_pre_mlp, g_post_mlp, g_ple = map(vec, (g_pre_mix, g_post_mix, g_pre_mlp, g_post_mlp, g_ple))
    p3 = p.reshape(DEPTH, T, PLE_DIM)

    h = x.reshape(T, D)
    for l in range(DEPTH):
        pa, pf, pd1, pd2, *xs = _inproj(h, g_pre_mix, wa, wf, wd_in, wc, l, tm)
        pa3 = pa.reshape(B, S, WA)
        pf3 = pf.reshape(B, S, WF)
        cmp_kv = _compress([a.reshape(B, nr, CMP_ROW_W) for a in xs], pos8, w1, w2p, l, ncp, n_cmp)
        o_nsa = _nsa(pa3, pf3, cmp_kv, S, tq_nsa, tk_nsa, nsa_slopes).reshape(T, NSA_Q_W)
        dil = [_dilated(pa, PA_DQ, PA_DKV, B, S, tm, 0, dil_slopes, 128),
               _dilated(pd1, PD_Q, PD_KV, B, S, tm, 1, dil_slopes, 128),
               _dilated(pd2, PD_Q, PD_KV, B, S, tm, 2, dil_slopes, 128)]
        o_hg = _hgrn2(pf3, hg_lb, l, S, hg_chunk).reshape(T, HG_W)
        h = _outproj(o_nsa, [d[0] for d in dil], [d[1] for d in dil], o_hg, h, wo, g_post_mix, l, tm)
        h = _mlp(h, p3, g_pre_mlp, wup, wdn, g_post_mlp, g_ple, wg, wp, l, tm_mlp)
    return h.reshape(B, S, D)
```

```python
import functools

import numpy as np
import jax
import jax.numpy as jnp
from jax import lax
from jax.experimental import pallas as pl
from jax.experimental.pallas import tpu as pltpu

F32 = jnp.float32
BF16 = jnp.bfloat16

D_MODEL = 1024
DEPTH = 2
HEAD_DIM = 64
NSA_HEADS = 6
NSA_KV_HEADS = 2
NSA_GROUP = NSA_HEADS // NSA_KV_HEADS
CMP_BLOCK = 32
CMP_STRIDE = 16
CMP_HIDDEN = 256
SLC_BLOCK = 64
SLC_TOPN = 8
WIN = 512
FORCE_SCORE = 1e9
DIL_PAIRS = ((128, 1), (512, 4), (2048, 16))
DIL_HPG = 2
DIL_HEADS = DIL_HPG * len(DIL_PAIRS)
HG_HEADS = 4
HG_DIM = 64
LB_TINY = 1e-30
D_FF = 4 * D_MODEL
PLE_DIM = 256
EPS = 1e-6
NEG = -1e30
LOG2E = 1.4426950408889634

NSA_Q_W = NSA_HEADS * HEAD_DIM
NSA_KV_W = NSA_KV_HEADS * HEAD_DIM
DIL_W = DIL_HEADS * HEAD_DIM
HG_W = HG_HEADS * HG_DIM
MIX_W = NSA_Q_W + DIL_W + HG_W
IN_WIDTHS = (NSA_Q_W,) + (NSA_KV_W,) * 6 + (3 * NSA_HEADS,) + (DIL_W,) * 3 + (HG_W,) * 4
IN_TOTAL = sum(IN_WIDTHS)
IN_OFF = tuple(int(v) for v in np.cumsum((0,) + IN_WIDTHS))
(OFF_NQ, OFF_NKC, OFF_NVC, OFF_NKS, OFF_NVS, OFF_NKW, OFF_NVW, OFF_GATE,
 OFF_DQ, OFF_DK, OFF_DV, OFF_HQ, OFF_HF, OFF_HI, OFF_HG) = IN_OFF[:-1]

LANE = 128
VMEM_LIMIT = 56 * 1024 * 1024

DIL_Q_W = DIL_HPG * HEAD_DIM
DIL_KV_W = DIL_HPG * LANE
PA_NQ = 0
PA_DQ = PA_NQ + NSA_Q_W
PA_KVS = PA_DQ + DIL_Q_W
PA_KVW = PA_KVS + NSA_KV_HEADS * LANE
PA_DKV = PA_KVW + NSA_KV_HEADS * LANE
WA = PA_DKV + DIL_KV_W
PD_KV, PD_Q = 0, DIL_KV_W
WD = DIL_KV_W + DIL_Q_W
PF_HQ, PF_HF, PF_HI, PF_HG, PF_GATE = 0, HG_W, 2 * HG_W, 3 * HG_W, 4 * HG_W
WF = PF_GATE + LANE
WC = 2 * NSA_KV_W
CMP_ROW_W = CMP_STRIDE * HEAD_DIM


def _dot(a, b):
    return jnp.dot(a, b, preferred_element_type=F32)


def _dot_nt(a, b):
    return lax.dot_general(a, b, (((1,), (1,)), ((), ())), preferred_element_type=F32)


def _rms(x, g):
    return x * lax.rsqrt(jnp.mean(x * x, axis=-1, keepdims=True) + EPS) * g


def _log2(n):
    l = int(n).bit_length() - 1
    assert (1 << l) == n, n
    return l


def _alibi_slopes():
    n = NSA_HEADS + DIL_HEADS
    s = 2.0 ** (-8.0 * np.arange(1, n + 1) / n)
    quads = s.reshape(-1, 4)
    nsa = [float(np.float32(v)) for v in quads[:, 2:].reshape(-1)]
    dil = [float(np.float32(v)) for v in quads[:, :2].reshape(-1)]
    return nsa, dil


def _layer(shape, l):
    zeros = (0,) * len(shape)
    return pl.BlockSpec((None,) + tuple(shape), lambda *_: (l,) + zeros)


def _inproj_body(x_ref, g_ref, wa_ref, wf_ref, wd_ref, wc_ref,
                 oa_ref, of_ref, od1_ref, od2_ref, xk0_ref, xk1_ref, xv0_ref, xv1_ref, hn_scr, *, tm):
    hn32 = _rms(x_ref[...], g_ref[...])
    hn = hn32.astype(BF16)
    for c in range(0, WA, 256):
        oa_ref[:, c:c + 256] = _dot(hn, wa_ref[:, c:c + 256]).astype(BF16)
    for c in range(0, WF, 256):
        c1 = min(c + 256, WF)
        of_ref[:, c:c1] = _dot(hn, wf_ref[:, c:c1])
    nlb = D_MODEL // LANE
    for c in range(nlb):
        hn_scr[c] = hn32[:, c * LANE:(c + 1) * LANE]

    def by_residue(d):
        parts = [jnp.concatenate([hn_scr[c, pl.ds(r, tm // d, stride=d), :] for c in range(nlb)], axis=1)
                 for r in range(d)]
        return jnp.concatenate(parts, axis=0).astype(BF16)

    for gi, o_ref in ((1, od1_ref), (2, od2_ref)):
        hp = by_residue(DIL_PAIRS[gi][1])
        for c0, c1 in ((0, DIL_KV_W), (DIL_KV_W, WD)):
            o_ref[:, c0:c1] = _dot(hp, wd_ref[gi - 1, :, c0:c1]).astype(BF16)
    assert DIL_PAIRS[2][1] == CMP_STRIDE
    cc = _dot(hp, wc_ref[...])
    nr = tm // CMP_STRIDE
    for j in range(CMP_STRIDE):
        for i, x_ref in enumerate((xk0_ref, xk1_ref, xv0_ref, xv1_ref)):
            x_ref[:, j * HEAD_DIM:(j + 1) * HEAD_DIM] = cc[j * nr:(j + 1) * nr, i * HEAD_DIM:(i + 1) * HEAD_DIM]


def _inproj(x2, g, wa, wf, wd, wc, l, tm):
    T = x2.shape[0]
    row = lambda i: (i, 0)
    nr = tm // CMP_STRIDE
    return pl.pallas_call(
        functools.partial(_inproj_body, tm=tm),
        grid=(T // tm,),
        in_specs=[pl.BlockSpec((tm, D_MODEL), row), _layer((1, D_MODEL), l),
                  _layer((D_MODEL, WA), l), _layer((D_MODEL, WF), l),
                  _layer((2, D_MODEL, WD), l), _layer((D_MODEL, WC), l)],
        out_specs=[pl.BlockSpec((tm, WA), row), pl.BlockSpec((tm, WF), row),
                   pl.BlockSpec((tm, WD), row), pl.BlockSpec((tm, WD), row),
                   ] + [pl.BlockSpec((nr, CMP_ROW_W), row)] * 4,
        out_shape=[jax.ShapeDtypeStruct((T, WA), BF16), jax.ShapeDtypeStruct((T, WF), F32),
                   jax.ShapeDtypeStruct((T, WD), BF16), jax.ShapeDtypeStruct((T, WD), BF16),
                   ] + [jax.ShapeDtypeStruct((T // CMP_STRIDE, CMP_ROW_W), F32)] * 4,
        scratch_shapes=[pltpu.VMEM((D_MODEL // LANE, tm, LANE), F32)],
        compiler_params=pltpu.CompilerParams(dimension_semantics=("arbitrary",),
                                             vmem_limit_bytes=VMEM_LIMIT),
        name="inproj",
    )(x2, g, wa, wf, wd, wc)


def _compress_body(xk0_ref, xk1_ref, xv0_ref, xv1_ref, pos_ref, w1_ref, w2_ref, out_ref, *, n_cmp):
    nr = xk0_ref.shape[1]
    half = (CMP_BLOCK // 2) * HEAD_DIM
    rows = lax.broadcasted_iota(jnp.int32, (nr, 1), 0)
    out_ref[...] = jnp.zeros(out_ref.shape, out_ref.dtype)
    srcs = ((xk0_ref, xv0_ref), (xk1_ref, xv1_ref))
    for h in range(NSA_KV_HEADS):
        acc = jnp.zeros((nr, LANE), F32)
        for ten in range(2):
            x = srcs[h][ten][0].astype(BF16)
            first = _dot(x, w1_ref[ten, 0:half, :])
            second = _dot(x, w1_ref[ten, half:2 * half, :])
            posb = _dot(pos_ref[ten], w1_ref[ten])[0:1, :]
            hid = first + pltpu.roll(second, nr - 1, 0) + posb
            act = hid * jax.nn.sigmoid(hid)
            acc = acc + _dot(act.astype(BF16), w2_ref[h, ten])
        acc = jnp.where(rows < n_cmp, acc, 0.0)
        out_ref[0, 0:nr, h * LANE:(h + 1) * LANE] = acc.astype(BF16)


def _compress(xs, pos8, w1, w2p, l, ncp, n_cmp):
    B, nr, kw = xs[0].shape
    seq = lambda b: (b, 0, 0)
    return pl.pallas_call(
        functools.partial(_compress_body, n_cmp=n_cmp),
        grid=(B,),
        in_specs=[pl.BlockSpec((1, nr, kw), seq)] * 4
        + [_layer(pos8.shape[1:], l), _layer(w1.shape[1:], l), _layer(w2p.shape[1:], l)],
        out_specs=pl.BlockSpec((1, ncp, NSA_KV_HEADS * LANE), seq),
        out_shape=jax.ShapeDtypeStruct((B, ncp, NSA_KV_HEADS * LANE), BF16),
        compiler_params=pltpu.CompilerParams(dimension_semantics=("arbitrary",),
                                             vmem_limit_bytes=VMEM_LIMIT),
        name="nsa_compress",
    )(*xs, pos8, w1, w2p)


NSA_PAT_BLK = HEAD_DIM
NSA_PAT_POS = HEAD_DIM + 32


def _nsa_pattern(seq):
    pos = np.arange(seq)
    pat = np.zeros((seq, LANE), np.float32)
    pat[pos, NSA_PAT_BLK + pos // SLC_BLOCK] = 1.0
    pat[:, NSA_PAT_POS:NSA_PAT_POS + 3] = (SLC_BLOCK * (pos // SLC_BLOCK))[:, None]
    pat[:, NSA_PAT_POS + 3:NSA_PAT_POS + 6] = (pos % SLC_BLOCK)[:, None]
    return jnp.asarray(np.stack([pat, np.roll(pat, HEAD_DIM, axis=1)]), BF16)


def _nsa_slope_rows(nsa_slopes):
    rows = np.zeros((NSA_KV_HEADS, 8, LANE), np.float32)
    for kh in range(NSA_KV_HEADS):
        for g in range(NSA_GROUP):
            rest = np.float32(nsa_slopes[kh * NSA_GROUP + g])
            for part in range(3):
                piece = np.float32(np.asarray(rest, dtype=BF16))
                rows[kh, g, NSA_PAT_POS + part] = piece
                rows[kh, g, NSA_PAT_POS + 3 + part] = piece
                rest = np.float32(rest - piece)
    rows[1] = np.roll(rows[1], HEAD_DIM, axis=1)
    return jnp.asarray(rows)


def _nsa_body(q_ref, kvs_ref, kvw_ref, kvc_ref, gate_ref, pat_ref, sl_ref, out_ref,
              ks_scr, vs_scr, kw_scr, vw_scr, s_scr, *, seq, tq, tk, n_cmp, n_slc, k_top, slopes):
    qi = pl.program_id(1)
    G = NSA_GROUP
    KH = NSA_KV_HEADS
    t0 = qi * tq
    ncp = kvc_ref.shape[1]
    ltk = _log2(tk)
    lane = lax.broadcasted_iota(jnp.int32, (1, LANE), 1)
    lo_half = lane < HEAD_DIM
    hi_half = lane >= HEAD_DIM
    mine = (lo_half, hi_half)
    other = (hi_half, lo_half)

    @pl.when(qi == 0)
    def _():
        one = jnp.ones((seq, LANE), BF16)
        for kh in range(KH):
            cols = slice(kh * LANE, (kh + 1) * LANE)
            kvs = kvs_ref[0, :, cols]
            kvw = kvw_ref[0, :, cols]
            ks_scr[kh] = jnp.where(mine[kh], kvs, pat_ref[kh])
            kw_scr[kh] = jnp.where(mine[kh], kvw, pat_ref[kh])
            vs_scr[kh] = jnp.where(mine[kh], one, kvs)
            vw_scr[kh] = jnp.where(mine[kh], one, kvw)

    ii = lax.broadcasted_iota(jnp.int32, (tq, 1), 0)
    row_t = t0 + ii
    nn = lax.broadcasted_iota(jnp.int32, (1, ncp), 1)
    maskc = ((nn * CMP_STRIDE + (CMP_BLOCK - 1)) <= row_t) & (nn < n_cmp)
    absd = jnp.abs(row_t.astype(F32) - (nn.astype(F32) * CMP_STRIDE + 0.5 * (CMP_BLOCK - 1)))
    mi = lax.broadcasted_iota(jnp.int32, (LANE, 1), 0)
    ov_t = ((nn * CMP_STRIDE < mi * SLC_BLOCK + SLC_BLOCK) & (nn * CMP_STRIDE + CMP_BLOCK > mi * SLC_BLOCK)
            & (nn < n_cmp) & (mi < n_slc))
    ov_t = jnp.where(ov_t, 1.0, 0.0).astype(BF16)
    nsp = -(-n_slc // 8) * 8
    mi_s = mi[0:nsp]
    cur_l = (t0 + lax.broadcasted_iota(jnp.int32, (1, tq), 1)) >> _log2(SLC_BLOCK)
    in_rng = mi_s < n_slc
    valid = (mi_s <= cur_l) & in_rng
    forced = (mi_s == 0) | (mi_s == cur_l) | (mi_s == cur_l - 1)

    def select(kh):
        qs = [q_ref[0, :, g * LANE:(g + 1) * LANE] * (HEAD_DIM ** -0.5) for g in range(G)]
        kvc = kvc_ref[0, :, kh * LANE:(kh + 1) * LANE]
        s_c = _dot_nt(jnp.concatenate([jnp.where(mine[kh], q, 0.0) for q in qs], axis=0), kvc)
        ps = []
        for g in range(G):
            s = jnp.where(maskc, s_c[g * tq:(g + 1) * tq] - slopes[kh * G + g] * absd, NEG)
            m = jnp.max(s, axis=-1, keepdims=True)
            e = jnp.where(maskc, jnp.exp(s - m), 0.0)
            ps.append(e / jnp.maximum(jnp.sum(e, axis=-1, keepdims=True), 1e-30))
        o_cmp = _dot(jnp.concatenate(ps, axis=0).astype(BF16), kvc)
        psum = ps[0] + ps[1] + ps[2]
        p_hi = psum.astype(BF16)
        p_lo = (psum - p_hi.astype(F32)).astype(BF16)
        imp_t = _dot_nt(ov_t, p_hi) + _dot_nt(ov_t, p_lo)
        score = jnp.where(valid, jnp.where(forced, FORCE_SCORE, imp_t[0:nsp]), -FORCE_SCORE)
        score = jnp.where(in_rng, score, -3.0 * FORCE_SCORE)
        rank = jnp.zeros((nsp, tq), F32)
        for mp in range(n_slc):
            row = score[mp:mp + 1, :]
            beats = (row > score) | ((row == score) & (mp < mi_s))
            rank = rank + jnp.where(beats, 1.0, 0.0)
        neg_t = jnp.where((rank < k_top) & valid, 0.0, NEG)
        lead = NSA_PAT_BLK if kh == 0 else 0
        pieces = [jnp.zeros((lead, tq), F32), neg_t, jnp.zeros((LANE - lead - nsp, tq), F32)]
        neg_t = jnp.concatenate([x for x in pieces if x.shape[0]], axis=0)
        return qs, o_cmp, neg_t.T

    def extended(kh, qs, neg):
        out = []
        for g in range(G):
            extra = sl_ref[kh, g:g + 1, :] + (0.0 if neg is None else neg)
            out.append(jnp.where(mine[kh], qs[g], extra.astype(BF16)))
        return jnp.concatenate(out, axis=0)

    def normalize(acc, kh):
        den = pltpu.roll(acc, HEAD_DIM, 1)
        return jnp.where(other[kh], acc / jnp.where(other[kh], den, 1.0), 0.0)

    n_kt = (t0 + tq + tk - 1) >> ltk
    nwt = min(-(-(WIN + tq) // tk), seq // tk)
    wk = nwt * tk
    kt0 = jnp.clip(n_kt - nwt, 0, seq // tk - nwt)
    k0 = pl.multiple_of(kt0 * tk, tk)
    span = pl.ds(k0, wk)
    rel = row_t - (k0 + lax.broadcasted_iota(jnp.int32, (1, wk), 1))
    ok_causal = jnp.where(rel >= 0, 0.0, NEG)
    ok_win = jnp.where(rel < WIN, ok_causal, NEG)

    def masked(s, bias):
        return (s.reshape(G, tq, wk) + bias[None]).reshape(G * tq, wk)

    o_cmp, o_win, q_sel, s_near, m_near = [], [], [], [], []
    picked = [select(kh) for kh in range(KH)]
    for kh in range(KH):
        qs, oc, neg = picked[kh]
        o_cmp.append(oc)
        q_sel.append(extended(kh, qs, neg))
        s_near.append(masked(_dot_nt(q_sel[kh], ks_scr[kh, span, :]), ok_causal))
        m_near.append(jnp.max(s_near[kh], axis=-1, keepdims=True))
    for kh in range(KH):
        s = masked(_dot_nt(extended(kh, picked[kh][0], None), kw_scr[kh, span, :]), ok_win)
        p = jnp.exp2(s - jnp.max(s, axis=-1, keepdims=True))
        o_win.append(normalize(_dot(p.astype(BF16), vw_scr[kh, span, :]), kh))

    def far_scores(kt, m_acc):
        out = []
        for kh in range(KH):
            s = _dot_nt(q_sel[kh], ks_scr[kh, pl.ds(pl.multiple_of(kt * tk, tk), tk), :])
            s_scr[kh, kt] = s
            m = m_acc[kh]
            for c in range(0, tk, LANE):
                m = jnp.maximum(m, s[:, c:c + LANE])
            out.append(m)
        return tuple(out)

    m_far = lax.fori_loop(0, kt0, far_scores, tuple(jnp.full((G * tq, LANE), NEG, F32) for _ in range(KH)))
    m_row = [jnp.maximum(jnp.max(m_far[kh], axis=-1, keepdims=True), m_near[kh]) for kh in range(KH)]
    acc0 = tuple(_dot(jnp.exp2(s_near[kh] - m_row[kh]).astype(BF16), vs_scr[kh, span, :]) for kh in range(KH))

    def far_accum(kt, acc):
        out = []
        for kh in range(KH):
            p = jnp.exp2(s_scr[kh, kt] - m_row[kh])
            out.append(acc[kh] + _dot(p.astype(BF16), vs_scr[kh, pl.ds(pl.multiple_of(kt * tk, tk), tk), :]))
        return tuple(out)

    acc = lax.fori_loop(0, kt0, far_accum, acc0)

    gates = jax.nn.sigmoid(gate_ref[0])
    heads = [[], []]
    for kh in range(KH):
        o_sel = normalize(acc[kh], kh)
        for g in range(G):
            hd = kh * G + g
            rows = slice(g * tq, (g + 1) * tq)
            heads[kh].append(gates[:, hd * 3:hd * 3 + 1] * o_cmp[kh][rows]
                             + gates[:, hd * 3 + 1:hd * 3 + 2] * o_sel[rows]
                             + gates[:, hd * 3 + 2:hd * 3 + 3] * o_win[kh][rows])
    for g in range(G):
        out_ref[0, :, g * LANE:(g + 1) * LANE] = jnp.where(lo_half, heads[1][g], heads[0][g]).astype(BF16)


def _nsa(pa, pf, cmp_kv, seq, tq, tk, nsa_slopes):
    B = pa.shape[0]
    n_cmp = (seq - CMP_BLOCK) // CMP_STRIDE + 1
    n_slc = seq // SLC_BLOCK
    ncp = cmp_kv.shape[1]
    kvw = NSA_KV_HEADS * LANE
    pat = _nsa_pattern(seq)
    sl_rows = _nsa_slope_rows([s * LOG2E for s in nsa_slopes])
    body = functools.partial(_nsa_body, seq=seq, tq=tq, tk=tk, n_cmp=n_cmp, n_slc=n_slc,
                             k_top=min(SLC_TOPN, n_slc), slopes=tuple(nsa_slopes))
    return pl.pallas_call(
        body,
        grid=(B, seq // tq),
        in_specs=[pl.BlockSpec((1, tq, NSA_Q_W), lambda b, i: (b, i, PA_NQ // NSA_Q_W)),
                  pl.BlockSpec((1, seq, kvw), lambda b, i: (b, 0, PA_KVS // kvw)),
                  pl.BlockSpec((1, seq, kvw), lambda b, i: (b, 0, PA_KVW // kvw)),
                  pl.BlockSpec((1, ncp, kvw), lambda b, i: (b, 0, 0)),
                  pl.BlockSpec((1, tq, LANE), lambda b, i: (b, i, PF_GATE // LANE)),
                  pl.BlockSpec((NSA_KV_HEADS, seq, LANE), lambda b, i: (0, 0, 0)),
                  pl.BlockSpec((NSA_KV_HEADS, 8, LANE), lambda b, i: (0, 0, 0))],
        out_specs=pl.BlockSpec((1, tq, NSA_Q_W), lambda b, i: (b, i, 0)),
        out_shape=jax.ShapeDtypeStruct((B, seq, NSA_Q_W), BF16),
        scratch_shapes=[pltpu.VMEM((NSA_KV_HEADS, seq, LANE), BF16)] * 4
        + [pltpu.VMEM((NSA_KV_HEADS, seq // tk, NSA_GROUP * tq, tk), F32)],
        compiler_params=pltpu.CompilerParams(dimension_semantics=("arbitrary",) * 2,
                                             vmem_limit_bytes=VMEM_LIMIT),
        name="nsa_attention",
    )(pa, pa, pa, cmp_kv, pf, pat, sl_rows)


def _dilated_body(q_ref, kv_ref, o_ref, lse_ref, *, nres, blk, ls, dil, win_keys, slopes, tq):
    ii = lax.broadcasted_iota(jnp.int32, (tq, 1), 0)
    jj = lax.broadcasted_iota(jnp.int32, (1, 2 * tq), 1)
    rel2 = ii - jj + tq
    ok2 = (rel2 >= 0) & (rel2 <= win_keys)
    lo_half = lax.broadcasted_iota(jnp.int32, (1, LANE), 1) < HEAD_DIM
    mine = (lo_half, jnp.logical_not(lo_half))
    bias2 = [(slopes[h] * float(dil)) * rel2.astype(F32) for h in range(DIL_HPG)]

    def rows(ref, r, t, lanes):
        if blk >= tq:
            a, b = divmod(t * tq, blk)
            return ref[0, a, r, b:b + tq, lanes]
        n = tq // blk
        return ref[0, t * n:(t + 1) * n, r, :, lanes].reshape(tq, LANE)

    def put(ref, r, t, val):
        if blk >= tq:
            a, b = divmod(t * tq, blk)
            ref[0, a, r, b:b + tq, :] = val
        else:
            n = tq // blk
            ref[0, t * n:(t + 1) * n, r, :, :] = val.reshape(n, blk, LANE)

    for r in range(nres):
        for t in range(ls // tq):
            q_pair = rows(q_ref, r, t, slice(0, LANE))
            outs, lses = [], []
            for h in range(DIL_HPG):
                lanes = slice(h * LANE, (h + 1) * LANE)
                qt = jnp.where(mine[h], q_pair, 0.0) * (HEAD_DIM ** -0.5)
                if t == 0:
                    kv = jnp.concatenate([rows(kv_ref, r, 0, lanes)] * 2, axis=0)
                    s = jnp.where(ok2 & (jj >= tq), _dot_nt(qt, kv) - bias2[h], NEG)
                else:
                    kv = jnp.concatenate([rows(kv_ref, r, t - 1, lanes), rows(kv_ref, r, t, lanes)], axis=0)
                    s = jnp.where(ok2, _dot_nt(qt, kv) - bias2[h], NEG)
                m = jnp.max(s, axis=-1, keepdims=True)
                e = jnp.exp(s - m)
                l = jnp.sum(e, axis=-1, keepdims=True)
                outs.append(_dot(e.astype(BF16), kv) / l)
                lses.append(jnp.broadcast_to(m + jnp.log(l), (tq, LANE)))
            put(o_ref, r, t, jnp.where(lo_half, outs[1], outs[0]).astype(BF16))
            put(lse_ref, r, t, jnp.where(lo_half, lses[1], lses[0]))


def _dilated(src, qcol, kvcol, batch, seq, tm, group, slopes, tq):
    win, dil = DIL_PAIRS[group]
    ls = seq // dil
    tq = min(tq, ls)
    if dil == 1:
        ntile, blk = 1, seq
    else:
        ntile, blk = seq // tm, tm // dil
    nres = max(1, min(dil, 16 * tq // ls))
    view = src.reshape(batch, ntile, dil, blk, src.shape[1])
    body = functools.partial(_dilated_body, nres=nres, blk=blk, ls=ls, dil=dil, win_keys=win // dil,
                             slopes=tuple(slopes[group * DIL_HPG:(group + 1) * DIL_HPG]), tq=tq)
    shp = (1, ntile, nres, blk)
    o, lse = pl.pallas_call(
        body,
        grid=(batch, dil // nres),
        in_specs=[pl.BlockSpec(shp + (DIL_Q_W,), lambda b, r: (b, 0, r, 0, qcol // DIL_Q_W)),
                  pl.BlockSpec(shp + (DIL_KV_W,), lambda b, r: (b, 0, r, 0, kvcol // DIL_KV_W))],
        out_specs=[pl.BlockSpec(shp + (DIL_Q_W,), lambda b, r: (b, 0, r, 0, 0))] * 2,
        out_shape=[jax.ShapeDtypeStruct((batch, ntile, dil, blk, DIL_Q_W), BF16),
                   jax.ShapeDtypeStruct((batch, ntile, dil, blk, DIL_Q_W), F32)],
        compiler_params=pltpu.CompilerParams(dimension_semantics=("arbitrary",) * 2,
                                             vmem_limit_bytes=VMEM_LIMIT),
        name=f"dilated_attention_g{group}",
    )(view, view)
    return o.reshape(batch * seq, DIL_Q_W), lse.reshape(batch * seq, DIL_Q_W)


def _hgrn2_body(q_ref, f_ref, i_ref, g_ref, lb_ref, o_ref, *, layer, seq, chunk):
    c = chunk
    sub = 8
    npair = HG_W // LANE
    lbs = lb_ref[...].astype(F32)
    mx = jnp.max(lbs, axis=0, keepdims=True)
    ex = jnp.exp(lbs - mx)
    sm = ex / jnp.sum(ex, axis=0, keepdims=True)
    lower = jnp.maximum(jnp.sum(sm[0:layer + 1], axis=0, keepdims=True) - sm[0:1], 0.0)
    log_lb = jnp.log(lower + LB_TINY)
    log_1m = jnp.log1p(-lower)

    lane = lax.broadcasted_iota(jnp.int32, (1, LANE), 1)
    head0 = lane < HG_DIM
    ri = lax.broadcasted_iota(jnp.int32, (c, 1), 0)
    ci = lax.broadcasted_iota(jnp.int32, (1, c), 1)
    tri = jnp.where(ci <= ri, 1.0, 0.0).astype(BF16)
    di = lax.broadcasted_iota(jnp.int32, (LANE, 1), 0)
    same_head = (di >= HG_DIM) == (lane >= HG_DIM)
    ones_blk = jnp.where(same_head, 1.0, 0.0).astype(BF16)
    gcol = lax.broadcasted_iota(jnp.int32, (1, sub * c), 1)
    gsum = jnp.where(((gcol >> _log2(c)) == (ri & (sub - 1)))
                     & (((gcol & (c - 1)) >> 3) == (ri >> 3)), 1.0, 0.0).astype(BF16)
    sp = lax.broadcasted_iota(jnp.int32, (1, sub, 1), 1)
    levels = []
    w = sub
    while w < c:
        same = (ri >> _log2(2 * w)) == (ci >> _log2(2 * w))
        levels.append((w, same & ((ri & (2 * w - 1)) >= w) & ((ci & (2 * w - 1)) < w)))
        w *= 2

    def split3(x):
        hi = x.astype(BF16)
        r1 = x - hi.astype(F32)
        mid = r1.astype(BF16)
        lo = (r1 - mid.astype(F32)).astype(BF16)
        return hi, mid, lo

    def pair_chunk(q, v, x, gt, llb, l1m, state_t):
        log_sig = jnp.minimum(x, 0.0) - jnp.log1p(jnp.exp(-jnp.abs(x)))
        t2 = l1m + log_sig
        lf = jnp.maximum(llb, t2) + jnp.log1p(jnp.exp(-jnp.abs(llb - t2)))
        kk = 1.0 - jnp.exp(lf)
        hi, mid, lo = split3(lf * LOG2E)
        b = _dot(tri, hi) + _dot(tri, mid) + _dot(tri, lo)
        vb = v.astype(BF16)

        q3 = q.reshape(c // sub, sub, LANE)
        k3 = kk.reshape(c // sub, sub, LANE)
        b3 = b.reshape(c // sub, sub, LANE)
        parts = []
        for tp in range(sub):
            dec = jnp.exp2(jnp.minimum(b3[:, tp:tp + 1, :] - b3, 0.0))
            parts.append(jnp.where(sp <= tp, q3[:, tp:tp + 1, :] * k3 * dec, 0.0).reshape(c, LANE))
        wall = jnp.concatenate(parts, axis=0)
        a_rep = _dot(wall.astype(BF16), ones_blk)
        z = a_rep * jnp.concatenate([v] * sub, axis=0)
        o = _dot(gsum, z.astype(BF16))

        a0 = jnp.zeros((c, c), F32)
        a1 = jnp.zeros((c, c), F32)
        for w, lmask in levels:
            b_r = b.reshape(c // (2 * w), 2 * w, LANE)
            bnd = jnp.broadcast_to(b_r[:, w - 1:w, :], b_r.shape).reshape(c, LANE)
            qe = q * jnp.exp2(jnp.minimum(b - bnd, 0.0))
            ke = (kk * jnp.exp2(jnp.minimum(bnd - b, 0.0))).astype(BF16)
            q2 = jnp.concatenate([jnp.where(head0, qe, 0.0), jnp.where(head0, 0.0, qe)], axis=0).astype(BF16)
            a01 = _dot_nt(q2, ke)
            a0 = a0 + jnp.where(lmask, a01[0:c], 0.0)
            a1 = a1 + jnp.where(lmask, a01[c:2 * c], 0.0)
        av = _dot(jnp.concatenate([a0, a1], axis=0).astype(BF16), vb)
        o = o + jnp.where(head0, av[0:c], av[c:2 * c])

        o = o + _dot_nt((q * jnp.exp2(b)).astype(BF16), state_t.astype(BF16))
        b_last = b[c - 1:c, :]
        khat = (kk * jnp.exp2(b_last - b)).astype(BF16)
        upd = lax.dot_general(vb, khat, (((0,), (0,)), ((), ())), preferred_element_type=F32)
        state_t = jnp.exp2(b_last) * state_t + jnp.where(same_head, upd, 0.0)

        o2 = o * o
        ms0 = jnp.sum(jnp.where(head0, o2, 0.0), axis=-1, keepdims=True)
        ms1 = jnp.sum(jnp.where(head0, 0.0, o2), axis=-1, keepdims=True)
        o = o * lax.rsqrt(jnp.where(head0, ms0, ms1) * (1.0 / HG_DIM) + EPS)
        return (o * (gt * jax.nn.sigmoid(gt))).astype(BF16), state_t

    def step(ic, states):
        rows = pl.ds(pl.multiple_of(ic * c, c), c)
        out = []
        for p in range(npair):
            cols = slice(p * LANE, (p + 1) * LANE)
            o, st = pair_chunk(q_ref[0, rows, cols], i_ref[0, rows, cols], f_ref[0, rows, cols],
                               g_ref[0, rows, cols], log_lb[:, cols], log_1m[:, cols], states[p])
            o_ref[0, rows, cols] = o
            out.append(st)
        return tuple(out)

    lax.fori_loop(0, seq // c, step, tuple(jnp.zeros((LANE, LANE), F32) for _ in range(npair)),
                  unroll=min(8, seq // c))


def _hgrn2(pf, hg_lb, layer, seq, chunk):
    B = pf.shape[0]
    sec = lambda off: (lambda b: (b, 0, off // HG_W))
    return pl.pallas_call(
        functools.partial(_hgrn2_body, layer=layer, seq=seq, chunk=chunk),
        grid=(B,),
        in_specs=[pl.BlockSpec((1, seq, HG_W), sec(PF_HQ)),
                  pl.BlockSpec((1, seq, HG_W), sec(PF_HF)),
                  pl.BlockSpec((1, seq, HG_W), sec(PF_HI)),
                  pl.BlockSpec((1, seq, HG_W), sec(PF_HG)),
                  pl.BlockSpec((DEPTH, HG_W), lambda b: (0, 0))],
        out_specs=pl.BlockSpec((1, seq, HG_W), lambda b: (b, 0, 0)),
        out_shape=jax.ShapeDtypeStruct((B, seq, HG_W), BF16),
        compiler_params=pltpu.CompilerParams(dimension_semantics=("arbitrary",),
                                             vmem_limit_bytes=VMEM_LIMIT),
        name="hgrn2",
    )(pf, pf, pf, pf, hg_lb)


def _outproj_body(nsa_ref, d0_ref, d1_ref, d2_ref, l0_ref, l1_ref, l2_ref, hg_ref, h_ref,
                  w_ref, g_ref, out_ref, tok_scr, *, tm):
    def token_order(ref, slot, d):
        if d == 1:
            return ref[...].astype(F32)
        n = tm // d
        for r in range(d):
            tok_scr[slot, pl.ds(r, n, stride=d), :] = ref[r * n:(r + 1) * n, :].astype(F32)
        return tok_scr[slot]

    dils = [d for _, d in DIL_PAIRS]
    os_ = [token_order(ref, 2 * g, dils[g]) for g, ref in enumerate((d0_ref, d1_ref, d2_ref))]
    ls = [token_order(ref, 2 * g + 1, dils[g]) for g, ref in enumerate((l0_ref, l1_ref, l2_ref))]
    lm = jnp.maximum(jnp.maximum(ls[0], ls[1]), ls[2])
    es = [jnp.exp(l - lm) for l in ls]
    inv = 1.0 / (es[0] + es[1] + es[2])
    mix = jnp.concatenate([nsa_ref[...]] + [(os_[g] * (es[g] * inv)).astype(BF16) for g in range(len(DIL_PAIRS))]
                          + [hg_ref[...]], axis=1)
    out_ref[...] = h_ref[...] + _rms(_dot(mix, w_ref[...]), g_ref[...])


def _outproj(nsa, dil_o, dil_l, hg, h2, w, g, l, tm):
    T = h2.shape[0]
    row = lambda i: (i, 0)
    return pl.pallas_call(
        functools.partial(_outproj_body, tm=tm),
        grid=(T // tm,),
        in_specs=[pl.BlockSpec((tm, NSA_Q_W), row)]
        + [pl.BlockSpec((tm, DIL_Q_W), row)] * 6
        + [pl.BlockSpec((tm, HG_W), row), pl.BlockSpec((tm, D_MODEL), row),
           _layer((MIX_W, D_MODEL), l), _layer((1, D_MODEL), l)],
        out_specs=pl.BlockSpec((tm, D_MODEL), row),
        out_shape=jax.ShapeDtypeStruct((T, D_MODEL), F32),
        scratch_shapes=[pltpu.VMEM((2 * len(DIL_PAIRS), tm, DIL_Q_W), F32)],
        compiler_params=pltpu.CompilerParams(dimension_semantics=("arbitrary",),
                                             vmem_limit_bytes=VMEM_LIMIT),
        name="outproj",
    )(nsa, *dil_o, *dil_l, hg, h2, w, g)


def _mlp_body(h_ref, p_ref, gpre_ref, wup_ref, wdn_ref, gpost_ref, gple_ref, wg_ref, wp_ref, out_ref, *, fc):
    h = h_ref[...]
    hn = _rms(h, gpre_ref[...]).astype(BF16)
    acc = jnp.zeros(h.shape, F32)
    for c in range(0, D_FF, fc):
        u = jnp.maximum(_dot(hn, wup_ref[:, c:c + fc]), 0.0)
        acc = acc + _dot((u * u).astype(BF16), wdn_ref[c:c + fc, :])
    h = h + _rms(acc, gpost_ref[...])
    gate = jax.nn.sigmoid(_dot(_rms(h, gple_ref[...]).astype(BF16), wg_ref[...]))
    out_ref[...] = h + _dot(p_ref[...].astype(BF16), wp_ref[...]) * gate


def _mlp(h2, p3, gpre, wup, wdn, gpost, gple, wg, wp, l, tm):
    T = h2.shape[0]
    row = lambda i: (i, 0)
    vec = _layer((1, D_MODEL), l)

    def resident(shape):
        return pl.BlockSpec((None,) + tuple(shape), lambda i: (l, 0, 0), pipeline_mode=pl.Buffered(1))

    return pl.pallas_call(
        functools.partial(_mlp_body, fc=512),
        grid=(T // tm,),
        in_specs=[pl.BlockSpec((tm, D_MODEL), row), pl.BlockSpec((None, tm, PLE_DIM), lambda i: (l, i, 0)), vec,
                  resident(wup.shape[1:]), resident(wdn.shape[1:]), vec, vec,
                  resident(wg.shape[1:]), resident(wp.shape[1:])],
        out_specs=pl.BlockSpec((tm, D_MODEL), row),
        out_shape=jax.ShapeDtypeStruct((T, D_MODEL), F32),
        compiler_params=pltpu.CompilerParams(dimension_semantics=("arbitrary",),
                                             vmem_limit_bytes=VMEM_LIMIT),
        name="mlp_ple",
    )(h2, p3, gpre, wup, wdn, gpost, gple, wg, wp)


def _inproj_columns():
    hd_cols = np.arange(HEAD_DIM)
    ca = np.zeros((WA,), np.int64)
    for g in range(NSA_GROUP):
        for kh in range(NSA_KV_HEADS):
            o = PA_NQ + g * LANE + kh * HEAD_DIM
            ca[o:o + HEAD_DIM] = OFF_NQ + (kh * NSA_GROUP + g) * HEAD_DIM + hd_cols
    ca[PA_DQ:PA_DQ + DIL_Q_W] = OFF_DQ + np.arange(DIL_Q_W)
    for kh in range(NSA_KV_HEADS):
        for base, ko, vo in ((PA_KVS, OFF_NKS, OFF_NVS), (PA_KVW, OFF_NKW, OFF_NVW)):
            o = base + kh * LANE
            first, second = (ko, vo) if kh == 0 else (vo, ko)
            ca[o:o + HEAD_DIM] = first + kh * HEAD_DIM + hd_cols
            ca[o + HEAD_DIM:o + LANE] = second + kh * HEAD_DIM + hd_cols

    def dil_kv(dst, base, g):
        for i in range(DIL_HPG):
            hd = g * DIL_HPG + i
            o = base + i * LANE
            first, second = (OFF_DK, OFF_DV) if i == 0 else (OFF_DV, OFF_DK)
            dst[o:o + HEAD_DIM] = first + hd * HEAD_DIM + hd_cols
            dst[o + HEAD_DIM:o + LANE] = second + hd * HEAD_DIM + hd_cols

    dil_kv(ca, PA_DKV, 0)
    cd = np.zeros((len(DIL_PAIRS) - 1, WD), np.int64)
    for g in range(1, len(DIL_PAIRS)):
        dil_kv(cd[g - 1], PD_KV, g)
        cd[g - 1, PD_Q:PD_Q + DIL_Q_W] = OFF_DQ + g * DIL_Q_W + np.arange(DIL_Q_W)
    cf = np.full((WF,), IN_TOTAL, np.int64)
    for dst, src in ((PF_HQ, OFF_HQ), (PF_HF, OFF_HF), (PF_HI, OFF_HI), (PF_HG, OFF_HG)):
        cf[dst:dst + HG_W] = src + np.arange(HG_W)
    cf[PF_GATE:PF_GATE + 3 * NSA_HEADS] = OFF_GATE + np.arange(3 * NSA_HEADS)
    cc = np.concatenate([OFF_NKC + np.arange(NSA_KV_W), OFF_NVC + np.arange(NSA_KV_W)])
    return ca, cd, cf, cc


def _take(w, idx, axis, scale=None):
    idx = np.asarray(idx)
    n_src = w.shape[axis]
    bounds = [0] + [i for i in range(1, len(idx))
                    if (idx[i] != idx[i - 1] + 1 and not (idx[i] == n_src and idx[i - 1] == n_src))
                    or (scale is not None and idx[i] < n_src and idx[i - 1] < n_src and scale[idx[i]] != scale[idx[i - 1]])]
    bounds.append(len(idx))
    pieces = []
    for a, b in zip(bounds[:-1], bounds[1:]):
        src = int(idx[a])
        if src == n_src:
            shape = list(w.shape)
            shape[axis] = b - a
            pieces.append(jnp.zeros(shape, w.dtype))
            continue
        piece = lax.slice_in_dim(w, src, src + (b - a), axis=axis)
        if scale is not None and scale[src] != 1.0:
            piece = piece * float(scale[src])
        pieces.append(piece)
    return jnp.concatenate(pieces, axis=axis)


def _outproj_rows():
    hd_cols = np.arange(HEAD_DIM)
    rows = np.arange(MIX_W)
    for g in range(NSA_GROUP):
        for slot, kh in enumerate((1, 0)):
            o = g * LANE + slot * HEAD_DIM
            rows[o:o + HEAD_DIM] = (kh * NSA_GROUP + g) * HEAD_DIM + hd_cols
    for g in range(len(DIL_PAIRS)):
        for slot, i in enumerate((1, 0)):
            o = NSA_Q_W + g * DIL_Q_W + slot * HEAD_DIM
            rows[o:o + HEAD_DIM] = NSA_Q_W + (g * DIL_HPG + i) * HEAD_DIM + hd_cols
    return rows


def kernel(x, p, w_in, w_out, cmp_pos, cmp_w1, cmp_w2, hg_lb, g_pre_mix, g_post_mix,
           g_pre_mlp, g_post_mlp, w_up, w_down, g_ple, w_ple_gate, w_ple_proj):
    B, S, D = x.shape
    T = B * S
    tm = 512 if S % 512 == 0 else 256
    tm_mlp = 512
    tq_nsa, tk_nsa = 256, 256
    tq_dil = 128
    hg_chunk = 128
    assert D == D_MODEL and S % tm == 0 and S // SLC_BLOCK <= LANE
    nsa_slopes, dil_slopes = _alibi_slopes()
    ca, cd, cf, cc = _inproj_columns()
    n_cmp = (S - CMP_BLOCK) // CMP_STRIDE + 1
    nr = S // CMP_STRIDE
    ncp = -(-nr // LANE) * LANE

    key_scale = np.ones((IN_TOTAL,), np.float32)
    key_scale[OFF_NKS:OFF_NKS + NSA_KV_W] = LOG2E
    key_scale[OFF_NKW:OFF_NKW + NSA_KV_W] = LOG2E
    wa = _take(w_in, ca, 2, scale=key_scale).astype(BF16)
    wd_in = jnp.stack([_take(w_in, cd[g], 2) for g in range(cd.shape[0])], axis=1).astype(BF16)
    wf = _take(w_in, cf, 2).astype(BF16)
    wc = _take(w_in, cc, 2).astype(BF16)
    pos8 = jnp.pad(cmp_pos.reshape(DEPTH, 2, 1, CMP_BLOCK * HEAD_DIM), ((0, 0), (0, 0), (0, 7), (0, 0))).astype(BF16)
    lo_pad, hi_pad = ((0, 0), (0, 0), (0, HEAD_DIM)), ((0, 0), (0, 0), (HEAD_DIM, 0))
    w2p = jnp.stack([jnp.stack([jnp.pad(cmp_w2[:, 0], lo_pad), jnp.pad(cmp_w2[:, 1], hi_pad)], axis=1),
                     jnp.stack([jnp.pad(cmp_w2[:, 0], hi_pad), jnp.pad(cmp_w2[:, 1], lo_pad)], axis=1)],
                    axis=1).astype(BF16)
    w1 = cmp_w1.astype(BF16)
    wo =_take(w_out, _outproj_rows(), 1).astype(BF16)
    wup, wdn = w_up.astype(BF16), w_down.astype(BF16)
    wg, wp = w_ple_gate.astype(BF16), w_ple_proj.astype(BF16)
    vec = lambda g: g.reshape(DEPTH, 1, D)
    g_pre_mix, g_post_mix, g_pre_mlp, g_post_mlp, g_ple = map(vec, (g_pre_mix, g_post_mix, g_pre_mlp, g_post_mlp, g_ple))
    p3 = p.reshape(DEPTH, T, PLE_DIM)

    h = x.reshape(T, D)
    for l in range(DEPTH):
        pa, pf, pd1, pd2, *xs = _inproj(h, g_pre_mix, wa, wf, wd_in, wc, l, tm)
        pa3 = pa.reshape(B, S, WA)
        pf3 = pf.reshape(B, S, WF)
        cmp_kv = _compress([a.reshape(B, nr, CMP_ROW_W) for a in xs], pos8, w1, w2p, l, ncp, n_cmp)
        o_nsa = _nsa(pa3, pf3, cmp_kv, S, tq_nsa, tk_nsa, nsa_slopes).reshape(T, NSA_Q_W)
        dil = [_dilated(pa, PA_DQ, PA_DKV, B, S, tm, 0, dil_slopes, tq_dil),
               _dilated(pd1, PD_Q, PD_KV, B, S, tm, 1, dil_slopes, tq_dil),
               _dilated(pd2, PD_Q, PD_KV, B, S, tm, 2, dil_slopes, tq_dil)]
        o_hg = _hgrn2(pf3, hg_lb, l, S, hg_chunk).reshape(T, HG_W)
        h = _outproj(o_nsa, [d[0] for d in dil], [d[1] for d in dil], o_hg, h, wo, g_post_mix, l, tm)
        h = _mlp(h, p3, g_pre_mlp, wup, wdn, g_post_mlp, g_ple, wg, wp, l, tm_mlp)
    return h.reshape(B, S, D)
```

```python
import functools

import numpy as np
import jax
import jax.numpy as jnp
from jax import lax
from jax.experimental import pallas as pl
from jax.experimental.pallas import tpu as pltpu

F32 = jnp.float32
BF16 = jnp.bfloat16

D_MODEL = 1024
DEPTH = 2
HEAD_DIM = 64
NSA_HEADS = 6
NSA_KV_HEADS = 2
NSA_GROUP = NSA_HEADS // NSA_KV_HEADS
CMP_BLOCK = 32
CMP_STRIDE = 16
CMP_HIDDEN = 256
SLC_BLOCK = 64
SLC_TOPN = 8
WIN = 512
FORCE_SCORE = 1e9
DIL_PAIRS = ((128, 1), (512, 4), (2048, 16))
DIL_HPG = 2
DIL_HEADS = DIL_HPG * len(DIL_PAIRS)
HG_HEADS = 4
HG_DIM = 64
LB_TINY = 1e-30
D_FF = 4 * D_MODEL
PLE_DIM = 256
EPS = 1e-6
NEG = -1e30
LOG2E = 1.4426950408889634

NSA_Q_W = NSA_HEADS * HEAD_DIM
NSA_KV_W = NSA_KV_HEADS * HEAD_DIM
DIL_W = DIL_HEADS * HEAD_DIM
HG_W = HG_HEADS * HG_DIM
MIX_W = NSA_Q_W + DIL_W + HG_W
IN_WIDTHS = (NSA_Q_W,) + (NSA_KV_W,) * 6 + (3 * NSA_HEADS,) + (DIL_W,) * 3 + (HG_W,) * 4
IN_TOTAL = sum(IN_WIDTHS)
IN_OFF = tuple(int(v) for v in np.cumsum((0,) + IN_WIDTHS))
(OFF_NQ, OFF_NKC, OFF_NVC, OFF_NKS, OFF_NVS, OFF_NKW, OFF_NVW, OFF_GATE,
 OFF_DQ, OFF_DK, OFF_DV, OFF_HQ, OFF_HF, OFF_HI, OFF_HG) = IN_OFF[:-1]

LANE = 128
VMEM_LIMIT = 56 * 1024 * 1024

DIL_Q_W = DIL_HPG * HEAD_DIM
DIL_KV_W = DIL_HPG * LANE
PA_NQ = 0
PA_DQ = PA_NQ + NSA_Q_W
PA_KVS = PA_DQ + DIL_Q_W
PA_KVW = PA_KVS + NSA_KV_HEADS * LANE
PA_DKV = PA_KVW + NSA_KV_HEADS * LANE
WA = PA_DKV + DIL_KV_W
PD_KV, PD_Q = 0, DIL_KV_W
WD = DIL_KV_W + DIL_Q_W
PF_HQ, PF_HF, PF_HI, PF_HG, PF_GATE = 0, HG_W, 2 * HG_W, 3 * HG_W, 4 * HG_W
WF = PF_GATE + LANE
WC = 2 * NSA_KV_W
CMP_ROW_W = CMP_STRIDE * HEAD_DIM


def _dot(a, b):
    return jnp.dot(a, b, preferred_element_type=F32)


def _dot_nt(a, b):
    return lax.dot_general(a, b, (((1,), (1,)), ((), ())), preferred_element_type=F32)


def _rms(x, g):
    return x * lax.rsqrt(jnp.mean(x * x, axis=-1, keepdims=True) + EPS) * g


def _log2(n):
    l = int(n).bit_length() - 1
    assert (1 << l) == n, n
    return l


def _alibi_slopes():
    n = NSA_HEADS + DIL_HEADS
    s = 2.0 ** (-8.0 * np.arange(1, n + 1) / n)
    quads = s.reshape(-1, 4)
    nsa = [float(np.float32(v)) for v in quads[:, 2:].reshape(-1)]
    dil = [float(np.float32(v)) for v in quads[:, :2].reshape(-1)]
    return nsa, dil


def _layer(shape, l):
    zeros = (0,) * len(shape)
    return pl.BlockSpec((None,) + tuple(shape), lambda *_: (l,) + zeros)


def _inproj_body(x_ref, g_ref, wa_ref, wf_ref, wd_ref, wc_ref,
                 oa_ref, of_ref, od1_ref, od2_ref, xk0_ref, xk1_ref, xv0_ref, xv1_ref, hn_scr, *, tm):
    hn32 = _rms(x_ref[...], g_ref[...])
    hn = hn32.astype(BF16)
    for c in range(0, WA, 256):
        oa_ref[:, c:c + 256] = _dot(hn, wa_ref[:, c:c + 256]).astype(BF16)
    for c in range(0, WF, 256):
        c1 = min(c + 256, WF)
        of_ref[:, c:c1] = _dot(hn, wf_ref[:, c:c1])
    nlb = D_MODEL // LANE
    for c in range(nlb):
        hn_scr[c] = hn32[:, c * LANE:(c + 1) * LANE]

    def by_residue(d):
        parts = [jnp.concatenate([hn_scr[c, pl.ds(r, tm // d, stride=d), :] for c in range(nlb)], axis=1)
                 for r in range(d)]
        return jnp.concatenate(parts, axis=0).astype(BF16)

    for gi, o_ref in ((1, od1_ref), (2, od2_ref)):
        hp = by_residue(DIL_PAIRS[gi][1])
        for c0, c1 in ((0, DIL_KV_W), (DIL_KV_W, WD)):
            o_ref[:, c0:c1] = _dot(hp, wd_ref[gi - 1, :, c0:c1]).astype(BF16)
    assert DIL_PAIRS[2][1] == CMP_STRIDE
    cc = _dot(hp, wc_ref[...])
    nr = tm // CMP_STRIDE
    for j in range(CMP_STRIDE):
        for i, x_ref in enumerate((xk0_ref, xk1_ref, xv0_ref, xv1_ref)):
            x_ref[:, j * HEAD_DIM:(j + 1) * HEAD_DIM] = cc[j * nr:(j + 1) * nr, i * HEAD_DIM:(i + 1) * HEAD_DIM]


def _inproj(x2, g, wa, wf, wd, wc, l, tm):
    T = x2.shape[0]
    row = lambda i: (i, 0)
    nr = tm // CMP_STRIDE
    return pl.pallas_call(
        functools.partial(_inproj_body, tm=tm),
        grid=(T // tm,),
        in_specs=[pl.BlockSpec((tm, D_MODEL), row), _layer((1, D_MODEL), l),
                  _layer((D_MODEL, WA), l), _layer((D_MODEL, WF), l),
                  _layer((2, D_MODEL, WD), l), _layer((D_MODEL, WC), l)],
        out_specs=[pl.BlockSpec((tm, WA), row), pl.BlockSpec((tm, WF), row),
                   pl.BlockSpec((tm, WD), row), pl.BlockSpec((tm, WD), row),
                   ] + [pl.BlockSpec((nr, CMP_ROW_W), row)] * 4,
        out_shape=[jax.ShapeDtypeStruct((T, WA), BF16), jax.ShapeDtypeStruct((T, WF), F32),
                   jax.ShapeDtypeStruct((T, WD), BF16), jax.ShapeDtypeStruct((T, WD), BF16),
                   ] + [jax.ShapeDtypeStruct((T // CMP_STRIDE, CMP_ROW_W), F32)] * 4,
        scratch_shapes=[pltpu.VMEM((D_MODEL // LANE, tm, LANE), F32)],
        compiler_params=pltpu.CompilerParams(dimension_semantics=("arbitrary",),
                                             vmem_limit_bytes=VMEM_LIMIT),
        name="inproj",
    )(x2, g, wa, wf, wd, wc)


def _compress_body(xk0_ref, xk1_ref, xv0_ref, xv1_ref, pos_ref, w1_ref, w2_ref, out_ref, *, n_cmp):
    nr = xk0_ref.shape[1]
    half = (CMP_BLOCK // 2) * HEAD_DIM
    rows = lax.broadcasted_iota(jnp.int32, (nr, 1), 0)
    out_ref[...] = jnp.zeros(out_ref.shape, out_ref.dtype)
    srcs = ((xk0_ref, xv0_ref), (xk1_ref, xv1_ref))
    for h in range(NSA_KV_HEADS):
        acc = jnp.zeros((nr, LANE), F32)
        for ten in range(2):
            x = srcs[h][ten][0].astype(BF16)
            first = _dot(x, w1_ref[ten, 0:half, :])
            second = _dot(x, w1_ref[ten, half:2 * half, :])
            posb = _dot(pos_ref[ten], w1_ref[ten])[0:1, :]
            hid = first + pltpu.roll(second, nr - 1, 0) + posb
            act = hid * jax.nn.sigmoid(hid)
            acc = acc + _dot(act.astype(BF16), w2_ref[h, ten])
        acc = jnp.where(rows < n_cmp, acc, 0.0)
        out_ref[0, 0:nr, h * LANE:(h + 1) * LANE] = acc.astype(BF16)


def _compress(xs, pos8, w1, w2p, l, ncp, n_cmp):
    B, nr, kw = xs[0].shape
    seq = lambda b: (b, 0, 0)
    return pl.pallas_call(
        functools.partial(_compress_body, n_cmp=n_cmp),
        grid=(B,),
        in_specs=[pl.BlockSpec((1, nr, kw), seq)] * 4
        + [_layer(pos8.shape[1:], l), _layer(w1.shape[1:], l), _layer(w2p.shape[1:], l)],
        out_specs=pl.BlockSpec((1, ncp, NSA_KV_HEADS * LANE), seq),
        out_shape=jax.ShapeDtypeStruct((B, ncp, NSA_KV_HEADS * LANE), BF16),
        compiler_params=pltpu.CompilerParams(dimension_semantics=("arbitrary",),
                                             vmem_limit_bytes=VMEM_LIMIT),
        name="nsa_compress",
    )(*xs, pos8, w1, w2p)


NSA_PAT_BLK = HEAD_DIM
NSA_PAT_POS = HEAD_DIM + 32


def _nsa_pattern(seq):
    pos = np.arange(seq)
    pat = np.zeros((seq, LANE), np.float32)
    pat[pos, NSA_PAT_BLK + pos // SLC_BLOCK] = 1.0
    pat[:, NSA_PAT_POS:NSA_PAT_POS + 3] = (SLC_BLOCK * (pos // SLC_BLOCK))[:, None]
    pat[:, NSA_PAT_POS + 3:NSA_PAT_POS + 6] = (pos % SLC_BLOCK)[:, None]
    return jnp.asarray(np.stack([pat, np.roll(pat, HEAD_DIM, axis=1)]), BF16)


def _nsa_slope_rows(nsa_slopes):
    rows = np.zeros((NSA_KV_HEADS, 8, LANE), np.float32)
    for kh in range(NSA_KV_HEADS):
        for g in range(NSA_GROUP):
            rest = np.float32(nsa_slopes[kh * NSA_GROUP + g])
            for part in range(3):
                piece = np.float32(np.asarray(rest, dtype=BF16))
                rows[kh, g, NSA_PAT_POS + part] = piece
                rows[kh, g, NSA_PAT_POS + 3 + part] = piece
                rest = np.float32(rest - piece)
    rows[1] = np.roll(rows[1], HEAD_DIM, axis=1)
    return jnp.asarray(rows)


def _nsa_body(q_ref, kvs_ref, kvw_ref, kvc_ref, gate_ref, pat_ref, sl_ref, out_ref,
              ks_scr, vs_scr, kw_scr, vw_scr, s_scr, *, seq, tq, tk, n_cmp, n_slc, k_top, slopes):
    qi = pl.program_id(1)
    G = NSA_GROUP
    KH = NSA_KV_HEADS
    t0 = qi * tq
    ncp = kvc_ref.shape[1]
    ltk = _log2(tk)
    lane = lax.broadcasted_iota(jnp.int32, (1, LANE), 1)
    lo_half = lane < HEAD_DIM
    hi_half = lane >= HEAD_DIM
    mine = (lo_half, hi_half)
    other = (hi_half, lo_half)

    @pl.when(qi == 0)
    def _():
        one = jnp.ones((seq, LANE), BF16)
        for kh in range(KH):
            cols = slice(kh * LANE, (kh + 1) * LANE)
            kvs = kvs_ref[0, :, cols]
            kvw = kvw_ref[0, :, cols]
            ks_scr[kh] = jnp.where(mine[kh], kvs, pat_ref[kh])
            kw_scr[kh] = jnp.where(mine[kh], kvw, pat_ref[kh])
            vs_scr[kh] = jnp.where(mine[kh], one, kvs)
            vw_scr[kh] = jnp.where(mine[kh], one, kvw)

    ii = lax.broadcasted_iota(jnp.int32, (tq, 1), 0)
    row_t = t0 + ii
    nn = lax.broadcasted_iota(jnp.int32, (1, ncp), 1)
    maskc = ((nn * CMP_STRIDE + (CMP_BLOCK - 1)) <= row_t) & (nn < n_cmp)
    absd = jnp.abs(row_t.astype(F32) - (nn.astype(F32) * CMP_STRIDE + 0.5 * (CMP_BLOCK - 1)))
    mi = lax.broadcasted_iota(jnp.int32, (LANE, 1), 0)
    ov_t = ((nn * CMP_STRIDE < mi * SLC_BLOCK + SLC_BLOCK) & (nn * CMP_STRIDE + CMP_BLOCK > mi * SLC_BLOCK)
            & (nn < n_cmp) & (mi < n_slc))
    ov_t = jnp.where(ov_t, 1.0, 0.0).astype(BF16)
    nsp = -(-n_slc // 8) * 8
    mi_s = mi[0:nsp]
    cur_l = (t0 + lax.broadcasted_iota(jnp.int32, (1, tq), 1)) >> _log2(SLC_BLOCK)
    in_rng = mi_s < n_slc
    valid = (mi_s <= cur_l) & in_rng
    forced = (mi_s == 0) | (mi_s == cur_l) | (mi_s == cur_l - 1)

    def select(kh):
        qs = [q_ref[0, :, g * LANE:(g + 1) * LANE] * (HEAD_DIM ** -0.5) for g in range(G)]
        kvc = kvc_ref[0, :, kh * LANE:(kh + 1) * LANE]
        s_c = _dot_nt(jnp.concatenate([jnp.where(mine[kh], q, 0.0) for q in qs], axis=0), kvc)
        ps = []
        for g in range(G):
            s = jnp.where(maskc, s_c[g * tq:(g + 1) * tq] - slopes[kh * G + g] * absd, NEG)
            m = jnp.max(s, axis=-1, keepdims=True)
            e = jnp.where(maskc, jnp.exp(s - m), 0.0)
            ps.append(e / jnp.maximum(jnp.sum(e, axis=-1, keepdims=True), 1e-30))
        o_cmp = _dot(jnp.concatenate(ps, axis=0).astype(BF16), kvc)
        psum = ps[0] + ps[1] + ps[2]
        p_hi = psum.astype(BF16)
        p_lo = (psum - p_hi.astype(F32)).astype(BF16)
        imp_t = _dot_nt(ov_t, p_hi) + _dot_nt(ov_t, p_lo)
        score = jnp.where(valid, jnp.where(forced, FORCE_SCORE, imp_t[0:nsp]), -FORCE_SCORE)
        score = jnp.where(in_rng, score, -3.0 * FORCE_SCORE)
        rank = jnp.zeros((nsp, tq), F32)
        for mp in range(n_slc):
            row = score[mp:mp + 1, :]
            beats = (row > score) | ((row == score) & (mp < mi_s))
            rank = rank + jnp.where(beats, 1.0, 0.0)
        neg_t = jnp.where((rank < k_top) & valid, 0.0, NEG)
        lead = NSA_PAT_BLK if kh == 0 else 0
        pieces = [jnp.zeros((lead, tq), F32), neg_t, jnp.zeros((LANE - lead - nsp, tq), F32)]
        neg_t = jnp.concatenate([x for x in pieces if x.shape[0]], axis=0)
        return qs, o_cmp, neg_t.T

    def extended(kh, qs, neg):
        out = []
        for g in range(G):
            extra = sl_ref[kh, g:g + 1, :] + (0.0 if neg is None else neg)
            out.append(jnp.where(mine[kh], qs[g], extra.astype(BF16)))
        return jnp.concatenate(out, axis=0)

    def normalize(acc, kh):
        den = pltpu.roll(acc, HEAD_DIM, 1)
        return jnp.where(other[kh], acc / jnp.where(other[kh], den, 1.0), 0.0)

    n_kt = (t0 + tq + tk - 1) >> ltk
    nwt = min(-(-(WIN + tq) // tk), seq // tk)
    wk = nwt * tk
    kt0 = jnp.clip(n_kt - nwt, 0, seq // tk - nwt)
    k0 = pl.multiple_of(kt0 * tk, tk)
    span = pl.ds(k0, wk)
    rel = row_t - (k0 + lax.broadcasted_iota(jnp.int32, (1, wk), 1))
    ok_causal = jnp.where(rel >= 0, 0.0, NEG)
    ok_win = jnp.where(rel < WIN, ok_causal, NEG)

    def masked(s, bias):
        return (s.reshape(G, tq, wk) + bias[None]).reshape(G * tq, wk)

    o_cmp, o_win, q_sel, s_near, m_near = [], [], [], [], []
    picked = [select(kh) for kh in range(KH)]
    for kh in range(KH):
        qs, oc, neg = picked[kh]
        o_cmp.append(oc)
        q_sel.append(extended(kh, qs, neg))
        s_near.append(masked(_dot_nt(q_sel[kh], ks_scr[kh, span, :]), ok_causal))
        m_near.append(jnp.max(s_near[kh], axis=-1, keepdims=True))
    for kh in range(KH):
        s = masked(_dot_nt(extended(kh, picked[kh][0], None), kw_scr[kh, span, :]), ok_win)
        p = jnp.exp2(s - jnp.max(s, axis=-1, keepdims=True))
        o_win.append(normalize(_dot(p.astype(BF16), vw_scr[kh, span, :]), kh))

    def far_scores(kt, m_acc):
        out = []
        for kh in range(KH):
            s = _dot_nt(q_sel[kh], ks_scr[kh, pl.ds(pl.multiple_of(kt * tk, tk), tk), :])
            s_scr[kh, kt] = s
            m = m_acc[kh]
            for c in range(0, tk, LANE):
                m = jnp.maximum(m, s[:, c:c + LANE])
            out.append(m)
        return tuple(out)

    m_far = lax.fori_loop(0, kt0, far_scores, tuple(jnp.full((G * tq, LANE), NEG, F32) for _ in range(KH)))
    m_row = [jnp.maximum(jnp.max(m_far[kh], axis=-1, keepdims=True), m_near[kh]) for kh in range(KH)]
    acc0 = tuple(_dot(jnp.exp2(s_near[kh] - m_row[kh]).astype(BF16), vs_scr[kh, span, :]) for kh in range(KH))

    def far_accum(kt, acc):
        out = []
        for kh in range(KH):
            p = jnp.exp2(s_scr[kh, kt] - m_row[kh])
            out.append(acc[kh] + _dot(p.astype(BF16), vs_scr[kh, pl.ds(pl.multiple_of(kt * tk, tk), tk), :]))
        return tuple(out)

    acc = lax.fori_loop(0, kt0, far_accum, acc0)

    gates = jax.nn.sigmoid(gate_ref[0])
    heads = [[], []]
    for kh in range(KH):
        o_sel = normalize(acc[kh], kh)
        for g in range(G):
            hd = kh * G + g
            rows = slice(g * tq, (g + 1) * tq)
            heads[kh].append(gates[:, hd * 3:hd * 3 + 1] * o_cmp[kh][rows]
                             + gates[:, hd * 3 + 1:hd * 3 + 2] * o_sel[rows]
                             + gates[:, hd * 3 + 2:hd * 3 + 3] * o_win[kh][rows])
    for g in range(G):
        out_ref[0, :, g * LANE:(g + 1) * LANE] = jnp.where(lo_half, heads[1][g], heads[0][g]).astype(BF16)


def _nsa(pa, pf, cmp_kv, seq, tq, tk, nsa_slopes):
    B = pa.shape[0]
    n_cmp = (seq - CMP_BLOCK) // CMP_STRIDE + 1
    n_slc = seq // SLC_BLOCK
    ncp = cmp_kv.shape[1]
    kvw = NSA_KV_HEADS * LANE
    pat = _nsa_pattern(seq)
    sl_rows = _nsa_slope_rows([s * LOG2E for s in nsa_slopes])
    body = functools.partial(_nsa_body, seq=seq, tq=tq, tk=tk, n_cmp=n_cmp, n_slc=n_slc,
                             k_top=min(SLC_TOPN, n_slc), slopes=tuple(nsa_slopes))
    return pl.pallas_call(
        body,
        grid=(B, seq // tq),
        in_specs=[pl.BlockSpec((1, tq, NSA_Q_W), lambda b, i: (b, i, PA_NQ // NSA_Q_W)),
                  pl.BlockSpec((1, seq, kvw), lambda b, i: (b, 0, PA_KVS // kvw)),
                  pl.BlockSpec((1, seq, kvw), lambda b, i: (b, 0, PA_KVW // kvw)),
                  pl.BlockSpec((1, ncp, kvw), lambda b, i: (b, 0, 0)),
                  pl.BlockSpec((1, tq, LANE), lambda b, i: (b, i, PF_GATE // LANE)),
                  pl.BlockSpec((NSA_KV_HEADS, seq, LANE), lambda b, i: (0, 0, 0)),
                  pl.BlockSpec((NSA_KV_HEADS, 8, LANE), lambda b, i: (0, 0, 0))],
        out_specs=pl.BlockSpec((1, tq, NSA_Q_W), lambda b, i: (b, i, 0)),
        out_shape=jax.ShapeDtypeStruct((B, seq, NSA_Q_W), BF16),
        scratch_shapes=[pltpu.VMEM((NSA_KV_HEADS, seq, LANE), BF16)] * 4
        + [pltpu.VMEM((NSA_KV_HEADS, seq // tk, NSA_GROUP * tq, tk), F32)],
        compiler_params=pltpu.CompilerParams(dimension_semantics=("arbitrary",) * 2,
                                             vmem_limit_bytes=VMEM_LIMIT),
        name="nsa_attention",
    )(pa, pa, pa, cmp_kv, pf, pat, sl_rows)


def _dilated_body(q_ref, kv_ref, o_ref, lse_ref, *, nres, blk, ls, dil, win_keys, slopes, tq):
    ii = lax.broadcasted_iota(jnp.int32, (tq, 1), 0)
    jj = lax.broadcasted_iota(jnp.int32, (1, 2 * tq), 1)
    rel2 = ii - jj + tq
    ok2 = (rel2 >= 0) & (rel2 <= win_keys)
    lo_half = lax.broadcasted_iota(jnp.int32, (1, LANE), 1) < HEAD_DIM
    mine = (lo_half, jnp.logical_not(lo_half))
    bias2 = [(slopes[h] * float(dil)) * rel2.astype(F32) for h in range(DIL_HPG)]

    def rows(ref, r, t, lanes):
        if blk >= tq:
            a, b = divmod(t * tq, blk)
            return ref[0, a, r, b:b + tq, lanes]
        n = tq // blk
        return ref[0, t * n:(t + 1) * n, r, :, lanes].reshape(tq, LANE)

    def put(ref, r, t, val):
        if blk >= tq:
            a, b = divmod(t * tq, blk)
            ref[0, a, r, b:b + tq, :] = val
        else:
            n = tq // blk
            ref[0, t * n:(t + 1) * n, r, :, :] = val.reshape(n, blk, LANE)

    for r in range(nres):
        for t in range(ls // tq):
            q_pair = rows(q_ref, r, t, slice(0, LANE))
            outs, lses = [], []
            for h in range(DIL_HPG):
                lanes = slice(h * LANE, (h + 1) * LANE)
                qt = jnp.where(mine[h], q_pair, 0.0) * (HEAD_DIM ** -0.5)
                if t == 0:
                    kv = jnp.concatenate([rows(kv_ref, r, 0, lanes)] * 2, axis=0)
                    s = jnp.where(ok2 & (jj >= tq), _dot_nt(qt, kv) - bias2[h], NEG)
                else:
                    kv = jnp.concatenate([rows(kv_ref, r, t - 1, lanes), rows(kv_ref, r, t, lanes)], axis=0)
                    s = jnp.where(ok2, _dot_nt(qt, kv) - bias2[h], NEG)
                m = jnp.max(s, axis=-1, keepdims=True)
                e = jnp.exp(s - m)
                l = jnp.sum(e, axis=-1, keepdims=True)
                outs.append(_dot(e.astype(BF16), kv) / l)
                lses.append(jnp.broadcast_to(m + jnp.log(l), (tq, LANE)))
            put(o_ref, r, t, jnp.where(lo_half, outs[1], outs[0]).astype(BF16))
            put(lse_ref, r, t, jnp.where(lo_half, lses[1], lses[0]))


def _dilated(src, qcol, kvcol, batch, seq, tm, group, slopes, tq):
    win, dil = DIL_PAIRS[group]
    ls = seq // dil
    tq = min(tq, ls)
    if dil == 1:
        ntile, blk = 1, seq
    else:
        ntile, blk = seq // tm, tm // dil
    nres = max(1, min(dil, 16 * tq // ls))
    view = src.reshape(batch, ntile, dil, blk, src.shape[1])
    body = functools.partial(_dilated_body, nres=nres, blk=blk, ls=ls, dil=dil, win_keys=win // dil,
                             slopes=tuple(slopes[group * DIL_HPG:(group + 1) * DIL_HPG]), tq=tq)
    shp = (1, ntile, nres, blk)
    o, lse = pl.pallas_call(
        body,
        grid=(batch, dil // nres),
        in_specs=[pl.BlockSpec(shp + (DIL_Q_W,), lambda b, r: (b, 0, r, 0, qcol // DIL_Q_W)),
                  pl.BlockSpec(shp + (DIL_KV_W,), lambda b, r: (b, 0, r, 0, kvcol // DIL_KV_W))],
        out_specs=[pl.BlockSpec(shp + (DIL_Q_W,), lambda b, r: (b, 0, r, 0, 0))] * 2,
        out_shape=[jax.ShapeDtypeStruct((batch, ntile, dil, blk, DIL_Q_W), BF16),
                   jax.ShapeDtypeStruct((batch, ntile, dil, blk, DIL_Q_W), F32)],
        compiler_params=pltpu.CompilerParams(dimension_semantics=("arbitrary",) * 2,
                                             vmem_limit_bytes=VMEM_LIMIT),
        name=f"dilated_attention_g{group}",
    )(view, view)
    return o.reshape(batch * seq, DIL_Q_W), lse.reshape(batch * seq, DIL_Q_W)


def _hgrn2_body(q_ref, f_ref, i_ref, g_ref, lb_ref, o_ref, *, layer, seq, chunk):
    c = chunk
    sub = 8
    npair = HG_W // LANE
    lbs = lb_ref[...].astype(F32)
    mx = jnp.max(lbs, axis=0, keepdims=True)
    ex = jnp.exp(lbs - mx)
    sm = ex / jnp.sum(ex, axis=0, keepdims=True)
    lower = jnp.maximum(jnp.sum(sm[0:layer + 1], axis=0, keepdims=True) - sm[0:1], 0.0)
    log_lb = jnp.log(lower + LB_TINY)
    log_1m = jnp.log1p(-lower)

    lane = lax.broadcasted_iota(jnp.int32, (1, LANE), 1)
    head0 = lane < HG_DIM
    ri = lax.broadcasted_iota(jnp.int32, (c, 1), 0)
    ci = lax.broadcasted_iota(jnp.int32, (1, c), 1)
    tri = jnp.where(ci <= ri, 1.0, 0.0).astype(BF16)
    di = lax.broadcasted_iota(jnp.int32, (LANE, 1), 0)
    same_head = (di >= HG_DIM) == (lane >= HG_DIM)
    ones_blk = jnp.where(same_head, 1.0, 0.0).astype(BF16)
    gcol = lax.broadcasted_iota(jnp.int32, (1, sub * c), 1)
    gsum = jnp.where(((gcol >> _log2(c)) == (ri & (sub - 1)))
                     & (((gcol & (c - 1)) >> 3) == (ri >> 3)), 1.0, 0.0).astype(BF16)
    sp = lax.broadcasted_iota(jnp.int32, (1, sub, 1), 1)
    levels = []
    w = sub
    while w < c:
        same = (ri >> _log2(2 * w)) == (ci >> _log2(2 * w))
        levels.append((w, same & ((ri & (2 * w - 1)) >= w) & ((ci & (2 * w - 1)) < w)))
        w *= 2

    def split3(x):
        hi = x.astype(BF16)
        r1 = x - hi.astype(F32)
        mid = r1.astype(BF16)
        lo = (r1 - mid.astype(F32)).astype(BF16)
        return hi, mid, lo

    def pair_chunk(q, v, x, gt, llb, l1m, state_t):
        log_sig = jnp.minimum(x, 0.0) - jnp.log1p(jnp.exp(-jnp.abs(x)))
        t2 = l1m + log_sig
        lf = jnp.maximum(llb, t2) + jnp.log1p(jnp.exp(-jnp.abs(llb - t2)))
        kk = 1.0 - jnp.exp(lf)
        hi, mid, lo = split3(lf * LOG2E)
        b = _dot(tri, hi) + _dot(tri, mid) + _dot(tri, lo)
        vb = v.astype(BF16)

        q3 = q.reshape(c // sub, sub, LANE)
        k3 = kk.reshape(c // sub, sub, LANE)
        b3 = b.reshape(c // sub, sub, LANE)
        parts = []
        for tp in range(sub):
            dec = jnp.exp2(jnp.minimum(b3[:, tp:tp + 1, :] - b3, 0.0))
            parts.append(jnp.where(sp <= tp, q3[:, tp:tp + 1, :] * k3 * dec, 0.0).reshape(c, LANE))
        wall = jnp.concatenate(parts, axis=0)
        a_rep = _dot(wall.astype(BF16), ones_blk)
        z = a_rep * jnp.concatenate([v] * sub, axis=0)
        o = _dot(gsum, z.astype(BF16))

        a0 = jnp.zeros((c, c), F32)
        a1 = jnp.zeros((c, c), F32)
        for w, lmask in levels:
            b_r = b.reshape(c // (2 * w), 2 * w, LANE)
            bnd = jnp.broadcast_to(b_r[:, w - 1:w, :], b_r.shape).reshape(c, LANE)
            qe = q * jnp.exp2(jnp.minimum(b - bnd, 0.0))
            ke = (kk * jnp.exp2(jnp.minimum(bnd - b, 0.0))).astype(BF16)
            q2 = jnp.concatenate([jnp.where(head0, qe, 0.0), jnp.where(head0, 0.0, qe)], axis=0).astype(BF16)
            a01 = _dot_nt(q2, ke)
            a0 = a0 + jnp.where(lmask, a01[0:c], 0.0)
            a1 = a1 + jnp.where(lmask, a01[c:2 * c], 0.0)
        av = _dot(jnp.concatenate([a0, a1], axis=0).astype(BF16), vb)
        o = o + jnp.where(head0, av[0:c], av[c:2 * c])

        o = o + _dot_nt((q * jnp.exp2(b)).astype(BF16), state_t.astype(BF16))
        b_last = b[c - 1:c, :]
        khat = (kk * jnp.exp2(b_last - b)).astype(BF16)
        upd = lax.dot_general(vb, khat, (((0,), (0,)), ((), ())), preferred_element_type=F32)
        state_t = jnp.exp2(b_last) * state_t + jnp.where(same_head, upd, 0.0)

        o2 = o * o
        ms0 = jnp.sum(jnp.where(head0, o2, 0.0), axis=-1, keepdims=True)
        ms1 = jnp.sum(jnp.where(head0, 0.0, o2), axis=-1, keepdims=True)
        o = o * lax.rsqrt(jnp.where(head0, ms0, ms1) * (1.0 / HG_DIM) + EPS)
        return (o * (gt * jax.nn.sigmoid(gt))).astype(BF16), state_t

    def step(ic, states):
        rows = pl.ds(pl.multiple_of(ic * c, c), c)
        out = []
        for p in range(npair):
            cols = slice(p * LANE, (p + 1) * LANE)
            o, st = pair_chunk(q_ref[0, rows, cols], i_ref[0, rows, cols], f_ref[0, rows, cols],
                               g_ref[0, rows, cols], log_lb[:, cols], log_1m[:, cols], states[p])
            o_ref[0, rows, cols] = o
            out.append(st)
        return tuple(out)

    lax.fori_loop(0, seq // c, step, tuple(jnp.zeros((LANE, LANE), F32) for _ in range(npair)),
                  unroll=min(8, seq // c))


def _hgrn2(pf, hg_lb, layer, seq, chunk):
    B = pf.shape[0]
    sec = lambda off: (lambda b: (b, 0, off // HG_W))
    return pl.pallas_call(
        functools.partial(_hgrn2_body, layer=layer, seq=seq, chunk=chunk),
        grid=(B,),
        in_specs=[pl.BlockSpec((1, seq, HG_W), sec(PF_HQ)),
                  pl.BlockSpec((1, seq, HG_W), sec(PF_HF)),
                  pl.BlockSpec((1, seq, HG_W), sec(PF_HI)),
                  pl.BlockSpec((1, seq, HG_W), sec(PF_HG)),
                  pl.BlockSpec((DEPTH, HG_W), lambda b: (0, 0))],
        out_specs=pl.BlockSpec((1, seq, HG_W), lambda b: (b, 0, 0)),
        out_shape=jax.ShapeDtypeStruct((B, seq, HG_W), BF16),
        compiler_params=pltpu.CompilerParams(dimension_semantics=("arbitrary",),
                                             vmem_limit_bytes=VMEM_LIMIT),
        name="hgrn2",
    )(pf, pf, pf, pf, hg_lb)


def _outproj_body(nsa_ref, d0_ref, d1_ref, d2_ref, l0_ref, l1_ref, l2_ref, hg_ref, h_ref,
                  w_ref, g_ref, out_ref, tok_scr, *, tm):
    def token_order(ref, slot, d):
        if d == 1:
            return ref[...].astype(F32)
        n = tm // d
        for r in range(d):
            tok_scr[slot, pl.ds(r, n, stride=d), :] = ref[r * n:(r + 1) * n, :].astype(F32)
        return tok_scr[slot]

    dils = [d for _, d in DIL_PAIRS]
    os_ = [token_order(ref, 2 * g, dils[g]) for g, ref in enumerate((d0_ref, d1_ref, d2_ref))]
    ls = [token_order(ref, 2 * g + 1, dils[g]) for g, ref in enumerate((l0_ref, l1_ref, l2_ref))]
    lm = jnp.maximum(jnp.maximum(ls[0], ls[1]), ls[2])
    es = [jnp.exp(l - lm) for l in ls]
    inv = 1.0 / (es[0] + es[1] + es[2])
    mix = jnp.concatenate([nsa_ref[...]] + [(os_[g] * (es[g] * inv)).astype(BF16) for g in range(len(DIL_PAIRS))]
                          + [hg_ref[...]], axis=1)
    out_ref[...] = h_ref[...] + _rms(_dot(mix, w_ref[...]), g_ref[...])


def _outproj(nsa, dil_o, dil_l, hg, h2, w, g, l, tm):
    T = h2.shape[0]
    row = lambda i: (i, 0)
    return pl.pallas_call(
        functools.partial(_outproj_body, tm=tm),
        grid=(T // tm,),
        in_specs=[pl.BlockSpec((tm, NSA_Q_W), row)]
        + [pl.BlockSpec((tm, DIL_Q_W), row)] * 6
        + [pl.BlockSpec((tm, HG_W), row), pl.BlockSpec((tm, D_MODEL), row),
           _layer((MIX_W, D_MODEL), l), _layer((1, D_MODEL), l)],
        out_specs=pl.BlockSpec((tm, D_MODEL), row),
        out_shape=jax.ShapeDtypeStruct((T, D_MODEL), F32),
        scratch_shapes=[pltpu.VMEM((2 * len(DIL_PAIRS), tm, DIL_Q_W), F32)],
        compiler_params=pltpu.CompilerParams(dimension_semantics=("arbitrary",),
                                             vmem_limit_bytes=VMEM_LIMIT),
        name="outproj",
    )(nsa, *dil_o, *dil_l, hg, h2, w, g)


def _mlp_body(h_ref, p_ref, gpre_ref, wup_ref, wdn_ref, gpost_ref, gple_ref, wg_ref, wp_ref, out_ref, *, fc):
    h = h_ref[...]
    hn = _rms(h, gpre_ref[...]).astype(BF16)
    acc = jnp.zeros(h.shape, F32)
    for c in range(0, D_FF, fc):
        u = jnp.maximum(_dot(hn, wup_ref[:, c:c + fc]), 0.0)
        acc = acc + _dot((u * u).astype(BF16), wdn_ref[c:c + fc, :])
    h = h + _rms(acc, gpost_ref[...])
    gate = jax.nn.sigmoid(_dot(_rms(h, gple_ref[...]).astype(BF16), wg_ref[...]))
    out_ref[...] = h + _dot(p_ref[...].astype(BF16), wp_ref[...]) * gate


def _mlp(h2, p3, gpre, wup, wdn, gpost, gple, wg, wp, l, tm):
    T = h2.shape[0]
    row = lambda i: (i, 0)
    vec = _layer((1, D_MODEL), l)

    def resident(shape):
        return pl.BlockSpec((None,) + tuple(shape), lambda i: (l, 0, 0), pipeline_mode=pl.Buffered(1))

    return pl.pallas_call(
        functools.partial(_mlp_body, fc=512),
        grid=(T // tm,),
        in_specs=[pl.BlockSpec((tm, D_MODEL), row), pl.BlockSpec((None, tm, PLE_DIM), lambda i: (l, i, 0)), vec,
                  resident(wup.shape[1:]), resident(wdn.shape[1:]), vec, vec,
                  resident(wg.shape[1:]), resident(wp.shape[1:])],
        out_specs=pl.BlockSpec((tm, D_MODEL), row),
        out_shape=jax.ShapeDtypeStruct((T, D_MODEL), F32),
        compiler_params=pltpu.CompilerParams(dimension_semantics=("arbitrary",),
                                             vmem_limit_bytes=VMEM_LIMIT),
        name="mlp_ple",
    )(h2, p3, gpre, wup, wdn, gpost, gple, wg, wp)


def _inproj_columns():
    hd_cols = np.arange(HEAD_DIM)
    ca = np.zeros((WA,), np.int64)
    for g in range(NSA_GROUP):
        for kh in range(NSA_KV_HEADS):
            o = PA_NQ + g * LANE + kh * HEAD_DIM
            ca[o:o + HEAD_DIM] = OFF_NQ + (kh * NSA_GROUP + g) * HEAD_DIM + hd_cols
    ca[PA_DQ:PA_DQ + DIL_Q_W] = OFF_DQ + np.arange(DIL_Q_W)
    for kh in range(NSA_KV_HEADS):
        for base, ko, vo in ((PA_KVS, OFF_NKS, OFF_NVS), (PA_KVW, OFF_NKW, OFF_NVW)):
            o = base + kh * LANE
            first, second = (ko, vo) if kh == 0 else (vo, ko)
            ca[o:o + HEAD_DIM] = first + kh * HEAD_DIM + hd_cols
            ca[o + HEAD_DIM:o + LANE] = second + kh * HEAD_DIM + hd_cols

    def dil_kv(dst, base, g):
        for i in range(DIL_HPG):
            hd = g * DIL_HPG + i
            o = base + i * LANE
            first, second = (OFF_DK, OFF_DV) if i == 0 else (OFF_DV, OFF_DK)
            dst[o:o + HEAD_DIM] = first + hd * HEAD_DIM + hd_cols
            dst[o + HEAD_DIM:o + LANE] = second + hd * HEAD_DIM + hd_cols

    dil_kv(ca, PA_DKV, 0)
    cd = np.zeros((len(DIL_PAIRS) - 1, WD), np.int64)
    for g in range(1, len(DIL_PAIRS)):
        dil_kv(cd[g - 1], PD_KV, g)
        cd[g - 1, PD_Q:PD_Q + DIL_Q_W] = OFF_DQ + g * DIL_Q_W + np.arange(DIL_Q_W)
    cf = np.full((WF,), IN_TOTAL, np.int64)
    for dst, src in ((PF_HQ, OFF_HQ), (PF_HF, OFF_HF), (PF_HI, OFF_HI), (PF_HG, OFF_HG)):
        cf[dst:dst + HG_W] = src + np.arange(HG_W)
    cf[PF_GATE:PF_GATE + 3 * NSA_HEADS] = OFF_GATE + np.arange(3 * NSA_HEADS)
    cc = np.concatenate([OFF_NKC + np.arange(NSA_KV_W), OFF_NVC + np.arange(NSA_KV_W)])
    return ca, cd, cf, cc


def _take(w, idx, axis):
    idx = np.asarray(idx)
    n_src = w.shape[axis]
    bounds = [0] + [i for i in range(1, len(idx))
                    if idx[i] != idx[i - 1] + 1 and not (idx[i] == n_src and idx[i - 1] == n_src)]
    bounds.append(len(idx))
    pieces = []
    for a, b in zip(bounds[:-1], bounds[1:]):
        src = int(idx[a])
        if src == n_src:
            shape = list(w.shape)
            shape[axis] = b - a
            pieces.append(jnp.zeros(shape, w.dtype))
        else:
            pieces.append(lax.slice_in_dim(w, src, src + (b - a), axis=axis))
    return jnp.concatenate(pieces, axis=axis)


def _outproj_rows():
    hd_cols = np.arange(HEAD_DIM)
    rows = np.arange(MIX_W)
    for g in range(NSA_GROUP):
        for slot, kh in enumerate((1, 0)):
            o = g * LANE + slot * HEAD_DIM
            rows[o:o + HEAD_DIM] = (kh * NSA_GROUP + g) * HEAD_DIM + hd_cols
    for g in range(len(DIL_PAIRS)):
        for slot, i in enumerate((1, 0)):
            o = NSA_Q_W + g * DIL_Q_W + slot * HEAD_DIM
            rows[o:o + HEAD_DIM] = NSA_Q_W + (g * DIL_HPG + i) * HEAD_DIM + hd_cols
    return rows


def kernel(x, p, w_in, w_out, cmp_pos, cmp_w1, cmp_w2, hg_lb, g_pre_mix, g_post_mix,
           g_pre_mlp, g_post_mlp, w_up, w_down, g_ple, w_ple_gate, w_ple_proj):
    B, S, D = x.shape
    T = B * S
    tm = 512 if S % 512 == 0 else 256
    tm_mlp = 512
    tq_nsa, tk_nsa = 256, 256
    tq_dil = 128
    hg_chunk = 128
    assert D == D_MODEL and S % tm == 0 and S // SLC_BLOCK <= LANE
    nsa_slopes, dil_slopes = _alibi_slopes()
    ca, cd, cf, cc = _inproj_columns()
    n_cmp = (S - CMP_BLOCK) // CMP_STRIDE + 1
    nr = S // CMP_STRIDE
    ncp = -(-nr // LANE) * LANE

    key_scale = np.ones((IN_TOTAL,), np.float32)
    key_scale[OFF_NKS:OFF_NKS + NSA_KV_W] = LOG2E
    key_scale[OFF_NKW:OFF_NKW + NSA_KV_W] = LOG2E
    w_in16 = (w_in * key_scale).astype(BF16)
    wa = _take(w_in16, ca, 2)
    wd_in = jnp.stack([_take(w_in16, cd[g], 2) for g in range(cd.shape[0])], axis=1)
    wf = _take(w_in16, cf, 2)
    wc = _take(w_in16, cc, 2)
    pos8 = jnp.pad(cmp_pos.reshape(DEPTH, 2, 1, CMP_BLOCK * HEAD_DIM), ((0, 0), (0, 0), (0, 7), (0, 0))).astype(BF16)
    lo_pad, hi_pad = ((0, 0), (0, 0), (0, HEAD_DIM)), ((0, 0), (0, 0), (HEAD_DIM, 0))
    w2p = jnp.stack([jnp.stack([jnp.pad(cmp_w2[:, 0], lo_pad), jnp.pad(cmp_w2[:, 1], hi_pad)], axis=1),
                     jnp.stack([jnp.pad(cmp_w2[:, 0], hi_pad), jnp.pad(cmp_w2[:, 1], lo_pad)], axis=1)],
                    axis=1).astype(BF16)
    w1 = cmp_w1.astype(BF16)
    wo = _take(w_out.astype(BF16), _outproj_rows(), 1)
    wup, wdn = w_up.astype(BF16), w_down.astype(BF16)
    wg, wp = w_ple_gate.astype(BF16), w_ple_proj.astype(BF16)
    vec = lambda g: g.reshape(DEPTH, 1, D)
    g_pre_mix, g_post_mix, g_pre_mlp, g_post_mlp, g_ple = map(vec, (g_pre_mix, g_post_mix, g_pre_mlp, g_post_mlp, g_ple))
    p3 = p.reshape(DEPTH, T, PLE_DIM)

    h = x.reshape(T, D)
    for l in range(DEPTH):
        pa, pf, pd1, pd2, *xs = _inproj(h, g_pre_mix, wa, wf, wd_in, wc, l, tm)
        pa3 = pa.reshape(B, S, WA)
        pf3 = pf.reshape(B, S, WF)
        cmp_kv = _compress([a.reshape(B, nr, CMP_ROW_W) for a in xs], pos8, w1, w2p, l, ncp, n_cmp)
        o_nsa = _nsa(pa3, pf3, cmp_kv, S, tq_nsa, tk_nsa, nsa_slopes).reshape(T, NSA_Q_W)
        dil = [_dilated(pa, PA_DQ, PA_DKV, B, S, tm, 0, dil_slopes, tq_dil),
               _dilated(pd1, PD_Q, PD_KV, B, S, tm, 1, dil_slopes, tq_dil),
               _dilated(pd2, PD_Q, PD_KV, B, S, tm, 2, dil_slopes, tq_dil)]
        o_hg = _hgrn2(pf3, hg_lb, l, S, hg_chunk).reshape(T, HG_W)
        h = _outproj(o_nsa, [d[0] for d in dil], [d[1] for d in dil], o_hg, h, wo, g_post_mix, l, tm)
        h = _mlp(h, p3, g_pre_mlp, wup, wdn, g_post_mlp, g_ple, wg, wp, l, tm_mlp)
    return h.reshape(B, S, D)
```

```python
import functools

import numpy as np
import jax
import jax.numpy as jnp
from jax import lax
from jax.experimental import pallas as pl
from jax.experimental.pallas import tpu as pltpu

F32 = jnp.float32
BF16 = jnp.bfloat16

D_MODEL = 1024
DEPTH = 2
HEAD_DIM = 64
NSA_HEADS = 6
NSA_KV_HEADS = 2
NSA_GROUP = NSA_HEADS // NSA_KV_HEADS
CMP_BLOCK = 32
CMP_STRIDE = 16
CMP_HIDDEN = 256
SLC_BLOCK = 64
SLC_TOPN = 8
WIN = 512
FORCE_SCORE = 1e9
DIL_PAIRS = ((128, 1), (512, 4), (2048, 16))
DIL_HPG = 2
DIL_HEADS = DIL_HPG * len(DIL_PAIRS)
HG_HEADS = 4
HG_DIM = 64
LB_TINY = 1e-30
D_FF = 4 * D_MODEL
PLE_DIM = 256
EPS = 1e-6
NEG = -1e30
LOG2E = 1.4426950408889634

NSA_Q_W = NSA_HEADS * HEAD_DIM
NSA_KV_W = NSA_KV_HEADS * HEAD_DIM
DIL_W = DIL_HEADS * HEAD_DIM
HG_W = HG_HEADS * HG_DIM
MIX_W = NSA_Q_W + DIL_W + HG_W
IN_WIDTHS = (NSA_Q_W,) + (NSA_KV_W,) * 6 + (3 * NSA_HEADS,) + (DIL_W,) * 3 + (HG_W,) * 4
IN_TOTAL = sum(IN_WIDTHS)
IN_OFF = tuple(int(v) for v in np.cumsum((0,) + IN_WIDTHS))
(OFF_NQ, OFF_NKC, OFF_NVC, OFF_NKS, OFF_NVS, OFF_NKW, OFF_NVW, OFF_GATE,
 OFF_DQ, OFF_DK, OFF_DV, OFF_HQ, OFF_HF, OFF_HI, OFF_HG) = IN_OFF[:-1]

LANE = 128
VMEM_LIMIT = 56 * 1024 * 1024

DIL_Q_W = DIL_HPG * HEAD_DIM
DIL_KV_W = DIL_HPG * LANE
PA_NQ = 0
PA_DQ = PA_NQ + NSA_Q_W
PA_KVS = PA_DQ + DIL_Q_W
PA_KVW = PA_KVS + NSA_KV_HEADS * LANE
PA_DKV = PA_KVW + NSA_KV_HEADS * LANE
WA = PA_DKV + DIL_KV_W
PD_KV, PD_Q = 0, DIL_KV_W
WD = DIL_KV_W + DIL_Q_W
PF_HQ, PF_HF, PF_HI, PF_HG, PF_GATE = 0, HG_W, 2 * HG_W, 3 * HG_W, 4 * HG_W
WF = PF_GATE + LANE
WC = 2 * NSA_KV_W
CMP_ROW_W = CMP_STRIDE * HEAD_DIM


def _dot(a, b):
    return jnp.dot(a, b, preferred_element_type=F32)


def _dot_nt(a, b):
    return lax.dot_general(a, b, (((1,), (1,)), ((), ())), preferred_element_type=F32)


def _rms(x, g):
    return x * lax.rsqrt(jnp.mean(x * x, axis=-1, keepdims=True) + EPS) * g


def _log2(n):
    l = int(n).bit_length() - 1
    assert (1 << l) == n, n
    return l


def _alibi_slopes():
    n = NSA_HEADS + DIL_HEADS
    s = 2.0 ** (-8.0 * np.arange(1, n + 1) / n)
    quads = s.reshape(-1, 4)
    nsa = [float(np.float32(v)) for v in quads[:, 2:].reshape(-1)]
    dil = [float(np.float32(v)) for v in quads[:, :2].reshape(-1)]
    return nsa, dil


def _layer(shape, l):
    zeros = (0,) * len(shape)
    return pl.BlockSpec((None,) + tuple(shape), lambda *_: (l,) + zeros)


def _inproj_body(x_ref, g_ref, wa_ref, wf_ref, wd_ref, wc_ref,
                 oa_ref, of_ref, od1_ref, od2_ref, xk0_ref, xk1_ref, xv0_ref, xv1_ref, hn_scr, *, tm):
    hn32 = _rms(x_ref[...], g_ref[...])
    hn = hn32.astype(BF16)
    for c in range(0, WA, 256):
        oa_ref[:, c:c + 256] = _dot(hn, wa_ref[:, c:c + 256]).astype(BF16)
    for c in range(0, WF, 256):
        c1 = min(c + 256, WF)
        of_ref[:, c:c1] = _dot(hn, wf_ref[:, c:c1])
    nlb = D_MODEL // LANE
    for c in range(nlb):
        hn_scr[c] = hn32[:, c * LANE:(c + 1) * LANE]

    def by_residue(d):
        parts = [jnp.concatenate([hn_scr[c, pl.ds(r, tm // d, stride=d), :] for c in range(nlb)], axis=1)
                 for r in range(d)]
        return jnp.concatenate(parts, axis=0).astype(BF16)

    for gi, o_ref in ((1, od1_ref), (2, od2_ref)):
        hp = by_residue(DIL_PAIRS[gi][1])
        for c0, c1 in ((0, DIL_KV_W), (DIL_KV_W, WD)):
            o_ref[:, c0:c1] = _dot(hp, wd_ref[gi - 1, :, c0:c1]).astype(BF16)
    assert DIL_PAIRS[2][1] == CMP_STRIDE
    cc = _dot(hp, wc_ref[...])
    nr = tm // CMP_STRIDE
    for j in range(CMP_STRIDE):
        for i, x_ref in enumerate((xk0_ref, xk1_ref, xv0_ref, xv1_ref)):
            x_ref[:, j * HEAD_DIM:(j + 1) * HEAD_DIM] = cc[j * nr:(j + 1) * nr, i * HEAD_DIM:(i + 1) * HEAD_DIM]


def _inproj(x2, g, wa, wf, wd, wc, l, tm):
    T = x2.shape[0]
    row = lambda i: (i, 0)
    nr = tm // CMP_STRIDE
    return pl.pallas_call(
        functools.partial(_inproj_body, tm=tm),
        grid=(T // tm,),
        in_specs=[pl.BlockSpec((tm, D_MODEL), row), _layer((1, D_MODEL), l),
                  _layer((D_MODEL, WA), l), _layer((D_MODEL, WF), l),
                  _layer((2, D_MODEL, WD), l), _layer((D_MODEL, WC), l)],
        out_specs=[pl.BlockSpec((tm, WA), row), pl.BlockSpec((tm, WF), row),
                   pl.BlockSpec((tm, WD), row), pl.BlockSpec((tm, WD), row),
                   ] + [pl.BlockSpec((nr, CMP_ROW_W), row)] * 4,
        out_shape=[jax.ShapeDtypeStruct((T, WA), BF16), jax.ShapeDtypeStruct((T, WF), F32),
                   jax.ShapeDtypeStruct((T, WD), BF16), jax.ShapeDtypeStruct((T, WD), BF16),
                   ] + [jax.ShapeDtypeStruct((T // CMP_STRIDE, CMP_ROW_W), F32)] * 4,
        scratch_shapes=[pltpu.VMEM((D_MODEL // LANE, tm, LANE), F32)],
        compiler_params=pltpu.CompilerParams(dimension_semantics=("arbitrary",),
                                             vmem_limit_bytes=VMEM_LIMIT),
        name="inproj",
    )(x2, g, wa, wf, wd, wc)


def _compress_body(xk0_ref, xk1_ref, xv0_ref, xv1_ref, pos_ref, w1_ref, w2_ref, out_ref, *, n_cmp):
    nr = xk0_ref.shape[1]
    half = (CMP_BLOCK // 2) * HEAD_DIM
    rows = lax.broadcasted_iota(jnp.int32, (nr, 1), 0)
    out_ref[...] = jnp.zeros(out_ref.shape, out_ref.dtype)
    srcs = ((xk0_ref, xv0_ref), (xk1_ref, xv1_ref))
    for h in range(NSA_KV_HEADS):
        acc = jnp.zeros((nr, LANE), F32)
        for ten in range(2):
            x = srcs[h][ten][0].astype(BF16)
            first = _dot(x, w1_ref[ten, 0:half, :])
            second = _dot(x, w1_ref[ten, half:2 * half, :])
            posb = _dot(pos_ref[ten], w1_ref[ten])[0:1, :]
            hid = first + pltpu.roll(second, nr - 1, 0) + posb
            act = hid * jax.nn.sigmoid(hid)
            acc = acc + _dot(act.astype(BF16), w2_ref[h, ten])
        acc = jnp.where(rows < n_cmp, acc, 0.0)
        out_ref[0, 0:nr, h * LANE:(h + 1) * LANE] = acc.astype(BF16)


def _compress(xs, pos8, w1, w2p, l, ncp, n_cmp):
    B, nr, kw = xs[0].shape
    seq = lambda b: (b, 0, 0)
    return pl.pallas_call(
        functools.partial(_compress_body, n_cmp=n_cmp),
        grid=(B,),
        in_specs=[pl.BlockSpec((1, nr, kw), seq)] * 4
        + [_layer(pos8.shape[1:], l), _layer(w1.shape[1:], l), _layer(w2p.shape[1:], l)],
        out_specs=pl.BlockSpec((1, ncp, NSA_KV_HEADS * LANE), seq),
        out_shape=jax.ShapeDtypeStruct((B, ncp, NSA_KV_HEADS * LANE), BF16),
        compiler_params=pltpu.CompilerParams(dimension_semantics=("arbitrary",),
                                             vmem_limit_bytes=VMEM_LIMIT),
        name="nsa_compress",
    )(*xs, pos8, w1, w2p)


NSA_PAT_BLK = HEAD_DIM
NSA_PAT_POS = HEAD_DIM + 32


def _nsa_pattern(seq):
    pos = np.arange(seq)
    pat = np.zeros((seq, LANE), np.float32)
    pat[pos, NSA_PAT_BLK + pos // SLC_BLOCK] = 1.0
    pat[:, NSA_PAT_POS:NSA_PAT_POS + 3] = (SLC_BLOCK * (pos // SLC_BLOCK))[:, None]
    pat[:, NSA_PAT_POS + 3:NSA_PAT_POS + 6] = (pos % SLC_BLOCK)[:, None]
    return jnp.asarray(np.stack([pat, np.roll(pat, HEAD_DIM, axis=1)]), BF16)


def _nsa_slope_rows(nsa_slopes):
    rows = np.zeros((NSA_KV_HEADS, 8, LANE), np.float32)
    for kh in range(NSA_KV_HEADS):
        for g in range(NSA_GROUP):
            rest = np.float32(nsa_slopes[kh * NSA_GROUP + g])
            for part in range(3):
                piece = np.float32(np.asarray(rest, dtype=BF16))
                rows[kh, g, NSA_PAT_POS + part] = piece
                rows[kh, g, NSA_PAT_POS + 3 + part] = piece
                rest = np.float32(rest - piece)
    rows[1] = np.roll(rows[1], HEAD_DIM, axis=1)
    return jnp.asarray(rows)


def _nsa_body(q_ref, kvs_ref, kvw_ref, kvc_ref, gate_ref, pat_ref, sl_ref, out_ref,
              ks_scr, vs_scr, kw_scr, vw_scr, s_scr, *, seq, tq, tk, n_cmp, n_slc, k_top, slopes):
    qi = pl.program_id(1)
    G = NSA_GROUP
    KH = NSA_KV_HEADS
    t0 = qi * tq
    ncp = kvc_ref.shape[1]
    ltk = _log2(tk)
    lane = lax.broadcasted_iota(jnp.int32, (1, LANE), 1)
    lo_half = lane < HEAD_DIM
    hi_half = lane >= HEAD_DIM
    mine = (lo_half, hi_half)
    other = (hi_half, lo_half)

    @pl.when(qi == 0)
    def _():
        one = jnp.ones((seq, LANE), BF16)
        for kh in range(KH):
            cols = slice(kh * LANE, (kh + 1) * LANE)
            kvs = kvs_ref[0, :, cols]
            kvw = kvw_ref[0, :, cols]
            ks_scr[kh] = jnp.where(mine[kh], kvs, pat_ref[kh])
            kw_scr[kh] = jnp.where(mine[kh], kvw, pat_ref[kh])
            vs_scr[kh] = jnp.where(mine[kh], one, kvs)
            vw_scr[kh] = jnp.where(mine[kh], one, kvw)

    ii = lax.broadcasted_iota(jnp.int32, (tq, 1), 0)
    row_t = t0 + ii
    nn = lax.broadcasted_iota(jnp.int32, (1, ncp), 1)
    maskc = ((nn * CMP_STRIDE + (CMP_BLOCK - 1)) <= row_t) & (nn < n_cmp)
    absd = jnp.abs(row_t.astype(F32) - (nn.astype(F32) * CMP_STRIDE + 0.5 * (CMP_BLOCK - 1)))
    mi = lax.broadcasted_iota(jnp.int32, (LANE, 1), 0)
    ov_t = ((nn * CMP_STRIDE < mi * SLC_BLOCK + SLC_BLOCK) & (nn * CMP_STRIDE + CMP_BLOCK > mi * SLC_BLOCK)
            & (nn < n_cmp) & (mi < n_slc))
    ov_t = jnp.where(ov_t, 1.0, 0.0).astype(BF16)
    nsp = -(-n_slc // 8) * 8
    mi_s = mi[0:nsp]
    cur_l = (t0 + lax.broadcasted_iota(jnp.int32, (1, tq), 1)) >> _log2(SLC_BLOCK)
    in_rng = mi_s < n_slc
    valid = (mi_s <= cur_l) & in_rng
    forced = (mi_s == 0) | (mi_s == cur_l) | (mi_s == cur_l - 1)

    def select(kh):
        qs = [q_ref[0, :, g * LANE:(g + 1) * LANE] * (HEAD_DIM ** -0.5) for g in range(G)]
        kvc = kvc_ref[0, :, kh * LANE:(kh + 1) * LANE]
        s_c = _dot_nt(jnp.concatenate([jnp.where(mine[kh], q, 0.0) for q in qs], axis=0), kvc)
        ps = []
        for g in range(G):
            s = jnp.where(maskc, s_c[g * tq:(g + 1) * tq] - slopes[kh * G + g] * absd, NEG)
            m = jnp.max(s, axis=-1, keepdims=True)
            e = jnp.where(maskc, jnp.exp(s - m), 0.0)
            ps.append(e / jnp.maximum(jnp.sum(e, axis=-1, keepdims=True), 1e-30))
        o_cmp = _dot(jnp.concatenate(ps, axis=0).astype(BF16), kvc)
        psum = ps[0] + ps[1] + ps[2]
        p_hi = psum.astype(BF16)
        p_lo = (psum - p_hi.astype(F32)).astype(BF16)
        imp_t = _dot_nt(ov_t, p_hi) + _dot_nt(ov_t, p_lo)
        score = jnp.where(valid, jnp.where(forced, FORCE_SCORE, imp_t[0:nsp]), -FORCE_SCORE)
        score = jnp.where(in_rng, score, -3.0 * FORCE_SCORE)
        rank = jnp.zeros((nsp, tq), F32)
        for mp in range(n_slc):
            row = score[mp:mp + 1, :]
            beats = (row > score) | ((row == score) & (mp < mi_s))
            rank = rank + jnp.where(beats, 1.0, 0.0)
        neg_t = jnp.where((rank < k_top) & valid, 0.0, NEG)
        lead = NSA_PAT_BLK if kh == 0 else 0
        pieces = [jnp.zeros((lead, tq), F32), neg_t, jnp.zeros((LANE - lead - nsp, tq), F32)]
        neg_t = jnp.concatenate([x for x in pieces if x.shape[0]], axis=0)
        return qs, o_cmp, neg_t.T

    def extended(kh, qs, neg):
        out = []
        for g in range(G):
            extra = sl_ref[kh, g:g + 1, :] + (0.0 if neg is None else neg)
            out.append(jnp.where(mine[kh], qs[g], extra.astype(BF16)))
        return jnp.concatenate(out, axis=0)

    def normalize(acc, kh):
        den = pltpu.roll(acc, HEAD_DIM, 1)
        return jnp.where(other[kh], acc / jnp.where(other[kh], den, 1.0), 0.0)

    n_kt = (t0 + tq + tk - 1) >> ltk
    nwt = min(-(-(WIN + tq) // tk), seq // tk)
    wk = nwt * tk
    kt0 = jnp.clip(n_kt - nwt, 0, seq // tk - nwt)
    k0 = pl.multiple_of(kt0 * tk, tk)
    span = pl.ds(k0, wk)
    rel = row_t - (k0 + lax.broadcasted_iota(jnp.int32, (1, wk), 1))
    ok_causal = jnp.where(rel >= 0, 0.0, NEG)
    ok_win = jnp.where(rel < WIN, ok_causal, NEG)

    def masked(s, bias):
        return (s.reshape(G, tq, wk) + bias[None]).reshape(G * tq, wk)

    o_cmp, o_win, q_sel, s_near, m_near = [], [], [], [], []
    picked = [select(kh) for kh in range(KH)]
    for kh in range(KH):
        qs, oc, neg = picked[kh]
        o_cmp.append(oc)
        q_sel.append(extended(kh, qs, neg))
        s_near.append(masked(_dot_nt(q_sel[kh], ks_scr[kh, span, :]), ok_causal))
        m_near.append(jnp.max(s_near[kh], axis=-1, keepdims=True))
    for kh in range(KH):
        s = masked(_dot_nt(extended(kh, picked[kh][0], None), kw_scr[kh, span, :]), ok_win)
        p = jnp.exp2(s - jnp.max(s, axis=-1, keepdims=True))
        o_win.append(normalize(_dot(p.astype(BF16), vw_scr[kh, span, :]), kh))

    def far_scores(kt, m_acc):
        out = []
        for kh in range(KH):
            s = _dot_nt(q_sel[kh], ks_scr[kh, pl.ds(pl.multiple_of(kt * tk, tk), tk), :])
            s_scr[kh, kt] = s
            m = m_acc[kh]
            for c in range(0, tk, LANE):
                m = jnp.maximum(m, s[:, c:c + LANE])
            out.append(m)
        return tuple(out)

    m_far = lax.fori_loop(0, kt0, far_scores, tuple(jnp.full((G * tq, LANE), NEG, F32) for _ in range(KH)))
    m_row = [jnp.maximum(jnp.max(m_far[kh], axis=-1, keepdims=True), m_near[kh]) for kh in range(KH)]
    acc0 = tuple(_dot(jnp.exp2(s_near[kh] - m_row[kh]).astype(BF16), vs_scr[kh, span, :]) for kh in range(KH))

    def far_accum(kt, acc):
        out = []
        for kh in range(KH):
            p = jnp.exp2(s_scr[kh, kt] - m_row[kh])
            out.append(acc[kh] + _dot(p.astype(BF16), vs_scr[kh, pl.ds(pl.multiple_of(kt * tk, tk), tk), :]))
        return tuple(out)

    acc = lax.fori_loop(0, kt0, far_accum, acc0)

    gates = jax.nn.sigmoid(gate_ref[0])
    heads = [[], []]
    for kh in range(KH):
        o_sel = normalize(acc[kh], kh)
        for g in range(G):
            hd = kh * G + g
            rows = slice(g * tq, (g + 1) * tq)
            heads[kh].append(gates[:, hd * 3:hd * 3 + 1] * o_cmp[kh][rows]
                             + gates[:, hd * 3 + 1:hd * 3 + 2] * o_sel[rows]
                             + gates[:, hd * 3 + 2:hd * 3 + 3] * o_win[kh][rows])
    for g in range(G):
        out_ref[0, :, g * LANE:(g + 1) * LANE] = jnp.where(lo_half, heads[1][g], heads[0][g]).astype(BF16)


def _nsa(pa, pf, cmp_kv, seq, tq, tk, nsa_slopes):
    B = pa.shape[0]
    n_cmp = (seq - CMP_BLOCK) // CMP_STRIDE + 1
    n_slc = seq // SLC_BLOCK
    ncp = cmp_kv.shape[1]
    kvw = NSA_KV_HEADS * LANE
    pat = _nsa_pattern(seq)
    sl_rows = _nsa_slope_rows([s * LOG2E for s in nsa_slopes])
    body = functools.partial(_nsa_body, seq=seq, tq=tq, tk=tk, n_cmp=n_cmp, n_slc=n_slc,
                             k_top=min(SLC_TOPN, n_slc), slopes=tuple(nsa_slopes))
    return pl.pallas_call(
        body,
        grid=(B, seq // tq),
        in_specs=[pl.BlockSpec((1, tq, NSA_Q_W), lambda b, i: (b, i, PA_NQ // NSA_Q_W)),
                  pl.BlockSpec((1, seq, kvw), lambda b, i: (b, 0, PA_KVS // kvw)),
                  pl.BlockSpec((1, seq, kvw), lambda b, i: (b, 0, PA_KVW // kvw)),
                  pl.BlockSpec((1, ncp, kvw), lambda b, i: (b, 0, 0)),
                  pl.BlockSpec((1, tq, LANE), lambda b, i: (b, i, PF_GATE // LANE)),
                  pl.BlockSpec((NSA_KV_HEADS, seq, LANE), lambda b, i: (0, 0, 0)),
                  pl.BlockSpec((NSA_KV_HEADS, 8, LANE), lambda b, i: (0, 0, 0))],
        out_specs=pl.BlockSpec((1, tq, NSA_Q_W), lambda b, i: (b, i, 0)),
        out_shape=jax.ShapeDtypeStruct((B, seq, NSA_Q_W), BF16),
        scratch_shapes=[pltpu.VMEM((NSA_KV_HEADS, seq, LANE), BF16)] * 4
        + [pltpu.VMEM((NSA_KV_HEADS, seq // tk, NSA_GROUP * tq, tk), F32)],
        compiler_params=pltpu.CompilerParams(dimension_semantics=("arbitrary",) * 2,
                                             vmem_limit_bytes=VMEM_LIMIT),
        name="nsa_attention",
    )(pa, pa, pa, cmp_kv, pf, pat, sl_rows)


def _dilated_body(q_ref, kv_ref, o_ref, lse_ref, *, nres, blk, ls, dil, win_keys, slopes, tq):
    ii = lax.broadcasted_iota(jnp.int32, (tq, 1), 0)
    jj = lax.broadcasted_iota(jnp.int32, (1, 2 * tq), 1)
    rel2 = ii - jj + tq
    ok2 = (rel2 >= 0) & (rel2 <= win_keys)
    lo_half = lax.broadcasted_iota(jnp.int32, (1, LANE), 1) < HEAD_DIM
    mine = (lo_half, jnp.logical_not(lo_half))
    bias2 = [(slopes[h] * float(dil)) * rel2.astype(F32) for h in range(DIL_HPG)]

    def rows(ref, r, t, lanes):
        if blk >= tq:
            a, b = divmod(t * tq, blk)
            return ref[0, a, r, b:b + tq, lanes]
        n = tq // blk
        return ref[0, t * n:(t + 1) * n, r, :, lanes].reshape(tq, LANE)

    def put(ref, r, t, val):
        if blk >= tq:
            a, b = divmod(t * tq, blk)
            ref[0, a, r, b:b + tq, :] = val
        else:
            n = tq // blk
            ref[0, t * n:(t + 1) * n, r, :, :] = val.reshape(n, blk, LANE)

    for r in range(nres):
        for t in range(ls // tq):
            q_pair = rows(q_ref, r, t, slice(0, LANE))
            outs, lses = [], []
            for h in range(DIL_HPG):
                lanes = slice(h * LANE, (h + 1) * LANE)
                qt = jnp.where(mine[h], q_pair, 0.0) * (HEAD_DIM ** -0.5)
                if t == 0:
                    kv = jnp.concatenate([rows(kv_ref, r, 0, lanes)] * 2, axis=0)
                    s = jnp.where(ok2 & (jj >= tq), _dot_nt(qt, kv) - bias2[h], NEG)
                else:
                    kv = jnp.concatenate([rows(kv_ref, r, t - 1, lanes), rows(kv_ref, r, t, lanes)], axis=0)
                    s = jnp.where(ok2, _dot_nt(qt, kv) - bias2[h], NEG)
                m = jnp.max(s, axis=-1, keepdims=True)
                e = jnp.exp(s - m)
                l = jnp.sum(e, axis=-1, keepdims=True)
                outs.append(_dot(e.astype(BF16), kv) / l)
                lses.append(jnp.broadcast_to(m + jnp.log(l), (tq, LANE)))
            put(o_ref, r, t, jnp.where(lo_half, outs[1], outs[0]).astype(BF16))
            put(lse_ref, r, t, jnp.where(lo_half, lses[1], lses[0]))


def _dilated(src, qcol, kvcol, batch, seq, tm, group, slopes, tq):
    win, dil = DIL_PAIRS[group]
    ls = seq // dil
    tq = min(tq, ls)
    if dil == 1:
        ntile, blk = 1, seq
    else:
        ntile, blk = seq // tm, tm // dil
    nres = max(1, min(dil, 16 * tq // ls))
    view = src.reshape(batch, ntile, dil, blk, src.shape[1])
    body = functools.partial(_dilated_body, nres=nres, blk=blk, ls=ls, dil=dil, win_keys=win // dil,
                             slopes=tuple(slopes[group * DIL_HPG:(group + 1) * DIL_HPG]), tq=tq)
    shp = (1, ntile, nres, blk)
    o, lse = pl.pallas_call(
        body,
        grid=(batch, dil // nres),
        in_specs=[pl.BlockSpec(shp + (DIL_Q_W,), lambda b, r: (b, 0, r, 0, qcol // DIL_Q_W)),
                  pl.BlockSpec(shp + (DIL_KV_W,), lambda b, r: (b, 0, r, 0, kvcol // DIL_KV_W))],
        out_specs=[pl.BlockSpec(shp + (DIL_Q_W,), lambda b, r: (b, 0, r, 0, 0))] * 2,
        out_shape=[jax.ShapeDtypeStruct((batch, ntile, dil, blk, DIL_Q_W), BF16),
                   jax.ShapeDtypeStruct((batch, ntile, dil, blk, DIL_Q_W), F32)],
        compiler_params=pltpu.CompilerParams(dimension_semantics=("arbitrary",) * 2,
                                             vmem_limit_bytes=VMEM_LIMIT),
        name=f"dilated_attention_g{group}",
    )(view, view)
    return o.reshape(batch * seq, DIL_Q_W), lse.reshape(batch * seq, DIL_Q_W)


def _hgrn2_body(q_ref, f_ref, i_ref, g_ref, lb_ref, o_ref, *, layer, seq, chunk):
    c = chunk
    sub = 8
    npair = HG_W // LANE
    lbs = lb_ref[...].astype(F32)
    mx = jnp.max(lbs, axis=0, keepdims=True)
    ex = jnp.exp(lbs - mx)
    sm = ex / jnp.sum(ex, axis=0, keepdims=True)
    lower = jnp.maximum(jnp.sum(sm[0:layer + 1], axis=0, keepdims=True) - sm[0:1], 0.0)
    log_lb = jnp.log(lower + LB_TINY)
    log_1m = jnp.log1p(-lower)

    lane = lax.broadcasted_iota(jnp.int32, (1, LANE), 1)
    head0 = lane < HG_DIM
    ri = lax.broadcasted_iota(jnp.int32, (c, 1), 0)
    ci = lax.broadcasted_iota(jnp.int32, (1, c), 1)
    tri = jnp.where(ci <= ri, 1.0, 0.0).astype(BF16)
    di = lax.broadcasted_iota(jnp.int32, (LANE, 1), 0)
    same_head = (di >= HG_DIM) == (lane >= HG_DIM)
    ones_blk = jnp.where(same_head, 1.0, 0.0).astype(BF16)
    gcol = lax.broadcasted_iota(jnp.int32, (1, sub * c), 1)
    gsum = jnp.where(((gcol >> _log2(c)) == (ri & (sub - 1)))
                     & (((gcol & (c - 1)) >> 3) == (ri >> 3)), 1.0, 0.0).astype(BF16)
    sp = lax.broadcasted_iota(jnp.int32, (1, sub, 1), 1)
    levels = []
    w = sub
    while w < c:
        same = (ri >> _log2(2 * w)) == (ci >> _log2(2 * w))
        levels.append((w, same & ((ri & (2 * w - 1)) >= w) & ((ci & (2 * w - 1)) < w)))
        w *= 2

    def split3(x):
        hi = x.astype(BF16)
        r1 = x - hi.astype(F32)
        mid = r1.astype(BF16)
        lo = (r1 - mid.astype(F32)).astype(BF16)
        return hi, mid, lo

    def pair_chunk(q, v, x, gt, llb, l1m, state_t):
        log_sig = jnp.minimum(x, 0.0) - jnp.log1p(jnp.exp(-jnp.abs(x)))
        t2 = l1m + log_sig
        lf = jnp.maximum(llb, t2) + jnp.log1p(jnp.exp(-jnp.abs(llb - t2)))
        kk = 1.0 - jnp.exp(lf)
        hi, mid, lo = split3(lf * LOG2E)
        b = _dot(tri, hi) + _dot(tri, mid) + _dot(tri, lo)
        vb = v.astype(BF16)

        q3 = q.reshape(c // sub, sub, LANE)
        k3 = kk.reshape(c // sub, sub, LANE)
        b3 = b.reshape(c // sub, sub, LANE)
        parts = []
        for tp in range(sub):
            dec = jnp.exp2(jnp.minimum(b3[:, tp:tp + 1, :] - b3, 0.0))
            parts.append(jnp.where(sp <= tp, q3[:, tp:tp + 1, :] * k3 * dec, 0.0).reshape(c, LANE))
        wall = jnp.concatenate(parts, axis=0)
        a_rep = _dot(wall.astype(BF16), ones_blk)
        z = a_rep * jnp.concatenate([v] * sub, axis=0)
        o = _dot(gsum, z.astype(BF16))

        a0 = jnp.zeros((c, c), F32)
        a1 = jnp.zeros((c, c), F32)
        for w, lmask in levels:
            b_r = b.reshape(c // (2 * w), 2 * w, LANE)
            bnd = jnp.broadcast_to(b_r[:, w - 1:w, :], b_r.shape).reshape(c, LANE)
            qe = q * jnp.exp2(jnp.minimum(b - bnd, 0.0))
            ke = (kk * jnp.exp2(jnp.minimum(bnd - b, 0.0))).astype(BF16)
            q2 = jnp.concatenate([jnp.where(head0, qe, 0.0), jnp.where(head0, 0.0, qe)], axis=0).astype(BF16)
            a01 = _dot_nt(q2, ke)
            a0 = a0 + jnp.where(lmask, a01[0:c], 0.0)
            a1 = a1 + jnp.where(lmask, a01[c:2 * c], 0.0)
        av = _dot(jnp.concatenate([a0, a1], axis=0).astype(BF16), vb)
        o = o + jnp.where(head0, av[0:c], av[c:2 * c])

        o = o + _dot_nt((q * jnp.exp2(b)).astype(BF16), state_t.astype(BF16))
        b_last = b[c - 1:c, :]
        khat = (kk * jnp.exp2(b_last - b)).astype(BF16)
        upd = lax.dot_general(vb, khat, (((0,), (0,)), ((), ())), preferred_element_type=F32)
        state_t = jnp.exp2(b_last) * state_t + jnp.where(same_head, upd, 0.0)

        o2 = o * o
        ms0 = jnp.sum(jnp.where(head0, o2, 0.0), axis=-1, keepdims=True)
        ms1 = jnp.sum(jnp.where(head0, 0.0, o2), axis=-1, keepdims=True)
        o = o * lax.rsqrt(jnp.where(head0, ms0, ms1) * (1.0 / HG_DIM) + EPS)
        return (o * (gt * jax.nn.sigmoid(gt))).astype(BF16), state_t

    def step(ic, states):
        rows = pl.ds(pl.multiple_of(ic * c, c), c)
        out = []
        for p in range(npair):
            cols = slice(p * LANE, (p + 1) * LANE)
            o, st = pair_chunk(q_ref[0, rows, cols], i_ref[0, rows, cols], f_ref[0, rows, cols],
                               g_ref[0, rows, cols], log_lb[:, cols], log_1m[:, cols], states[p])
            o_ref[0, rows, cols] = o
            out.append(st)
        return tuple(out)

    lax.fori_loop(0, seq // c, step, tuple(jnp.zeros((LANE, LANE), F32) for _ in range(npair)),
                  unroll=min(8, seq // c))


def _hgrn2(pf, hg_lb, layer, seq, chunk):
    B = pf.shape[0]
    sec = lambda off: (lambda b: (b, 0, off // HG_W))
    return pl.pallas_call(
        functools.partial(_hgrn2_body, layer=layer, seq=seq, chunk=chunk),
        grid=(B,),
        in_specs=[pl.BlockSpec((1, seq, HG_W), sec(PF_HQ)),
                  pl.BlockSpec((1, seq, HG_W), sec(PF_HF)),
                  pl.BlockSpec((1, seq, HG_W), sec(PF_HI)),
                  pl.BlockSpec((1, seq, HG_W), sec(PF_HG)),
                  pl.BlockSpec((DEPTH, HG_W), lambda b: (0, 0))],
        out_specs=pl.BlockSpec((1, seq, HG_W), lambda b: (b, 0, 0)),
        out_shape=jax.ShapeDtypeStruct((B, seq, HG_W), BF16),
        compiler_params=pltpu.CompilerParams(dimension_semantics=("arbitrary",),
                                             vmem_limit_bytes=VMEM_LIMIT),
        name="hgrn2",
    )(pf, pf, pf, pf, hg_lb)


def _outproj_body(nsa_ref, d0_ref, d1_ref, d2_ref, l0_ref, l1_ref, l2_ref, hg_ref, h_ref,
                  w_ref, g_ref, out_ref, tok_scr, *, tm):
    def token_order(ref, slot, d):
        if d == 1:
            return ref[...].astype(F32)
        n = tm // d
        for r in range(d):
            tok_scr[slot, pl.ds(r, n, stride=d), :] = ref[r * n:(r + 1) * n, :].astype(F32)
        return tok_scr[slot]

    dils = [d for _, d in DIL_PAIRS]
    os_ = [token_order(ref, 2 * g, dils[g]) for g, ref in enumerate((d0_ref, d1_ref, d2_ref))]
    ls = [token_order(ref, 2 * g + 1, dils[g]) for g, ref in enumerate((l0_ref, l1_ref, l2_ref))]
    lm = jnp.maximum(jnp.maximum(ls[0], ls[1]), ls[2])
    es = [jnp.exp(l - lm) for l in ls]
    inv = 1.0 / (es[0] + es[1] + es[2])
    mix = jnp.concatenate([nsa_ref[...]] + [(os_[g] * (es[g] * inv)).astype(BF16) for g in range(len(DIL_PAIRS))]
                          + [hg_ref[...]], axis=1)
    out_ref[...] = h_ref[...] + _rms(_dot(mix, w_ref[...]), g_ref[...])


def _outproj(nsa, dil_o, dil_l, hg, h2, w, g, l, tm):
    T = h2.shape[0]
    row = lambda i: (i, 0)
    return pl.pallas_call(
        functools.partial(_outproj_body, tm=tm),
        grid=(T // tm,),
        in_specs=[pl.BlockSpec((tm, NSA_Q_W), row)]
        + [pl.BlockSpec((tm, DIL_Q_W), row)] * 6
        + [pl.BlockSpec((tm, HG_W), row), pl.BlockSpec((tm, D_MODEL), row),
           _layer((MIX_W, D_MODEL), l), _layer((1, D_MODEL), l)],
        out_specs=pl.BlockSpec((tm, D_MODEL), row),
        out_shape=jax.ShapeDtypeStruct((T, D_MODEL), F32),
        scratch_shapes=[pltpu.VMEM((2 * len(DIL_PAIRS), tm, DIL_Q_W), F32)],
        compiler_params=pltpu.CompilerParams(dimension_semantics=("arbitrary",),
                                             vmem_limit_bytes=VMEM_LIMIT),
        name="outproj",
    )(nsa, *dil_o, *dil_l, hg, h2, w, g)


def _mlp_body(h_ref, p_ref, gpre_ref, wup_ref, wdn_ref, gpost_ref, gple_ref, wg_ref, wp_ref, out_ref, *, fc):
    h = h_ref[...]
    hn = _rms(h, gpre_ref[...]).astype(BF16)
    acc = jnp.zeros(h.shape, F32)
    for c in range(0, D_FF, fc):
        u = jnp.maximum(_dot(hn, wup_ref[:, c:c + fc]), 0.0)
        acc = acc + _dot((u * u).astype(BF16), wdn_ref[c:c + fc, :])
    h = h + _rms(acc, gpost_ref[...])
    gate = jax.nn.sigmoid(_dot(_rms(h, gple_ref[...]).astype(BF16), wg_ref[...]))
    out_ref[...] = h + _dot(p_ref[...].astype(BF16), wp_ref[...]) * gate


def _mlp(h2, p3, gpre, wup, wdn, gpost, gple, wg, wp, l, tm):
    T = h2.shape[0]
    row = lambda i: (i, 0)
    vec = _layer((1, D_MODEL), l)

    def resident(shape):
        return pl.BlockSpec((None,) + tuple(shape), lambda i: (l, 0, 0), pipeline_mode=pl.Buffered(1))

    return pl.pallas_call(
        functools.partial(_mlp_body, fc=512),
        grid=(T // tm,),
        in_specs=[pl.BlockSpec((tm, D_MODEL), row), pl.BlockSpec((None, tm, PLE_DIM), lambda i: (l, i, 0)), vec,
                  resident(wup.shape[1:]), resident(wdn.shape[1:]), vec, vec,
                  resident(wg.shape[1:]), resident(wp.shape[1:])],
        out_specs=pl.BlockSpec((tm, D_MODEL), row),
        out_shape=jax.ShapeDtypeStruct((T, D_MODEL), F32),
        compiler_params=pltpu.CompilerParams(dimension_semantics=("arbitrary",),
                                             vmem_limit_bytes=VMEM_LIMIT),
        name="mlp_ple",
    )(h2, p3, gpre, wup, wdn, gpost, gple, wg, wp)


def _inproj_columns():
    hd_cols = np.arange(HEAD_DIM)
    ca = np.zeros((WA,), np.int64)
    for g in range(NSA_GROUP):
        for kh in range(NSA_KV_HEADS):
            o = PA_NQ + g * LANE + kh * HEAD_DIM
            ca[o:o + HEAD_DIM] = OFF_NQ + (kh * NSA_GROUP + g) * HEAD_DIM + hd_cols
    ca[PA_DQ:PA_DQ + DIL_Q_W] = OFF_DQ + np.arange(DIL_Q_W)
    for kh in range(NSA_KV_HEADS):
        for base, ko, vo in ((PA_KVS, OFF_NKS, OFF_NVS), (PA_KVW, OFF_NKW, OFF_NVW)):
            o = base + kh * LANE
            first, second = (ko, vo) if kh == 0 else (vo, ko)
            ca[o:o + HEAD_DIM] = first + kh * HEAD_DIM + hd_cols
            ca[o + HEAD_DIM:o + LANE] = second + kh * HEAD_DIM + hd_cols

    def dil_kv(dst, base, g):
        for i in range(DIL_HPG):
            hd = g * DIL_HPG + i
            o = base + i * LANE
            first, second = (OFF_DK, OFF_DV) if i == 0 else (OFF_DV, OFF_DK)
            dst[o:o + HEAD_DIM] = first + hd * HEAD_DIM + hd_cols
            dst[o + HEAD_DIM:o + LANE] = second + hd * HEAD_DIM + hd_cols

    dil_kv(ca, PA_DKV, 0)
    cd = np.zeros((len(DIL_PAIRS) - 1, WD), np.int64)
    for g in range(1, len(DIL_PAIRS)):
        dil_kv(cd[g - 1], PD_KV, g)
        cd[g - 1, PD_Q:PD_Q + DIL_Q_W] = OFF_DQ + g * DIL_Q_W + np.arange(DIL_Q_W)
    cf = np.full((WF,), IN_TOTAL, np.int64)
    for dst, src in ((PF_HQ, OFF_HQ), (PF_HF, OFF_HF), (PF_HI, OFF_HI), (PF_HG, OFF_HG)):
        cf[dst:dst + HG_W] = src + np.arange(HG_W)
    cf[PF_GATE:PF_GATE + 3 * NSA_HEADS] = OFF_GATE + np.arange(3 * NSA_HEADS)
    cc = np.concatenate([OFF_NKC + np.arange(NSA_KV_W), OFF_NVC + np.arange(NSA_KV_W)])
    return ca, cd, cf, cc


def _take(w, idx, axis, scale=None):
    idx = np.asarray(idx)
    n_src = w.shape[axis]
    bounds = [0] + [i for i in range(1, len(idx))
                    if (idx[i] != idx[i - 1] + 1 and not (idx[i] == n_src and idx[i - 1] == n_src))
                    or (scale is not None and idx[i] < n_src and idx[i - 1] < n_src and scale[idx[i]] != scale[idx[i - 1]])]
    bounds.append(len(idx))
    pieces = []
    for a, b in zip(bounds[:-1], bounds[1:]):
        src = int(idx[a])
        if src == n_src:
            shape = list(w.shape)
            shape[axis] = b - a
            pieces.append(jnp.zeros(shape, w.dtype))
            continue
        piece = lax.slice_in_dim(w, src, src + (b - a), axis=axis)
        if scale is not None and scale[src] != 1.0:
            piece = piece * float(scale[src])
        pieces.append(piece)
    return jnp.concatenate(pieces, axis=axis)


def _outproj_rows():
    hd_cols = np.arange(HEAD_DIM)
    rows = np.arange(MIX_W)
    for g in range(NSA_GROUP):
        for slot, kh in enumerate((1, 0)):
            o = g * LANE + slot * HEAD_DIM
            rows[o:o + HEAD_DIM] = (kh * NSA_GROUP + g) * HEAD_DIM + hd_cols
    for g in range(len(DIL_PAIRS)):
        for slot, i in enumerate((1, 0)):
            o = NSA_Q_W + g * DIL_Q_W + slot * HEAD_DIM
            rows[o:o + HEAD_DIM] = NSA_Q_W + (g * DIL_HPG + i) * HEAD_DIM + hd_cols
    return rows


def kernel(x, p, w_in, w_out, cmp_pos, cmp_w1, cmp_w2, hg_lb, g_pre_mix, g_post_mix,
           g_pre_mlp, g_post_mlp, w_up, w_down, g_ple, w_ple_gate, w_ple_proj):
    B, S, D = x.shape
    T = B * S
    tm = 1024 if S % 1024 == 0 else 256
    tm_mlp = 512
    tq_nsa, tk_nsa = 256, 256
    tq_dil = 128
    hg_chunk = 128
    assert D == D_MODEL and S % tm == 0 and S // SLC_BLOCK <= LANE
    nsa_slopes, dil_slopes = _alibi_slopes()
    ca, cd, cf, cc = _inproj_columns()
    n_cmp = (S - CMP_BLOCK) // CMP_STRIDE + 1
    nr = S // CMP_STRIDE
    ncp = -(-nr // LANE) * LANE

    key_scale = np.ones((IN_TOTAL,), np.float32)
    key_scale[OFF_NKS:OFF_NKS + NSA_KV_W] = LOG2E
    key_scale[OFF_NKW:OFF_NKW + NSA_KV_W] = LOG2E
    wa = _take(w_in, ca, 2, scale=key_scale).astype(BF16)
    wd_in = jnp.stack([_take(w_in, cd[g], 2) for g in range(cd.shape[0])], axis=1).astype(BF16)
    wf = _take(w_in, cf, 2).astype(BF16)
    wc = _take(w_in, cc, 2).astype(BF16)
    pos8 = jnp.pad(cmp_pos.reshape(DEPTH, 2, 1, CMP_BLOCK * HEAD_DIM), ((0, 0), (0, 0), (0, 7), (0, 0))).astype(BF16)
    lo_pad, hi_pad = ((0, 0), (0, 0), (0, HEAD_DIM)), ((0, 0), (0, 0), (HEAD_DIM, 0))
    w2p = jnp.stack([jnp.stack([jnp.pad(cmp_w2[:, 0], lo_pad), jnp.pad(cmp_w2[:, 1], hi_pad)], axis=1),
                     jnp.stack([jnp.pad(cmp_w2[:, 0], hi_pad), jnp.pad(cmp_w2[:, 1], lo_pad)], axis=1)],
                    axis=1).astype(BF16)
    w1 = cmp_w1.astype(BF16)
    wo =_take(w_out, _outproj_rows(), 1).astype(BF16)
    wup, wdn = w_up.astype(BF16), w_down.astype(BF16)
    wg, wp = w_ple_gate.astype(BF16), w_ple_proj.astype(BF16)
    vec = lambda g: g.reshape(DEPTH, 1, D)
    g_pre_mix, g_post_mix, g_pre_mlp, g_post_mlp, g_ple = map(vec, (g_pre_mix, g_post_mix, g_pre_mlp, g_post_mlp, g_ple))
    p3 = p.reshape(DEPTH, T, PLE_DIM)

    h = x.reshape(T, D)
    for l in range(DEPTH):
        pa, pf, pd1, pd2, *xs = _inproj(h, g_pre_mix, wa, wf, wd_in, wc, l, tm)
        pa3 = pa.reshape(B, S, WA)
        pf3 = pf.reshape(B, S, WF)
        cmp_kv = _compress([a.reshape(B, nr, CMP_ROW_W) for a in xs], pos8, w1, w2p, l, ncp, n_cmp)
        o_nsa = _nsa(pa3, pf3, cmp_kv, S, tq_nsa, tk_nsa, nsa_slopes).reshape(T, NSA_Q_W)
        dil = [_dilated(pa, PA_DQ, PA_DKV, B, S, tm, 0, dil_slopes, tq_dil),
               _dilated(pd1, PD_Q, PD_KV, B, S, tm, 1, dil_slopes, tq_dil),
               _dilated(pd2, PD_Q, PD_KV, B, S, tm, 2, dil_slopes, tq_dil)]
        o_hg = _hgrn2(pf3, hg_lb, l, S, hg_chunk).reshape(T, HG_W)
        h = _outproj(o_nsa, [d[0] for d in dil], [d[1] for d in dil], o_hg, h, wo, g_post_mix, l, tm)
        h = _mlp(h, p3, g_pre_mlp, wup, wdn, g_post_mlp, g_ple, wg, wp, l, tm_mlp)
    return h.reshape(B, S, D)
```
